```python
import math
import jax
import jax.numpy as jnp
from jax import lax
import numpy as np

D_MODEL = 2048
BATCH = 1
SEQ = 8192
DEPTH = 4

GRID_W = 64
CTX_LEN = 256
N_MOD = 9
RMS_EPS = 1e-6
L2_EPS = 1e-6
FFN_HIDDEN = ((8 * D_MODEL // 3 + 127) // 128) * 128

GDN_HEADS = 8
GDN_DK = 128
GDN_DV = 128
GDN_CONV = 5
GDN_CHUNK = 64
GDN_QK_W = GDN_HEADS * GDN_DK
GDN_V_W = GDN_HEADS * GDN_DV

MLSTM_HEADS = 4
MLSTM_DQK = 128
MLSTM_DV = 256
MLSTM_CHUNK = 64
MLSTM_QK_W = MLSTM_HEADS * MLSTM_DQK
MLSTM_V_W = MLSTM_HEADS * MLSTM_DV

HYENA_WIDTH = 1024
HYENA_ORDER = 2
HYENA_SHORT = 3
HYENA_BANDS = 16
HYENA_POS_DIM = 1 + 2 * HYENA_BANDS
HYENA_FILTER_HIDDEN = 64
HYENA_TARGET = 1e-2
HYENA_FAST_DECAY = 0.3
HYENA_SLOW_DECAY = 1.5

N_BRANCH = 3
STATE_SIZES = (2 * GDN_QK_W + GDN_V_W, 2 * GDN_HEADS, 2 * GDN_HEADS, 2 * MLSTM_QK_W + MLSTM_V_W, 2 * MLSTM_HEADS, 2 * MLSTM_HEADS)
OUT_SIZES = (GDN_V_W, MLSTM_V_W, 3 * HYENA_WIDTH, N_BRANCH * D_MODEL)
N_STATE = sum(STATE_SIZES)
N_IN = N_STATE + sum(OUT_SIZES)

kernel_name = 'hybrid_gdn_mlstm_hyena_macaron_dit'


def _rms_norm(x, g):
    xf = x.astype(jnp.float32)
    y = xf * lax.rsqrt(jnp.mean(xf * xf, axis=-1, keepdims=True) + RMS_EPS)
    return (y * g.astype(jnp.float32)).astype(x.dtype)


def _l2norm(x):
    return x * lax.rsqrt(jnp.sum(x * x, axis=-1, keepdims=True) + L2_EPS)


def _modulation(cond, w, b):
    m = jax.nn.silu(cond) @ w + b
    m = m.reshape(cond.shape[0], N_MOD, 1, D_MODEL)
    return [m[:, i] for i in range(N_MOD)]


def _modulate(x, g, shift, scale):
    return _rms_norm(x, g) * (1 + scale) + shift


def _swiglu(h, w_gate, w_up, w_down):
    return (jax.nn.silu(h @ w_gate) * (h @ w_up)) @ w_down


def _split(t, sizes):
    return jnp.split(t, np.cumsum(sizes)[:-1].tolist(), axis=-1)


def _dwconv(x, w):
    k = w.shape[-1]
    filt = jnp.transpose(w)[:, None, :].astype(x.dtype)
    return lax.conv_general_dilated(x, filt, window_strides=(1,), padding=[(k // 2, k // 2)],
                                    dimension_numbers=('NWC', 'WIO', 'NWC'), feature_group_count=x.shape[-1])


def _to_chunks(t, chunk):
    b, l = t.shape[:2]
    t = t.reshape((b, l // chunk, chunk) + t.shape[2:])
    return jnp.swapaxes(jnp.moveaxis(t, 1, 0), 2, 3)


def _from_chunks(o, b, l):
    o = jnp.moveaxis(jnp.swapaxes(o, 2, 3), 0, 1)
    return o.reshape((b, l) + o.shape[3:])


def _gdn_scan(q, k, v, g, beta, state0):
    b, l, h, dk = q.shape
    dv = v.shape[-1]
    c = GDN_CHUNK
    qc, kc, vc, gc, bc = (_to_chunks(t, c) for t in (q, k, v, g, beta))
    gcum = jnp.cumsum(gc, axis=-1)
    incl = jnp.tril(jnp.ones((c, c), dtype=bool))
    strict = jnp.tril(jnp.ones((c, c), dtype=bool), -1)
    decay = jnp.exp(jnp.where(incl, gcum[..., :, None] - gcum[..., None, :], -jnp.inf))
    kb = kc * bc[..., None]
    a = jnp.where(strict, jnp.einsum('nbhid,nbhjd->nbhij', kb, kc) * decay, 0.0)
    rhs = jnp.concatenate([vc * bc[..., None], kb * jnp.exp(gcum)[..., None]], axis=-1)
    sol = lax.linalg.triangular_solve(a, rhs, left_side=True, lower=True, unit_diagonal=True)
    u, w = sol[..., :dv], sol[..., dv:]
    qk = jnp.einsum('nbhid,nbhjd->nbhij', qc, kc) * decay
    q_dec = qc * jnp.exp(gcum)[..., None]
    k_end = kc * jnp.exp(gcum[..., -1:] - gcum)[..., None]
    g_end = jnp.exp(gcum[..., -1])

    def step(s, xs):
        q_i, qk_i, u_i, w_i, k_i, g_i = xs
        v_new = u_i - jnp.einsum('bhck,bhkv->bhcv', w_i, s)
        o = jnp.einsum('bhck,bhkv->bhcv', q_i, s) + jnp.einsum('bhij,bhjv->bhiv', qk_i, v_new)
        s = s * g_i[..., None, None] + jnp.einsum('bhck,bhcv->bhkv', k_i, v_new)
        return s, o

    s_fin, o = lax.scan(step, state0, (q_dec, qk, u, w, k_end, g_end))
    return _from_chunks(o, b, l), s_fin


def _mlstm_scan(q, k, v, log_i, log_f, state0):
    b, l, h, dqk = q.shape
    c = MLSTM_CHUNK
    qc, kc, vc, ic, fc = (_to_chunks(t, c) for t in (q, k, v, log_i, log_f))
    bcum = jnp.cumsum(fc, axis=-1)
    incl = jnp.tril(jnp.ones((c, c), dtype=bool))
    log_d = jnp.where(incl, bcum[..., :, None] - bcum[..., None, :] + ic[..., None, :], -jnp.inf)
    m_intra = jnp.max(log_d, axis=-1)
    log_end = bcum[..., -1:] - bcum + ic
    m_end = jnp.max(log_end, axis=-1)
    qk = jnp.einsum('nbhid,nbhjd->nbhij', qc, kc)

    def step(state, xs):
        c_st, n_st, m_st = state
        q_i, k_i, v_i, qk_i, ld_i, mi_i, b_i, le_i, me_i = xs
        m_t = jnp.maximum(b_i + m_st[..., None], mi_i)
        w_inter = jnp.exp(b_i + m_st[..., None] - m_t)
        s = qk_i * jnp.exp(ld_i - m_t[..., None])
        num = w_inter[..., None] * jnp.einsum('bhcd,bhde->bhce', q_i, c_st) + jnp.einsum('bhij,bhje->bhie', s, v_i)
        den = w_inter * jnp.einsum('bhcd,bhd->bhc', q_i, n_st) + jnp.sum(s, axis=-1)
        h_out = num / jnp.maximum(jnp.abs(den), jnp.exp(-m_t))[..., None]
        b_last = b_i[..., -1]
        m_new = jnp.maximum(b_last + m_st, me_i)
        carry = jnp.exp(b_last + m_st - m_new)
        wk = k_i * jnp.exp(le_i - m_new[..., None])[..., None]
        c_new = carry[..., None, None] * c_st + jnp.einsum('bhcd,bhce->bhde', wk, v_i)
        n_new = carry[..., None] * n_st + jnp.sum(wk, axis=2)
        return (c_new, n_new, m_new), h_out

    state, hs = lax.scan(step, state0, (qc, kc, vc, qk, log_d, m_intra, bcum, log_end, m_end))
    return _from_chunks(hs, b, l), state


def _flip(t):
    return jnp.flip(t, axis=1)


def _ident(t):
    return t


def _bidirectional(scan_fn, ctx_dirs, lat_dirs, state0):
    outs_ctx, outs_lat = [], []
    for d in range(2):
        rev = _flip if d == 1 else _ident
        o_ctx, st_ctx = scan_fn(*[rev(t) for t in ctx_dirs[d]], state0)
        o_lat, _ = scan_fn(*[rev(t) for t in lat_dirs[d]], st_ctx)
        outs_ctx.append(rev(o_ctx))
        outs_lat.append(rev(o_lat))
    return outs_ctx[0] + outs_ctx[1], outs_lat[0] + outs_lat[1]


def _state_inputs(st, p):
    f32 = jnp.float32
    b, l, _ = st.shape
    g_qkv, g_beta, g_a, m_qkv, m_i, m_f = _split(st.astype(f32), STATE_SIZES)
    g_qkv = jax.nn.silu(_dwconv(g_qkv, p['gdn_conv']))
    q, k, v = _split(g_qkv, (GDN_QK_W, GDN_QK_W, GDN_V_W))
    q = _l2norm(q.reshape(b, l, GDN_HEADS, GDN_DK)) * (GDN_DK ** -0.5)
    k = _l2norm(k.reshape(b, l, GDN_HEADS, GDN_DK))
    v = v.reshape(b, l, GDN_HEADS, GDN_DV)
    beta = jax.nn.sigmoid(g_beta).reshape(b, l, 2, GDN_HEADS)
    g = -jnp.exp(p['gdn_a_log'].astype(f32)) * jax.nn.softplus(g_a.reshape(b, l, 2, GDN_HEADS) + p['gdn_dt_bias'].astype(f32))
    gdn = tuple((q, k, v, g[:, :, d], beta[:, :, d]) for d in range(2))
    mq, mk, mv = _split(m_qkv, (MLSTM_QK_W, MLSTM_QK_W, MLSTM_V_W))
    mq = mq.reshape(b, l, MLSTM_HEADS, MLSTM_DQK) * (MLSTM_DQK ** -0.5)
    mk = mk.reshape(b, l, MLSTM_HEADS, MLSTM_DQK)
    mv = mv.reshape(b, l, MLSTM_HEADS, MLSTM_DV)
    log_i = m_i.reshape(b, l, 2, MLSTM_HEADS) + p['mlstm_i_bias'].astype(f32)
    log_f = jax.nn.log_sigmoid(m_f.reshape(b, l, 2, MLSTM_HEADS) + p['mlstm_f_bias'].astype(f32))
    mls = tuple((mq, mk, mv, log_i[:, :, d], log_f[:, :, d]) for d in range(2))
    return gdn, mls


def _hyena_filter_rfft(length, p):
    f32 = jnp.float32
    t = jnp.linspace(0.0, 1.0, length, dtype=f32)[:, None]
    w = (2.0 * math.pi / length) * jnp.arange(length, dtype=f32)[:, None]
    bands = jnp.linspace(1e-4, HYENA_BANDS - 1, HYENA_BANDS, dtype=f32)[None, :]
    feats = jnp.concatenate([t, jnp.cos(bands * w), -jnp.sin(bands * w)], axis=-1)
    freq = p['hf_freq'].astype(f32)
    h = jnp.sin(freq[0] * (feats @ p['hf_w1'].astype(f32) + p['hf_b1'].astype(f32)))
    h = jnp.sin(freq[1] * (h @ p['hf_w2'].astype(f32) + p['hf_b2'].astype(f32)))
    h = (h @ p['hf_w3'].astype(f32)).reshape(length, 2, HYENA_ORDER, HYENA_WIDTH)
    decay = jnp.abs(p['hf_decay'].astype(f32)).reshape(HYENA_ORDER, HYENA_WIDTH)
    h = h * jnp.exp(-t[:, :, None, None] * decay)
    fwd, bwd = h[:, 0], h[:, 1]
    taps = jnp.concatenate([fwd[:1] + bwd[:1], fwd[1:], jnp.zeros_like(fwd[:1]), jnp.flip(bwd[1:], axis=0)], axis=0)
    return jnp.fft.rfft(taps, axis=0)


def _fft_conv(u, kf, bias):
    n = 2 * u.shape[1]
    y = jnp.fft.irfft(jnp.fft.rfft(u, n=n, axis=1) * kf, n=n, axis=1)[:, :u.shape[1]]
    return y + u * bias


def _hyena(u, p):
    f32 = jnp.float32
    u = _dwconv(u.astype(f32), p['hy_short_w']) + p['hy_short_b'].astype(f32)
    v, x1, x2 = jnp.split(u, 3, axis=-1)
    kf = _hyena_filter_rfft(u.shape[1], p)
    bias = p['hy_bias'].astype(f32)
    z = x1 * _fft_conv(v, kf[:, 0], bias[0])
    z = x2 * _fft_conv(z, kf[:, 1], bias[1])
    return z


def _merge_branches(y_gdn, y_mlstm, cols, p, dtype):
    f32 = jnp.float32
    b, l, _ = cols.shape
    z, o, hy, gates = _split(cols, OUT_SIZES)
    a = _rms_norm(y_gdn, p['gdn_norm']) * jax.nn.silu(z.astype(f32)).reshape(b, l, GDN_HEADS, GDN_DV)
    m = _rms_norm(y_mlstm, p['mlstm_norm'].reshape(MLSTM_HEADS, MLSTM_DV)) * jax.nn.sigmoid(o.astype(f32)).reshape(b, l, MLSTM_HEADS, MLSTM_DV)
    y_a = a.reshape(b, l, GDN_V_W).astype(dtype) @ p['w_a_out']
    y_b = m.reshape(b, l, MLSTM_V_W).astype(dtype) @ p['w_b_out']
    y_c = _hyena(hy, p).astype(dtype) @ p['w_c_out']
    g = jax.nn.sigmoid(gates.astype(f32)).astype(dtype).reshape(b, l, N_BRANCH, D_MODEL)
    merged = g[:, :, 0] * y_a + g[:, :, 1] * y_b + g[:, :, 2] * y_c
    return merged @ p['w_out']


def _hybrid_mixer(h_lat, h_ctx, p, need_ctx):
    f32 = jnp.float32
    dtype = h_lat.dtype
    b = h_lat.shape[0]
    p_lat = h_lat @ p['w_in']
    p_ctx = h_ctx @ (p['w_in'] if need_ctx else p['w_in'][:, :N_STATE])
    gdn_c, ml_c = _state_inputs(p_ctx[..., :N_STATE], p)
    gdn_l, ml_l = _state_inputs(p_lat[..., :N_STATE], p)
    s0 = jnp.zeros((b, GDN_HEADS, GDN_DK, GDN_DV), f32)
    m0 = (jnp.zeros((b, MLSTM_HEADS, MLSTM_DQK, MLSTM_DV), f32), jnp.zeros((b, MLSTM_HEADS, MLSTM_DQK), f32),
          jnp.zeros((b, MLSTM_HEADS), f32))
    a_ctx, a_lat = _bidirectional(_gdn_scan, gdn_c, gdn_l, s0)
    b_ctx, b_lat = _bidirectional(_mlstm_scan, ml_c, ml_l, m0)
    y_lat = _merge_branches(a_lat, b_lat, p_lat[..., N_STATE:], p, dtype)
    y_ctx = _merge_branches(a_ctx, b_ctx, p_ctx[..., N_STATE:], p, dtype) if need_ctx else None
    return y_lat, y_ctx


def setup_inputs(seed: int = 0) -> dict:
    key = jax.random.key(seed)
    ks = iter(jax.random.split(key, 40))
    f32 = jnp.float32

    def nrm(shape, scale):
        return jax.random.normal(next(ks), shape, f32) * scale

    x = nrm((BATCH, SEQ, D_MODEL), 1.0)
    c = nrm((BATCH, D_MODEL), 1.0)
    ctx = nrm((BATCH, CTX_LEN, D_MODEL), 1.0)
    c_ctx = nrm((D_MODEL,), 1.0)
    w_ada = nrm((DEPTH, D_MODEL, N_MOD * D_MODEL), 0.5 * D_MODEL ** -0.5)
    b_ada = nrm((DEPTH, N_MOD * D_MODEL), 0.02)
    norm_g = 1.0 + nrm((DEPTH, 3, D_MODEL), 0.02)
    ffn_w_gate = nrm((DEPTH, 2, D_MODEL, FFN_HIDDEN), D_MODEL ** -0.5)
    ffn_w_up = nrm((DEPTH, 2, D_MODEL, FFN_HIDDEN), D_MODEL ** -0.5)
    ffn_w_down = nrm((DEPTH, 2, FFN_HIDDEN, D_MODEL), FFN_HIDDEN ** -0.5)
    w_in = nrm((DEPTH, D_MODEL, N_IN), D_MODEL ** -0.5)
    gdn_conv = nrm((DEPTH, 2 * GDN_QK_W + GDN_V_W, GDN_CONV), GDN_CONV ** -0.5)
    gdn_a_log = jnp.log(jax.random.uniform(next(ks), (DEPTH, 2, GDN_HEADS), f32, 1.0, 16.0))
    dt = jnp.exp(jax.random.uniform(next(ks), (DEPTH, 2, GDN_HEADS), f32, math.log(1e-3), math.log(1e-1)))
    gdn_dt_bias = dt + jnp.log(-jnp.expm1(-dt))
    gdn_norm = 1.0 + nrm((DEPTH, GDN_DV), 0.02)
    mlstm_i_bias = nrm((DEPTH, 2, MLSTM_HEADS), 0.1)
    mlstm_f_bias = jnp.linspace(3.0, 6.0, MLSTM_HEADS, dtype=f32) + nrm((DEPTH, 2, MLSTM_HEADS), 0.1)
    mlstm_norm = 1.0 + nrm((DEPTH, MLSTM_V_W), 0.02)
    hy_short_w = nrm((DEPTH, 3 * HYENA_WIDTH, HYENA_SHORT), HYENA_SHORT ** -0.5)
    hy_short_b = nrm((DEPTH, 3 * HYENA_WIDTH), 0.02)
    hf_w1 = nrm((DEPTH, HYENA_POS_DIM, HYENA_FILTER_HIDDEN), HYENA_POS_DIM ** -0.5)
    hf_b1 = nrm((DEPTH, HYENA_FILTER_HIDDEN), 0.1)
    hf_w2 = nrm((DEPTH, HYENA_FILTER_HIDDEN, HYENA_FILTER_HIDDEN), HYENA_FILTER_HIDDEN ** -0.5)
    hf_b2 = nrm((DEPTH, HYENA_FILTER_HIDDEN), 0.1)
    hf_w3 = nrm((DEPTH, HYENA_FILTER_HIDDEN, 2 * HYENA_ORDER * HYENA_WIDTH), 0.005)
    hf_freq = 1.0 + nrm((DEPTH, 2, HYENA_FILTER_HIDDEN), 0.1)
    min_decay = math.log(HYENA_TARGET) / HYENA_SLOW_DECAY
    max_decay = math.log(HYENA_TARGET) / HYENA_FAST_DECAY
    hf_decay = jnp.linspace(min_decay, max_decay, HYENA_ORDER * HYENA_WIDTH, dtype=f32) + nrm((DEPTH, HYENA_ORDER * HYENA_WIDTH), 0.05)
    hy_bias = nrm((DEPTH, HYENA_ORDER, HYENA_WIDTH), 0.5)
    w_a_out = nrm((DEPTH, GDN_V_W, D_MODEL), GDN_V_W ** -0.5)
    w_b_out = nrm((DEPTH, MLSTM_V_W, D_MODEL), MLSTM_V_W ** -0.5)
    w_c_out = nrm((DEPTH, HYENA_WIDTH, D_MODEL), HYENA_WIDTH ** -0.5)
    w_out = nrm((DEPTH, D_MODEL, D_MODEL), D_MODEL ** -0.5)
    final_norm = 1.0 + nrm((D_MODEL,), 0.02)
    return {'x': x, 'c': c, 'ctx': ctx, 'c_ctx': c_ctx, 'w_ada': w_ada, 'b_ada': b_ada, 'norm_g': norm_g,
            'ffn_w_gate': ffn_w_gate, 'ffn_w_up': ffn_w_up, 'ffn_w_down': ffn_w_down, 'w_in': w_in,
            'gdn_conv': gdn_conv, 'gdn_a_log': gdn_a_log, 'gdn_dt_bias': gdn_dt_bias, 'gdn_norm': gdn_norm,
            'mlstm_i_bias': mlstm_i_bias, 'mlstm_f_bias': mlstm_f_bias, 'mlstm_norm': mlstm_norm,
            'hy_short_w': hy_short_w, 'hy_short_b': hy_short_b, 'hf_w1': hf_w1, 'hf_b1': hf_b1, 'hf_w2': hf_w2,
            'hf_b2': hf_b2, 'hf_w3': hf_w3, 'hf_freq': hf_freq, 'hf_decay': hf_decay, 'hy_bias': hy_bias,
            'w_a_out': w_a_out, 'w_b_out': w_b_out, 'w_c_out': w_c_out, 'w_out': w_out, 'final_norm': final_norm}


def reference(x, c, ctx, c_ctx, w_ada, b_ada, norm_g, ffn_w_gate, ffn_w_up, ffn_w_down, w_in, gdn_conv, gdn_a_log,
              gdn_dt_bias, gdn_norm, mlstm_i_bias, mlstm_f_bias, mlstm_norm, hy_short_w, hy_short_b, hf_w1, hf_b1,
              hf_w2, hf_b2, hf_w3, hf_freq, hf_decay, hy_bias, w_a_out, w_b_out, w_c_out, w_out, final_norm):
    s = ctx
    for l in range(DEPTH):
        last = l == DEPTH - 1
        p = {'w_in': w_in[l], 'gdn_conv': gdn_conv[l], 'gdn_a_log': gdn_a_log[l], 'gdn_dt_bias': gdn_dt_bias[l],
             'gdn_norm': gdn_norm[l], 'mlstm_i_bias': mlstm_i_bias[l], 'mlstm_f_bias': mlstm_f_bias[l],
             'mlstm_norm': mlstm_norm[l], 'hy_short_w': hy_short_w[l], 'hy_short_b': hy_short_b[l],
             'hf_w1': hf_w1[l], 'hf_b1': hf_b1[l], 'hf_w2': hf_w2[l], 'hf_b2': hf_b2[l], 'hf_w3': hf_w3[l],
             'hf_freq': hf_freq[l], 'hf_decay': hf_decay[l], 'hy_bias': hy_bias[l], 'w_a_out': w_a_out[l],
             'w_b_out': w_b_out[l], 'w_c_out': w_c_out[l], 'w_out': w_out[l]}
        ml = _modulation(c, w_ada[l], b_ada[l])
        mc = _modulation(c_ctx[None], w_ada[l], b_ada[l])
        x = x + 0.5 * ml[2] * _swiglu(_modulate(x, norm_g[l, 0], ml[0], ml[1]), ffn_w_gate[l, 0], ffn_w_up[l, 0], ffn_w_down[l, 0])
        s = s + 0.5 * mc[2] * _swiglu(_modulate(s, norm_g[l, 0], mc[0], mc[1]), ffn_w_gate[l, 0], ffn_w_up[l, 0], ffn_w_down[l, 0])
        y_lat, y_ctx = _hybrid_mixer(_modulate(x, norm_g[l, 1], ml[3], ml[4]), _modulate(s, norm_g[l, 1], mc[3], mc[4]), p, not last)
        x = x + ml[5] * y_lat
        x = x + 0.5 * ml[8] * _swiglu(_modulate(x, norm_g[l, 2], ml[6], ml[7]), ffn_w_gate[l, 1], ffn_w_up[l, 1], ffn_w_down[l, 1])
        if not last:
            s = s + mc[5] * y_ctx
            s = s + 0.5 * mc[8] * _swiglu(_modulate(s, norm_g[l, 2], mc[6], mc[7]), ffn_w_gate[l, 1], ffn_w_up[l, 1], ffn_w_down[l, 1])
    return _rms_norm(x, final_norm)
```

```python
import functools
import math

import jax
import jax.numpy as jnp
import numpy as np
from jax import lax
from jax.experimental import pallas as pl
from jax.experimental.pallas import tpu as pltpu

F32 = jnp.float32
BF16 = jnp.bfloat16

N_MOD = 9
RMS_EPS = 1e-6
L2_EPS = 1e-6
CHUNK = 64
GDN_CONV = 5
HYENA_SHORT = 3
HYENA_BANDS = 16
LANES = 128
SUBLANES = 8
DFT_N2 = 128
VMEM_LIMIT = 56 * 1024 * 1024


def _cparams(n_axes):
    return pltpu.CompilerParams(dimension_semantics=("arbitrary",) * n_axes,
                                vmem_limit_bytes=VMEM_LIMIT)


def _sigmoid(x):
    return jax.nn.sigmoid(x)


def _silu(x):
    return x * jax.nn.sigmoid(x)


def _softplus(x):
    return jnp.maximum(x, 0.0) + jnp.log(1.0 + jnp.exp(-jnp.abs(x)))


def _dot(a, b):
    return jnp.dot(a.astype(BF16), b.astype(BF16), preferred_element_type=F32)


def _dot_nt(a, b):
    return lax.dot_general(a.astype(BF16), b.astype(BF16), (((1,), (1,)), ((), ())),
                           preferred_element_type=F32)


def _dot_tn(a, b):
    return lax.dot_general(a.astype(BF16), b.astype(BF16), (((0,), (0,)), ((), ())),
                           preferred_element_type=F32)


def _dot_exact(a, b):
    return jnp.dot(a, b, preferred_element_type=F32, precision=lax.Precision.HIGHEST)


def _row_is_ctx(row0, rows, ctx_len):
    r = row0 + lax.broadcasted_iota(jnp.int32, (rows, 1), 0)
    return r < ctx_len


def _modnorm(x, g, shift, scale):
    y = x * lax.rsqrt(jnp.mean(x * x, axis=-1, keepdims=True) + RMS_EPS) * g
    return y * (1.0 + scale) + shift


ROW_CHUNK = 128


def _for_row_chunks(tm, fn):
    rc = math.gcd(tm, ROW_CHUNK)

    def body(c, carry):
        fn(pl.multiple_of(c * rc, rc), rc)
        return carry

    lax.fori_loop(0, tm // rc, body, 0)


def _store_modulated(x_ref, m_ref, g_ref, h_sc, row0, tm, ctx_len):
    def chunk(r0, rc):
        is_ctx = _row_is_ctx(row0 + r0, rc, ctx_len)
        shift = jnp.where(is_ctx, m_ref[0:1, :], m_ref[1:2, :])
        scale = jnp.where(is_ctx, m_ref[2:3, :], m_ref[3:4, :])
        h_sc[pl.ds(r0, rc), :] = _modnorm(x_ref[pl.ds(r0, rc), :], g_ref[...], shift, scale).astype(h_sc.dtype)

    _for_row_chunks(tm, chunk)


def _mod_kernel(c_ref, w_ref, b_ref, o_ref):
    o_ref[0] = _dot(_silu(c_ref[...]), w_ref[0]) + b_ref[0]


def _modulation_call(cond, w_ada, b_ada):
    depth, d, n = w_ada.shape
    tn = _pick_tile(n, (1024, 512, 256, 128))
    return pl.pallas_call(
        _mod_kernel,
        grid=(depth, n // tn),
        in_specs=[pl.BlockSpec((SUBLANES, d), lambda l, j: (0, 0)),
                  pl.BlockSpec((1, d, tn), lambda l, j: (l, 0, j)),
                  pl.BlockSpec((1, 1, tn), lambda l, j: (l, 0, j))],
        out_specs=pl.BlockSpec((1, SUBLANES, tn), lambda l, j: (l, 0, j)),
        out_shape=jax.ShapeDtypeStruct((depth, SUBLANES, n), F32),
        compiler_params=_cparams(2),
        name="adaln_modulation",
    )(cond, w_ada, b_ada.reshape(depth, 1, n))


def _ffn_kernel(x_ref, m_ref, g_ref, wg_ref, wu_ref, wd_ref, o_ref, h_sc, acc_sc, *, tm, ctx_len):
    row0 = pl.program_id(0) * tm
    f = pl.program_id(1)

    @pl.when(f == 0)
    def _():
        _store_modulated(x_ref, m_ref, g_ref, h_sc, row0, tm, ctx_len)
        acc_sc[...] = jnp.zeros_like(acc_sc)

    h = h_sc[...]
    a = _silu(_dot(h, wg_ref[...])) * _dot(h, wu_ref[...])
    acc_sc[...] += _dot(a, wd_ref[...])

    @pl.when(f == pl.num_programs(1) - 1)
    def _():
        def chunk(r0, rc):
            gate = jnp.where(_row_is_ctx(row0 + r0, rc, ctx_len), m_ref[4:5, :], m_ref[5:6, :])
            rows = pl.ds(r0, rc)
            o_ref[rows, :] = x_ref[rows, :] + 0.5 * gate * acc_sc[rows, :]

        _for_row_chunks(tm, chunk)


def _ffn_call(x, mods, g, wg, wu, wd, layer, which, ctx_len, tm, tf):
    m, d = x.shape
    fp = wg.shape[-1]
    return pl.pallas_call(
        functools.partial(_ffn_kernel, tm=tm, ctx_len=ctx_len),
        grid=(m // tm, fp // tf),
        in_specs=[pl.BlockSpec((tm, d), lambda i, f: (i, 0)),
                  pl.BlockSpec((SUBLANES, d), lambda i, f: (0, 0)),
                  pl.BlockSpec((1, d), lambda i, f: (0, 0)),
                  pl.BlockSpec((None, None, d, tf), lambda i, f: (layer, which, 0, f)),
                  pl.BlockSpec((None, None, d, tf), lambda i, f: (layer, which, 0, f)),
                  pl.BlockSpec((None, None, tf, d), lambda i, f: (layer, which, f, 0))],
        out_specs=pl.BlockSpec((tm, d), lambda i, f: (i, 0)),
        out_shape=jax.ShapeDtypeStruct((m, d), F32),
        scratch_shapes=[pltpu.VMEM((tm, d), BF16), pltpu.VMEM((tm, d), F32)],
        compiler_params=_cparams(2),
        name="macaron_swiglu",
    )(x, mods, g, wg, wu, wd)


def _inproj_kernel(x_ref, m_ref, g_ref, w_ref, o_ref, h_sc, *, tm, ctx_len):
    @pl.when(pl.program_id(1) == 0)
    def _():
        _store_modulated(x_ref, m_ref, g_ref, h_sc, pl.program_id(0) * tm, tm, ctx_len)

    o_ref[...] = _dot(h_sc[...], w_ref[...])


def _inproj_call(x, mods, g, w, ctx_len, tm, tn):
    m, d = x.shape
    n = w.shape[1]
    return pl.pallas_call(
        functools.partial(_inproj_kernel, tm=tm, ctx_len=ctx_len),
        grid=(m // tm, n // tn),
        in_specs=[pl.BlockSpec((tm, d), lambda i, j: (i, 0)),
                  pl.BlockSpec((SUBLANES, d), lambda i, j: (0, 0)),
                  pl.BlockSpec((1, d), lambda i, j: (0, 0)),
                  pl.BlockSpec((d, tn), lambda i, j: (0, j))],
        out_specs=pl.BlockSpec((tm, tn), lambda i, j: (i, j)),
        out_shape=jax.ShapeDtypeStruct((m, n), F32),
        scratch_shapes=[pltpu.VMEM((tm, d), BF16)],
        compiler_params=_cparams(2),
        name="mixer_in_proj",
    )(x, mods, g, w)


def _dwconv_kernel(prev_ref, cur_ref, next_ref, w_ref, b_ref, o_ref, ext_sc, *, tm, taps, ctx_len, m_rows, act):
    i = pl.program_id(0)
    ext_sc[0:SUBLANES, :] = prev_ref[...]
    ext_sc[SUBLANES:SUBLANES + tm, :] = cur_ref[...]
    ext_sc[SUBLANES + tm:2 * SUBLANES + tm, :] = next_ref[...]
    r = i * tm + lax.broadcasted_iota(jnp.int32, (tm, 1), 0)
    seg_lo = jnp.where(r < ctx_len, 0, ctx_len)
    seg_hi = jnp.where(r < ctx_len, ctx_len, m_rows)
    acc = jnp.zeros(o_ref.shape, F32) + b_ref[...]
    for t in range(taps):
        off = t - taps // 2
        xs = ext_sc[pl.ds(SUBLANES + off, tm), :]
        valid = jnp.logical_and(r + off >= seg_lo, r + off < seg_hi)
        acc = acc + jnp.where(valid, xs, 0.0) * w_ref[t:t + 1, :]
    o_ref[...] = _silu(acc) if act else acc


def _dwconv_call(p, col0, ncols, w_t, b, ctx_len, tm, tc, act):
    m = p.shape[0]
    taps = w_t.shape[0]
    cb0 = col0 // tc
    rb = tm // SUBLANES
    last = m // SUBLANES - 1
    return pl.pallas_call(
        functools.partial(_dwconv_kernel, tm=tm, taps=taps, ctx_len=ctx_len, m_rows=m, act=act),
        grid=(m // tm, ncols // tc),
        in_specs=[pl.BlockSpec((SUBLANES, tc), lambda i, j: (jnp.maximum(i * rb - 1, 0), cb0 + j)),
                  pl.BlockSpec((tm, tc), lambda i, j: (i, cb0 + j)),
                  pl.BlockSpec((SUBLANES, tc), lambda i, j: (jnp.minimum((i + 1) * rb, last), cb0 + j)),
                  pl.BlockSpec((taps, tc), lambda i, j: (0, j)),
                  pl.BlockSpec((1, tc), lambda i, j: (0, j))],
        out_specs=pl.BlockSpec((tm, tc), lambda i, j: (i, j)),
        out_shape=jax.ShapeDtypeStruct((m, ncols), F32),
        scratch_shapes=[pltpu.VMEM((tm + 2 * SUBLANES, tc), F32)],
        compiler_params=_cparams(2),
        name="depthwise_conv",
    )(p, p, p, w_t, b)


def _scan_chunk(d, s, nc_ctx, nc_tot):
    bwd = jnp.where(s < nc_ctx, nc_ctx - 1 - s, nc_tot - 1 - (s - nc_ctx))
    return jnp.where(d == 0, s, bwd)


def _causal_masks(d):
    ii = lax.broadcasted_iota(jnp.int32, (CHUNK, CHUNK), 0)
    jj = lax.broadcasted_iota(jnp.int32, (CHUNK, CHUNK), 1)
    rel = jnp.where(d == 0, ii - jj, jj - ii)
    return ii, jj, rel >= 0, rel > 0


def _unit_triangular_inverse(a, ii, jj):
    eye = (ii == jj).astype(F32)

    def same_block(width):
        return (ii // width) == (jj // width)

    b8 = same_block(8)
    d1 = jnp.where(b8, a, 0.0)
    d2 = _dot(d1, d1)
    d4 = _dot(d2, d2)
    p = _dot(_dot(eye - d1, eye + d2), eye + d4)
    inner = b8
    for width in (16, 32, 64):
        outer = same_block(width)
        off = jnp.where(jnp.logical_and(outer, jnp.logical_not(inner)), a, 0.0)
        p = p - _dot(_dot(p, off), p)
        inner = outer
    return p


def _gdn_scan_kernel(q_ref, k_ref, v_ref, bt_ref, a_ref, alog_ref, dtb_ref, o_ref, s_sc, *, heads, dk, dv):
    d = pl.program_id(0)

    @pl.when(pl.program_id(1) == 0)
    def _():
        s_sc[...] = jnp.zeros_like(s_sc)

    ii, jj, incl, strict = _causal_masks(d)
    beta = _sigmoid(bt_ref[...])
    glog = -jnp.exp(alog_ref[0]) * _softplus(a_ref[...] + dtb_ref[0])
    gcum = _dot_exact(incl.astype(F32), glog)
    gtot = jnp.where(d == 0, gcum[CHUNK - 1:CHUNK, :], gcum[0:1, :])
    e_in = jnp.exp(gcum)
    e_out = jnp.exp(gtot - gcum)
    g_end = jnp.exp(gtot)
    gcum_t = gcum.T
    for h in range(heads):
        q = q_ref[:, h * dk:(h + 1) * dk]
        k = k_ref[:, h * dk:(h + 1) * dk]
        v = v_ref[:, h * dv:(h + 1) * dv]
        q = q * lax.rsqrt(jnp.sum(q * q, axis=-1, keepdims=True) + L2_EPS) * (dk ** -0.5)
        k = k * lax.rsqrt(jnp.sum(k * k, axis=-1, keepdims=True) + L2_EPS)
        b_col = beta[:, h:h + 1]
        decay = jnp.exp(jnp.where(incl, gcum[:, h:h + 1] - gcum_t[h:h + 1, :], -jnp.inf))
        kb = k * b_col
        a = jnp.where(strict, _dot_nt(kb, k) * decay, 0.0)
        t_inv = _unit_triangular_inverse(a, ii, jj)
        sol = _dot(t_inv, jnp.concatenate([v * b_col, kb * e_in[:, h:h + 1]], axis=1))
        u, w = sol[:, :dv], sol[:, dv:]
        qk = _dot_nt(q, k) * decay
        state = s_sc[h]
        v_new = u - _dot(w, state)
        o_ref[0, :, h * dv:(h + 1) * dv] = _dot(q * e_in[:, h:h + 1], state) + _dot(qk, v_new)
        s_sc[h] = state * g_end[:, h:h + 1] + _dot_tn(k * e_out[:, h:h + 1], v_new)


def _gdn_scan_call(qkv, p, beta_cb, alog, dtb, heads, dk, dv, nc_ctx):
    m = qkv.shape[0]
    nc = m // CHUNK
    qw, vw = heads * dk, heads * dv
    assert qw == vw
    cidx = functools.partial(_scan_chunk, nc_ctx=nc_ctx, nc_tot=nc)
    return pl.pallas_call(
        functools.partial(_gdn_scan_kernel, heads=heads, dk=dk, dv=dv),
        grid=(2, nc),
        in_specs=[pl.BlockSpec((CHUNK, qw), lambda d, s: (cidx(d, s), 0)),
                  pl.BlockSpec((CHUNK, qw), lambda d, s: (cidx(d, s), 1)),
                  pl.BlockSpec((CHUNK, vw), lambda d, s: (cidx(d, s), 2)),
                  pl.BlockSpec((CHUNK, LANES), lambda d, s: (cidx(d, s), beta_cb + 2 * d)),
                  pl.BlockSpec((CHUNK, LANES), lambda d, s: (cidx(d, s), beta_cb + 2 * d + 1)),
                  pl.BlockSpec((1, 1, LANES), lambda d, s: (d, 0, 0)),
                  pl.BlockSpec((1, 1, LANES), lambda d, s: (d, 0, 0))],
        out_specs=pl.BlockSpec((1, CHUNK, vw), lambda d, s: (d, cidx(d, s), 0)),
        out_shape=jax.ShapeDtypeStruct((2, m, vw), F32),
        scratch_shapes=[pltpu.VMEM((heads, dk, dv), F32)],
        compiler_params=_cparams(2),
        name="gdn_scan",
    )(qkv, qkv, qkv, p, p, alog, dtb)


def _mlstm_scan_kernel(q_ref, k_ref, v_ref, i_ref, f_ref, ib_ref, fb_ref, o_ref, c_sc, m_sc, *, heads, dqk, dv):
    d = pl.program_id(0)

    @pl.when(pl.program_id(1) == 0)
    def _():
        c_sc[...] = jnp.zeros_like(c_sc)
        m_sc[...] = jnp.zeros_like(m_sc)

    _, _, incl, _ = _causal_masks(d)
    log_i = i_ref[...] + ib_ref[0]
    log_f = -_softplus(-(f_ref[...] + fb_ref[0]))
    bcum = _dot_exact(incl.astype(F32), log_f)
    btot = jnp.where(d == 0, bcum[CHUNK - 1:CHUNK, :], bcum[0:1, :])
    log_end = btot - bcum + log_i
    m_end = jnp.max(log_end, axis=0, keepdims=True)
    m_st = m_sc[...]
    m_new = jnp.maximum(btot + m_st, m_end)
    carry = jnp.exp(btot + m_st - m_new)
    k_scale = jnp.exp(log_end - m_new)
    b_inter = bcum + m_st
    bcum_t = bcum.T
    log_i_t = log_i.T
    ones_col = (lax.broadcasted_iota(jnp.int32, (CHUNK, LANES), 1) == 0).astype(F32)
    for h in range(heads):
        q = q_ref[:, h * dqk:(h + 1) * dqk] * (dqk ** -0.5)
        k = k_ref[:, h * dqk:(h + 1) * dqk]
        v_ext = jnp.concatenate([v_ref[:, h * dv:(h + 1) * dv], ones_col], axis=1)
        log_d = jnp.where(incl, bcum[:, h:h + 1] - bcum_t[h:h + 1, :] + log_i_t[h:h + 1, :], -jnp.inf)
        m_t = jnp.maximum(b_inter[:, h:h + 1], jnp.max(log_d, axis=-1, keepdims=True))
        w_inter = jnp.exp(b_inter[:, h:h + 1] - m_t)
        s = _dot_nt(q, k) * jnp.exp(log_d - m_t)
        c_ext = c_sc[h]
        out = w_inter * _dot(q, c_ext) + _dot(s, v_ext)
        den = jnp.maximum(jnp.abs(out[:, dv:dv + 1]), jnp.exp(-m_t))
        o_ref[0, :, h * dv:(h + 1) * dv] = out[:, :dv] / den
        c_sc[h] = carry[:, h:h + 1] * c_ext + _dot_tn(k * k_scale[:, h:h + 1], v_ext)
    m_sc[...] = m_new


def _mlstm_scan_call(p, q_cb, gate_cb, ib, fb, heads, dqk, dv, nc_ctx):
    m = p.shape[0]
    nc = m // CHUNK
    qw, vw = heads * dqk, heads * dv
    assert q_cb % qw == 0 and (q_cb + 2 * qw) % vw == 0
    cidx = functools.partial(_scan_chunk, nc_ctx=nc_ctx, nc_tot=nc)
    qb, kb, vb = q_cb // qw, q_cb // qw + 1, (q_cb + 2 * qw) // vw
    gb = gate_cb // LANES
    return pl.pallas_call(
        functools.partial(_mlstm_scan_kernel, heads=heads, dqk=dqk, dv=dv),
        grid=(2, nc),
        in_specs=[pl.BlockSpec((CHUNK, qw), lambda d, s: (cidx(d, s), qb)),
                  pl.BlockSpec((CHUNK, qw), lambda d, s: (cidx(d, s), kb)),
                  pl.BlockSpec((CHUNK, vw), lambda d, s: (cidx(d, s), vb)),
                  pl.BlockSpec((CHUNK, LANES), lambda d, s: (cidx(d, s), gb + 2 * d)),
                  pl.BlockSpec((CHUNK, LANES), lambda d, s: (cidx(d, s), gb + 2 * d + 1)),
                  pl.BlockSpec((1, 1, LANES), lambda d, s: (d, 0, 0)),
                  pl.BlockSpec((1, 1, LANES), lambda d, s: (d, 0, 0))],
        out_specs=pl.BlockSpec((1, CHUNK, vw), lambda d, s: (d, cidx(d, s), 0)),
        out_shape=jax.ShapeDtypeStruct((2, m, vw), F32),
        scratch_shapes=[pltpu.VMEM((heads, dqk, dv + LANES), F32), pltpu.VMEM((1, LANES), F32)],
        compiler_params=_cparams(2),
        name="mlstm_scan",
    )(p, p, p, p, p, ib, fb)


def _filter_kernel(feat_ref, w1_ref, b1_ref, w2_ref, b2_ref, w3_ref, freq_ref, dec_ref, o_ref):
    feats = feat_ref[...]
    h = jnp.sin(freq_ref[0:1, :] * (_dot(feats, w1_ref[...]) + b1_ref[...]))
    h = jnp.sin(freq_ref[1:2, :] * (_dot(h, w2_ref[...]) + b2_ref[...]))
    o_ref[...] = _dot(h, w3_ref[...]) * jnp.exp(-feats[:, 0:1] * jnp.abs(dec_ref[...]))


def _filter_call(feats, w1, b1, w2, b2, w3, freq, dec, tl, tn):
    length, fp = feats.shape
    hid = w2.shape[0]
    n = w3.shape[1]
    return pl.pallas_call(
        _filter_kernel,
        grid=(length // tl, n // tn),
        in_specs=[pl.BlockSpec((tl, fp), lambda i, j: (i, 0)),
                  pl.BlockSpec((fp, hid), lambda i, j: (0, 0)),
                  pl.BlockSpec((1, hid), lambda i, j: (0, 0)),
                  pl.BlockSpec((hid, hid), lambda i, j: (0, 0)),
                  pl.BlockSpec((1, hid), lambda i, j: (0, 0)),
                  pl.BlockSpec((hid, tn), lambda i, j: (0, j)),
                  pl.BlockSpec((2, hid), lambda i, j: (0, 0)),
                  pl.BlockSpec((1, tn), lambda i, j: (0, j))],
        out_specs=pl.BlockSpec((tl, tn), lambda i, j: (i, j)),
        out_shape=jax.ShapeDtypeStruct((length, n), F32),
        compiler_params=_cparams(2),
        name="hyena_filter",
    )(feats, w1, b1, w2, b2, w3, freq, dec)


def _dft_constants(length):
    n2 = DFT_N2
    n = 2 * length
    n1 = n // n2
    kh = n1 // 2

    def cis(num, den):
        ang = (2.0 * math.pi / den) * (num % den).astype(F32)
        return jnp.cos(ang), jnp.sin(ang)

    ar = lambda size: jnp.arange(size, dtype=jnp.int32)
    c, s = cis(ar(n1)[None, :, None] * (n2 * ar(kh)[None, None, :] + ar(n2)[:, None, None]), n)
    m1 = jnp.stack([c, -s], axis=2).reshape(n2, 2 * n1, kh)
    c, s = cis(ar(n2)[:, None] * ar(n2)[None, :], n2)
    g2 = jnp.block([[c, s], [-s, c]])
    c, s = cis(ar(n2)[None, :, None] * (ar(n1)[:, None, None] + n1 * ar(n2)[None, None, :]), n)
    minv = jnp.concatenate([jnp.concatenate([c, -s], axis=2), jnp.concatenate([s, c], axis=2)], axis=1)
    c, s = cis(ar(kh)[:, None] * ar(n1)[None, :], n1)
    pinv = jnp.stack([c, -s], axis=2).reshape(kh, 2 * n1) / n
    return m1.astype(BF16), g2.astype(BF16), minv.astype(BF16), pinv.astype(BF16)


def _dft1_kernel(x_ref, m_ref, o_ref):
    o_ref[...] = _dot(m_ref[0], x_ref[...])


def _dft1_call(x2, m1, width, tc):
    kh = x2.shape[0]
    n2, rows, _ = m1.shape
    per = width // tc
    return pl.pallas_call(
        _dft1_kernel,
        grid=(n2, per),
        in_specs=[pl.BlockSpec((kh, tc), lambda a, j: (0, a * per + j)),
                  pl.BlockSpec((1, rows, kh), lambda a, j: (a, 0, 0))],
        out_specs=pl.BlockSpec((rows, tc), lambda a, j: (0, a * per + j)),
        out_shape=jax.ShapeDtypeStruct((rows, n2 * width), F32),
        compiler_params=_cparams(2),
        name="dft_stage1",
    )(x2, m1)


def _filter_spectrum_kernel(af_ref, ab_ref, g_ref, o_ref):
    sf = _dot(g_ref[...], af_ref[...])
    sb = _dot(g_ref[...], ab_ref[...])
    sign = jnp.where(lax.broadcasted_iota(jnp.int32, (2 * DFT_N2, 1), 0) < DFT_N2, 1.0, -1.0)
    o_ref[...] = sf + sign * sb


def _filter_spectrum_call(a, g2, tc):
    rows, width = a.shape
    half = width // 2
    blk = 2 * DFT_N2
    return pl.pallas_call(
        _filter_spectrum_kernel,
        grid=(rows // blk, half // tc),
        in_specs=[pl.BlockSpec((blk, tc), lambda a_, j: (a_, j)),
                  pl.BlockSpec((blk, tc), lambda a_, j: (a_, half // tc + j)),
                  pl.BlockSpec((blk, blk), lambda a_, j: (0, 0))],
        out_specs=pl.BlockSpec((blk, tc), lambda a_, j: (a_, j)),
        out_shape=jax.ShapeDtypeStruct((rows, half), F32),
        compiler_params=_cparams(2),
        name="hyena_filter_spectrum",
    )(a, a, g2)


def _complex_mul(x, kf, n):
    xr, xi = x[:n], x[n:]
    kr, ki = kf[:n], kf[n:]
    return jnp.concatenate([xr * kr - xi * ki, xr * ki + xi * kr], axis=0)


def _spectral_mid_kernel(a_ref, g_ref, kf_ref, minv_ref, o_ref):
    x = _dot(g_ref[...], a_ref[...])
    o_ref[...] = _dot(minv_ref[0], _complex_mul(x, kf_ref[...], DFT_N2))


def _spectral_mid_call(a, g2, kf, kf_cb, minv, tc):
    rows, width = a.shape
    blk = 2 * DFT_N2
    return pl.pallas_call(
        _spectral_mid_kernel,
        grid=(rows // blk, width // tc),
        in_specs=[pl.BlockSpec((blk, tc), lambda a_, j: (a_, j)),
                  pl.BlockSpec((blk, blk), lambda a_, j: (0, 0)),
                  pl.BlockSpec((blk, tc), lambda a_, j: (a_, kf_cb + j)),
                  pl.BlockSpec((1, blk, blk), lambda a_, j: (a_, 0, 0))],
        out_specs=pl.BlockSpec((blk, tc), lambda a_, j: (a_, j)),
        out_shape=jax.ShapeDtypeStruct((rows, width), F32),
        compiler_params=_cparams(2),
        name="hyena_spectral_mid",
    )(a, g2, kf, minv)


def _idft2_kernel(z_ref, p_ref, v_ref, gate_ref, bias_ref, o_ref):
    y = _dot(p_ref[...], z_ref[...])
    o_ref[...] = gate_ref[...] * (y + v_ref[...] * bias_ref[...])


def _idft2_call(z2, pinv, v2, gate2, bias, width):
    rows = z2.shape[0]
    kh = pinv.shape[0]
    n2 = z2.shape[1] // width
    return pl.pallas_call(
        _idft2_kernel,
        grid=(n2,),
        in_specs=[pl.BlockSpec((rows, width), lambda a: (0, a)),
                  pl.BlockSpec((kh, rows), lambda a: (0, 0)),
                  pl.BlockSpec((kh, width), lambda a: (0, a)),
                  pl.BlockSpec((kh, width), lambda a: (0, a)),
                  pl.BlockSpec((1, width), lambda a: (0, 0))],
        out_specs=pl.BlockSpec((kh, width), lambda a: (0, a)),
        out_shape=jax.ShapeDtypeStruct((kh, n2 * width), F32),
        compiler_params=_cparams(1),
        name="idft_stage2_gate",
    )(z2, pinv, v2, gate2, bias)


def _long_conv(v, gate, bias, kf, kf_cb, consts):
    m1, g2, minv, pinv = consts
    length, width = v.shape
    kh = length // DFT_N2
    tc = min(width, 1024)
    a = _dft1_call(v.reshape(kh, DFT_N2 * width), m1, width, tc)
    a = a.reshape(a.shape[0] * DFT_N2, width)
    z = _spectral_mid_call(a, g2, kf, kf_cb, minv, tc)
    z = z.reshape(z.shape[0] // DFT_N2, DFT_N2 * width)
    y = _idft2_call(z, pinv, v.reshape(kh, DFT_N2 * width), gate.reshape(kh, DFT_N2 * width), bias, width)
    return y.reshape(length, width)


def _dense_dft_constants(length):
    n = 2 * length
    k = jnp.arange(n, dtype=jnp.int32)[:, None]
    t = jnp.arange(length, dtype=jnp.int32)[None, :]
    ang = (2.0 * math.pi / n) * ((k * t) % n).astype(F32)
    c, s = jnp.cos(ang), jnp.sin(ang)
    fwd = jnp.concatenate([c, -s], axis=0)
    inv = jnp.concatenate([c.T, -s.T], axis=1) / n
    return fwd.astype(BF16), inv.astype(BF16)


def _dense_spectrum_kernel(tf_ref, tb_ref, f_ref, o_ref, *, n):
    sign = jnp.where(lax.broadcasted_iota(jnp.int32, (2 * n, 1), 0) < n, 1.0, -1.0)
    o_ref[...] = _dot(f_ref[...], tf_ref[...]) + sign * _dot(f_ref[...], tb_ref[...])


def _dense_spectrum_call(taps, fwd, tc):
    length, width = taps.shape
    half = width // 2
    rows = fwd.shape[0]
    return pl.pallas_call(
        functools.partial(_dense_spectrum_kernel, n=rows // 2),
        grid=(half // tc,),
        in_specs=[pl.BlockSpec((length, tc), lambda j: (0, j)),
                  pl.BlockSpec((length, tc), lambda j: (0, half // tc + j)),
                  pl.BlockSpec((rows, length), lambda j: (0, 0))],
        out_specs=pl.BlockSpec((rows, tc), lambda j: (0, j)),
        out_shape=jax.ShapeDtypeStruct((rows, half), F32),
        compiler_params=_cparams(1),
        name="hyena_filter_spectrum_dense",
    )(taps, taps, fwd)


def _dense_hyena_kernel(v_ref, x1_ref, x2_ref, kf0_ref, kf1_ref, f_ref, inv_ref, b0_ref, b1_ref, o_ref, *, n):
    def conv(u, kf_ref, b_ref):
        y = _dot(inv_ref[...], _complex_mul(_dot(f_ref[...], u), kf_ref[...], n))
        return y + u * b_ref[...]

    z = x1_ref[...] * conv(v_ref[...], kf0_ref, b0_ref)
    o_ref[...] = x2_ref[...] * conv(z, kf1_ref, b1_ref)


def _dense_hyena_call(hc, length, width, kf, fwd, inv, bias0, bias1, tc):
    rows = fwd.shape[0]
    per = width // tc
    return pl.pallas_call(
        functools.partial(_dense_hyena_kernel, n=rows // 2),
        grid=(per,),
        in_specs=[pl.BlockSpec((length, tc), lambda j: (0, j)),
                  pl.BlockSpec((length, tc), lambda j: (0, per + j)),
                  pl.BlockSpec((length, tc), lambda j: (0, 2 * per + j)),
                  pl.BlockSpec((rows, tc), lambda j: (0, j)),
                  pl.BlockSpec((rows, tc), lambda j: (0, per + j)),
                  pl.BlockSpec((rows, length), lambda j: (0, 0)),
                  pl.BlockSpec((length, rows), lambda j: (0, 0)),
                  pl.BlockSpec((1, tc), lambda j: (0, j)),
                  pl.BlockSpec((1, tc), lambda j: (0, j))],
        out_specs=pl.BlockSpec((length, tc), lambda j: (0, j)),
        out_shape=jax.ShapeDtypeStruct((length, width), F32),
        compiler_params=_cparams(1),
        name="hyena_dense",
    )(hc, hc, hc, kf, kf, fwd, inv, bias0, bias1)


def _filter_features(length):
    t = jnp.linspace(0.0, 1.0, length, dtype=F32)[:, None]
    w = (2.0 * math.pi / length) * jnp.arange(length, dtype=F32)[:, None]
    bands = jnp.linspace(1e-4, HYENA_BANDS - 1, HYENA_BANDS, dtype=F32)[None, :]
    feats = jnp.concatenate([t, jnp.cos(bands * w), -jnp.sin(bands * w)], axis=-1)
    return jnp.pad(feats, ((0, 0), (0, LANES - feats.shape[1])))


def _head_rms(x, heads, width):
    outs = []
    for h in range(heads):
        xh = x[:, h * width:(h + 1) * width]
        outs.append(xh * lax.rsqrt(jnp.mean(xh * xh, axis=-1, keepdims=True) + RMS_EPS))
    return jnp.concatenate(outs, axis=1)


def _merge_kernel(og_ref, om_ref, z_ref, o_ref, hy_ref, g0_ref, g1_ref, g2_ref, gn_ref, mn_ref,
                  wa_ref, wb_ref, wc_ref, out_ref, *, gdn_heads, gdn_dv, ml_heads, ml_dv):
    a = _head_rms(og_ref[0] + og_ref[1], gdn_heads, gdn_dv) * gn_ref[...] * _silu(z_ref[...])
    b = _head_rms(om_ref[0] + om_ref[1], ml_heads, ml_dv) * mn_ref[...] * _sigmoid(o_ref[...])
    out_ref[...] = (_sigmoid(g0_ref[...]) * _dot(a, wa_ref[...]) + _sigmoid(g1_ref[...]) * _dot(b, wb_ref[...])
                    + _sigmoid(g2_ref[...]) * _dot(hy_ref[...], wc_ref[...]))


def _merge_call(og, om, p, z_cb, o_cb, hy, gate_cb, gn, mn, wa, wb, wc, layer, heads, tm):
    m, d = p.shape[0], wa.shape[-1]
    gw, mw, hw = og.shape[-1], om.shape[-1], hy.shape[-1]
    gdn_heads, gdn_dv, ml_heads, ml_dv = heads
    zb, ob, gb = z_cb // gw, o_cb // mw, gate_cb // d
    return pl.pallas_call(
        functools.partial(_merge_kernel, gdn_heads=gdn_heads, gdn_dv=gdn_dv, ml_heads=ml_heads, ml_dv=ml_dv),
        grid=(m // tm,),
        in_specs=[pl.BlockSpec((2, tm, gw), lambda i: (0, i, 0)),
                  pl.BlockSpec((2, tm, mw), lambda i: (0, i, 0)),
                  pl.BlockSpec((tm, gw), lambda i: (i, zb)),
                  pl.BlockSpec((tm, mw), lambda i: (i, ob)),
                  pl.BlockSpec((tm, hw), lambda i: (i, 0)),
                  pl.BlockSpec((tm, d), lambda i: (i, gb)),
                  pl.BlockSpec((tm, d), lambda i: (i, gb + 1)),
                  pl.BlockSpec((tm, d), lambda i: (i, gb + 2)),
                  pl.BlockSpec((1, gw), lambda i: (0, 0)),
                  pl.BlockSpec((1, mw), lambda i: (0, 0)),
                  pl.BlockSpec((None, gw, d), lambda i: (layer, 0, 0)),
                  pl.BlockSpec((None, mw, d), lambda i: (layer, 0, 0)),
                  pl.BlockSpec((None, hw, d), lambda i: (layer, 0, 0))],
        out_specs=pl.BlockSpec((tm, d), lambda i: (i, 0)),
        out_shape=jax.ShapeDtypeStruct((m, d), F32),
        compiler_params=_cparams(1),
        name="branch_merge",
    )(og, om, p, p, hy, p, p, p, gn, mn, wa, wb, wc)


def _outproj_kernel(y_ref, x_ref, m_ref, w_ref, o_ref, *, tm, ctx_len):
    gate = jnp.where(_row_is_ctx(pl.program_id(0) * tm, tm, ctx_len), m_ref[0:1, :], m_ref[1:2, :])
    o_ref[...] = x_ref[...] + gate * _dot(y_ref[...], w_ref[...])


def _outproj_call(y, x, mods, w, layer, ctx_len, tm):
    m, d = x.shape
    return pl.pallas_call(
        functools.partial(_outproj_kernel, tm=tm, ctx_len=ctx_len),
        grid=(m // tm,),
        in_specs=[pl.BlockSpec((tm, d), lambda i: (i, 0)),
                  pl.BlockSpec((tm, d), lambda i: (i, 0)),
                  pl.BlockSpec((SUBLANES, d), lambda i: (0, 0)),
                  pl.BlockSpec((None, d, d), lambda i: (layer, 0, 0))],
        out_specs=pl.BlockSpec((tm, d), lambda i: (i, 0)),
        out_shape=jax.ShapeDtypeStruct((m, d), F32),
        compiler_params=_cparams(1),
        name="mixer_out_proj",
    )(y, x, mods, w)


def _final_norm_kernel(x_ref, g_ref, o_ref):
    x = x_ref[...]
    o_ref[...] = x * lax.rsqrt(jnp.mean(x * x, axis=-1, keepdims=True) + RMS_EPS) * g_ref[...]


def _final_norm_call(x, g, row0, rows, tm):
    d = x.shape[1]
    return pl.pallas_call(
        _final_norm_kernel,
        grid=(rows // tm,),
        in_specs=[pl.BlockSpec((tm, d), lambda i: (row0 // tm + i, 0)),
                  pl.BlockSpec((1, d), lambda i: (0, 0))],
        out_specs=pl.BlockSpec((tm, d), lambda i: (i, 0)),
        out_shape=jax.ShapeDtypeStruct((rows, d), F32),
        compiler_params=_cparams(1),
        name="final_rms_norm",
    )(x, g)


def _pack_rows(rows, d):
    out = jnp.concatenate([r.reshape(1, d) for r in rows], axis=0)
    return jnp.pad(out, ((0, SUBLANES - out.shape[0]), (0, 0)))


def _lane_row(vals, lanes=LANES):
    return jnp.pad(vals.astype(F32), ((0, 0), (0, lanes - vals.shape[1])))[:, None, :]


def _pick_tile(total, candidates):
    for c in candidates:
        if total % c == 0:
            return c
    raise ValueError(f"no tile for {total} among {candidates}")


def kernel(x, c, ctx, c_ctx, w_ada, b_ada, norm_g, ffn_w_gate, ffn_w_up, ffn_w_down, w_in, gdn_conv, gdn_a_log,
           gdn_dt_bias, gdn_norm, mlstm_i_bias, mlstm_f_bias, mlstm_norm, hy_short_w, hy_short_b, hf_w1, hf_b1,
           hf_w2, hf_b2, hf_w3, hf_freq, hf_decay, hy_bias, w_a_out, w_b_out, w_c_out, w_out, final_norm):
    batch, seq, d = x.shape
    assert batch == 1 and c.shape[0] == 1 and ctx.shape[0] == 1
    ctx_len = ctx.shape[1]
    depth = w_ada.shape[0]
    m = ctx_len + seq
    assert ctx_len % CHUNK == 0 and seq % CHUNK == 0

    gdn_heads = gdn_a_log.shape[-1]
    gdn_vw = w_a_out.shape[1]
    gdn_qkw = (gdn_conv.shape[1] - gdn_vw) // 2
    gdn_dk, gdn_dv = gdn_qkw // gdn_heads, gdn_vw // gdn_heads
    ml_heads = mlstm_i_bias.shape[-1]
    ml_vw = w_b_out.shape[1]
    hy_w = w_c_out.shape[1]
    n_in = w_in.shape[-1]
    ml_qkw = (n_in - (2 * gdn_qkw + gdn_vw) - 4 * gdn_heads - 4 * ml_heads - ml_vw - gdn_vw - ml_vw
              - 3 * hy_w - 3 * d) // 2
    ml_dqk, ml_dv = ml_qkw // ml_heads, ml_vw // ml_heads
    assert gdn_dk == LANES and gdn_dv == LANES and ml_dqk == LANES and ml_dv % LANES == 0

    o_gqkv = 0
    o_gbeta = o_gqkv + 2 * gdn_qkw + gdn_vw
    o_ga = o_gbeta + 2 * gdn_heads
    o_mqkv = o_ga + 2 * gdn_heads
    o_mi = o_mqkv + 2 * ml_qkw + ml_vw
    o_mf = o_mi + 2 * ml_heads
    o_z = o_mf + 2 * ml_heads
    o_o = o_z + gdn_vw
    o_hy = o_o + ml_vw
    o_gates = o_hy + 3 * hy_w
    assert o_gates + 3 * d == n_in

    def small(off, heads):
        return [jnp.pad(w_in[:, :, off + dd * heads:off + (dd + 1) * heads], ((0, 0), (0, 0), (0, LANES - heads)))
                for dd in range(2)]

    gb, ga = small(o_gbeta, gdn_heads), small(o_ga, gdn_heads)
    mi, mf = small(o_mi, ml_heads), small(o_mf, ml_heads)
    groups = [w_in[:, :, o_gqkv:o_gbeta], w_in[:, :, o_mqkv:o_mi], w_in[:, :, o_z:o_o], w_in[:, :, o_o:o_hy],
              w_in[:, :, o_hy:o_gates], w_in[:, :, o_gates:],
              gb[0], ga[0], gb[1], ga[1], mi[0], mf[0], mi[1], mf[1]]
    w_in_p = jnp.concatenate(groups, axis=-1).astype(BF16)
    c_gqkv = 0
    c_mqkv = c_gqkv + 2 * gdn_qkw + gdn_vw
    c_z = c_mqkv + 2 * ml_qkw + ml_vw
    c_o = c_z + gdn_vw
    c_hy = c_o + ml_vw
    c_gates = c_hy + 3 * hy_w
    c_gsmall = c_gates + 3 * d
    c_msmall = c_gsmall + 4 * LANES
    n_p = c_msmall + 4 * LANES
    assert w_in_p.shape[-1] == n_p

    f_hidden = ffn_w_gate.shape[-1]
    tf = 256
    f_pad = -(-f_hidden // tf) * tf
    wg = jnp.pad(ffn_w_gate, ((0, 0), (0, 0), (0, 0), (0, f_pad - f_hidden))).astype(BF16)
    wu = jnp.pad(ffn_w_up, ((0, 0), (0, 0), (0, 0), (0, f_pad - f_hidden))).astype(BF16)
    wd = jnp.pad(ffn_w_down, ((0, 0), (0, 0), (0, f_pad - f_hidden), (0, 0))).astype(BF16)
    wa, wb, wc, wo = (t.astype(BF16) for t in (w_a_out, w_b_out, w_c_out, w_out))

    tm_big = _pick_tile(m, (768, 512, 384, 256, 128, 64))
    tm_mid = _pick_tile(m, (256, 128, 64))
    tm_small = _pick_tile(m, (128, 64))
    tn_in = _pick_tile(n_p, (512, 256, 128))
    nc_ctx = ctx_len // CHUNK

    cond = jnp.pad(jnp.concatenate([c_ctx[None, :], c], axis=0), ((0, SUBLANES - 2), (0, 0)))
    mods = _modulation_call(cond, w_ada, b_ada)

    dft_consts = _dft_constants(seq)
    dense_fwd, dense_inv = _dense_dft_constants(ctx_len)
    feats_lat = _filter_features(seq)
    feats_ctx = _filter_features(ctx_len)
    tl = _pick_tile(seq, (512, 256, 128, 64))

    s = jnp.concatenate([ctx[0], x[0]], axis=0)
    for l in range(depth):
        last = l == depth - 1

        def mod(idx):
            return [mods[l, 0, idx * d:(idx + 1) * d], mods[l, 1, idx * d:(idx + 1) * d]]

        s = _ffn_call(s, _pack_rows(mod(0) + mod(1) + mod(2), d), norm_g[l, 0][None, :], wg, wu, wd, l, 0,
                      ctx_len, tm_big, tf)
        p = _inproj_call(s, _pack_rows(mod(3) + mod(4), d), norm_g[l, 1][None, :], w_in_p[l], ctx_len, tm_big, tn_in)

        gqkv = _dwconv_call(p, c_gqkv, 2 * gdn_qkw + gdn_vw, jnp.transpose(gdn_conv[l]),
                            jnp.zeros((1, 2 * gdn_qkw + gdn_vw), F32), ctx_len, tm_mid, 512, True)
        hc = _dwconv_call(p, c_hy, 3 * hy_w, jnp.transpose(hy_short_w[l]), hy_short_b[l][None, :],
                          ctx_len, tm_mid, 512, False)

        og = _gdn_scan_call(gqkv, p, c_gsmall // LANES, _lane_row(gdn_a_log[l]), _lane_row(gdn_dt_bias[l]),
                            gdn_heads, gdn_dk, gdn_dv, nc_ctx)
        om = _mlstm_scan_call(p, c_mqkv, c_msmall, _lane_row(mlstm_i_bias[l]), _lane_row(mlstm_f_bias[l]),
                              ml_heads, ml_dqk, ml_dv, nc_ctx)

        fargs = (jnp.pad(hf_w1[l], ((0, LANES - hf_w1.shape[1]), (0, 0))), hf_b1[l][None, :], hf_w2[l],
                 hf_b2[l][None, :], hf_w3[l], hf_freq[l], jnp.tile(hf_decay[l], 2)[None, :])
        taps = _filter_call(feats_lat, *fargs, tl, 1024)
        m1, g2, minv, pinv = dft_consts
        kh = seq // DFT_N2
        ta = _dft1_call(taps.reshape(kh, DFT_N2 * taps.shape[1]), m1, taps.shape[1], 1024)
        kf = _filter_spectrum_call(ta.reshape(ta.shape[0] * DFT_N2, taps.shape[1]), g2, 1024)
        hl = hc[ctx_len:]
        tc_h = min(hy_w, 1024)
        z1 = _long_conv(hl[:, :hy_w], hl[:, hy_w:2 * hy_w], hy_bias[l, 0][None, :], kf, 0, dft_consts)
        z2 = _long_conv(z1, hl[:, 2 * hy_w:], hy_bias[l, 1][None, :], kf, hy_w // tc_h, dft_consts)
        if not last:
            taps_c = _filter_call(feats_ctx, *fargs, _pick_tile(ctx_len, (256, 128, 64)), 1024)
            kf_c = _dense_spectrum_call(taps_c, dense_fwd, 512)
            zc = _dense_hyena_call(hc[:ctx_len], ctx_len, hy_w, kf_c, dense_fwd, dense_inv, hy_bias[l, 0][None, :],
                                   hy_bias[l, 1][None, :], 256)
        else:
            zc = jnp.zeros((ctx_len, hy_w), F32)
        hy = jnp.concatenate([zc, z2], axis=0)

        merged = _merge_call(og, om, p, c_z, c_o, hy, c_gates, jnp.tile(gdn_norm[l], gdn_heads)[None, :],
                             mlstm_norm[l][None, :], wa, wb, wc, l, (gdn_heads, gdn_dv, ml_heads, ml_dv), tm_small)
        s = _outproj_call(merged, s, _pack_rows(mod(5), d), wo, l, ctx_len, tm_mid)
        s = _ffn_call(s, _pack_rows(mod(6) + mod(7) + mod(8), d), norm_g[l, 2][None, :], wg, wu, wd, l, 1,
                      ctx_len, tm_big, tf)

    tm_fin = _pick_tile(math.gcd(ctx_len, seq), (256, 128, 64))
    out = _final_norm_call(s, final_norm[None, :], ctx_len, seq, tm_fin)
    return out[None]
```

```python
import functools
import math

import jax
import jax.numpy as jnp
import numpy as np
from jax import lax
from jax.experimental import pallas as pl
from jax.experimental.pallas import tpu as pltpu

F32 = jnp.float32
BF16 = jnp.bfloat16

N_MOD = 9
RMS_EPS = 1e-6
L2_EPS = 1e-6
CHUNK = 64
GDN_CONV = 5
HYENA_SHORT = 3
HYENA_BANDS = 16
LANES = 128
SUBLANES = 8
DFT_N2 = 128
VMEM_LIMIT = 56 * 1024 * 1024


def _cparams(n_axes):
    return pltpu.CompilerParams(dimension_semantics=("arbitrary",) * n_axes,
                                vmem_limit_bytes=VMEM_LIMIT)


def _sigmoid(x):
    return jax.nn.sigmoid(x)


def _silu(x):
    return x * jax.nn.sigmoid(x)


def _softplus(x):
    return jnp.maximum(x, 0.0) + jnp.log(1.0 + jnp.exp(-jnp.abs(x)))


def _dot(a, b):
    return jnp.dot(a.astype(BF16), b.astype(BF16), preferred_element_type=F32)


def _dot_nt(a, b):
    return lax.dot_general(a.astype(BF16), b.astype(BF16), (((1,), (1,)), ((), ())),
                           preferred_element_type=F32)


def _dot_tn(a, b):
    return lax.dot_general(a.astype(BF16), b.astype(BF16), (((0,), (0,)), ((), ())),
                           preferred_element_type=F32)


def _dot_exact(a, b):
    return jnp.dot(a, b, preferred_element_type=F32, precision=lax.Precision.HIGHEST)


def _row_is_ctx(row0, rows, ctx_len):
    r = row0 + lax.broadcasted_iota(jnp.int32, (rows, 1), 0)
    return r < ctx_len


def _modnorm(x, g, shift, scale):
    y = x * lax.rsqrt(jnp.mean(x * x, axis=-1, keepdims=True) + RMS_EPS) * g
    return y * (1.0 + scale) + shift


ROW_CHUNK = 128


def _for_row_chunks(tm, fn):
    rc = math.gcd(tm, ROW_CHUNK)

    def body(c, carry):
        fn(pl.multiple_of(c * rc, rc), rc)
        return carry

    lax.fori_loop(0, tm // rc, body, 0)


def _store_modulated(x_ref, m_ref, g_ref, h_sc, row0, tm, ctx_len):
    def chunk(r0, rc):
        is_ctx = _row_is_ctx(row0 + r0, rc, ctx_len)
        shift = jnp.where(is_ctx, m_ref[0:1, :], m_ref[1:2, :])
        scale = jnp.where(is_ctx, m_ref[2:3, :], m_ref[3:4, :])
        h_sc[pl.ds(r0, rc), :] = _modnorm(x_ref[pl.ds(r0, rc), :], g_ref[...], shift, scale).astype(h_sc.dtype)

    _for_row_chunks(tm, chunk)


def _mod_kernel(c_ref, w_ref, b_ref, o_ref):
    o_ref[0] = _dot(_silu(c_ref[...]), w_ref[0]) + b_ref[0]


def _modulation_call(cond, w_ada, b_ada):
    depth, d, n = w_ada.shape
    tn = _pick_tile(n, (1024, 512, 256, 128))
    return pl.pallas_call(
        _mod_kernel,
        grid=(depth, n // tn),
        in_specs=[pl.BlockSpec((SUBLANES, d), lambda l, j: (0, 0)),
                  pl.BlockSpec((1, d, tn), lambda l, j: (l, 0, j)),
                  pl.BlockSpec((1, 1, tn), lambda l, j: (l, 0, j))],
        out_specs=pl.BlockSpec((1, SUBLANES, tn), lambda l, j: (l, 0, j)),
        out_shape=jax.ShapeDtypeStruct((depth, SUBLANES, n), F32),
        compiler_params=_cparams(2),
        name="adaln_modulation",
    )(cond, w_ada, b_ada.reshape(depth, 1, n))


def _ffn_kernel(x_ref, m_ref, g_ref, wg_ref, wu_ref, wd_ref, o_ref, h_sc, acc_sc, *, tm, ctx_len):
    row0 = pl.program_id(0) * tm
    f = pl.program_id(1)

    @pl.when(f == 0)
    def _():
        _store_modulated(x_ref, m_ref, g_ref, h_sc, row0, tm, ctx_len)
        acc_sc[...] = jnp.zeros_like(acc_sc)

    h = h_sc[...]
    a = _silu(_dot(h, wg_ref[...])) * _dot(h, wu_ref[...])
    acc_sc[...] += _dot(a, wd_ref[...])

    @pl.when(f == pl.num_programs(1) - 1)
    def _():
        def chunk(r0, rc):
            gate = jnp.where(_row_is_ctx(row0 + r0, rc, ctx_len), m_ref[4:5, :], m_ref[5:6, :])
            rows = pl.ds(r0, rc)
            o_ref[rows, :] = x_ref[rows, :] + 0.5 * gate * acc_sc[rows, :]

        _for_row_chunks(tm, chunk)


def _ffn_call(x, mods, g, wg, wu, wd, layer, which, ctx_len, tm, tf):
    m, d = x.shape
    fp = wg.shape[-1]
    return pl.pallas_call(
        functools.partial(_ffn_kernel, tm=tm, ctx_len=ctx_len),
        grid=(m // tm, fp // tf),
        in_specs=[pl.BlockSpec((tm, d), lambda i, f: (i, 0)),
                  pl.BlockSpec((SUBLANES, d), lambda i, f: (0, 0)),
                  pl.BlockSpec((1, d), lambda i, f: (0, 0)),
                  pl.BlockSpec((None, None, d, tf), lambda i, f: (layer, which, 0, f)),
                  pl.BlockSpec((None, None, d, tf), lambda i, f: (layer, which, 0, f)),
                  pl.BlockSpec((None, None, tf, d), lambda i, f: (layer, which, f, 0))],
        out_specs=pl.BlockSpec((tm, d), lambda i, f: (i, 0)),
        out_shape=jax.ShapeDtypeStruct((m, d), F32),
        scratch_shapes=[pltpu.VMEM((tm, d), BF16), pltpu.VMEM((tm, d), F32)],
        compiler_params=_cparams(2),
        name="macaron_swiglu",
    )(x, mods, g, wg, wu, wd)


def _inproj_kernel(x_ref, m_ref, g_ref, w_ref, o_ref, h_sc, *, tm, ctx_len):
    @pl.when(pl.program_id(1) == 0)
    def _():
        _store_modulated(x_ref, m_ref, g_ref, h_sc, pl.program_id(0) * tm, tm, ctx_len)

    o_ref[...] = _dot(h_sc[...], w_ref[...])


def _inproj_call(x, mods, g, w, ctx_len, tm, tn):
    m, d = x.shape
    n = w.shape[1]
    return pl.pallas_call(
        functools.partial(_inproj_kernel, tm=tm, ctx_len=ctx_len),
        grid=(m // tm, n // tn),
        in_specs=[pl.BlockSpec((tm, d), lambda i, j: (i, 0)),
                  pl.BlockSpec((SUBLANES, d), lambda i, j: (0, 0)),
                  pl.BlockSpec((1, d), lambda i, j: (0, 0)),
                  pl.BlockSpec((d, tn), lambda i, j: (0, j))],
        out_specs=pl.BlockSpec((tm, tn), lambda i, j: (i, j)),
        out_shape=jax.ShapeDtypeStruct((m, n), F32),
        scratch_shapes=[pltpu.VMEM((tm, d), BF16)],
        compiler_params=_cparams(2),
        name="mixer_in_proj",
    )(x, mods, g, w)


def _dwconv_kernel(prev_ref, cur_ref, next_ref, w_ref, b_ref, o_ref, ext_sc, *, tm, taps, ctx_len, m_rows, act):
    i = pl.program_id(0)
    ext_sc[0:SUBLANES, :] = prev_ref[...]
    ext_sc[SUBLANES:SUBLANES + tm, :] = cur_ref[...]
    ext_sc[SUBLANES + tm:2 * SUBLANES + tm, :] = next_ref[...]
    r = i * tm + lax.broadcasted_iota(jnp.int32, (tm, 1), 0)
    seg_lo = jnp.where(r < ctx_len, 0, ctx_len)
    seg_hi = jnp.where(r < ctx_len, ctx_len, m_rows)
    acc = jnp.zeros(o_ref.shape, F32) + b_ref[...]
    for t in range(taps):
        off = t - taps // 2
        xs = ext_sc[pl.ds(SUBLANES + off, tm), :]
        valid = jnp.logical_and(r + off >= seg_lo, r + off < seg_hi)
        acc = acc + jnp.where(valid, xs, 0.0) * w_ref[t:t + 1, :]
    o_ref[...] = _silu(acc) if act else acc


def _dwconv_call(p, col0, ncols, w_t, b, ctx_len, tm, tc, act):
    m = p.shape[0]
    taps = w_t.shape[0]
    cb0 = col0 // tc
    rb = tm // SUBLANES
    last = m // SUBLANES - 1
    return pl.pallas_call(
        functools.partial(_dwconv_kernel, tm=tm, taps=taps, ctx_len=ctx_len, m_rows=m, act=act),
        grid=(m // tm, ncols // tc),
        in_specs=[pl.BlockSpec((SUBLANES, tc), lambda i, j: (jnp.maximum(i * rb - 1, 0), cb0 + j)),
                  pl.BlockSpec((tm, tc), lambda i, j: (i, cb0 + j)),
                  pl.BlockSpec((SUBLANES, tc), lambda i, j: (jnp.minimum((i + 1) * rb, last), cb0 + j)),
                  pl.BlockSpec((taps, tc), lambda i, j: (0, j)),
                  pl.BlockSpec((1, tc), lambda i, j: (0, j))],
        out_specs=pl.BlockSpec((tm, tc), lambda i, j: (i, j)),
        out_shape=jax.ShapeDtypeStruct((m, ncols), F32),
        scratch_shapes=[pltpu.VMEM((tm + 2 * SUBLANES, tc), F32)],
        compiler_params=_cparams(2),
        name="depthwise_conv",
    )(p, p, p, w_t, b)


def _bwd_chunk(s, nc_ctx, nc_tot):
    return jnp.where(s < nc_ctx, nc_ctx - 1 - s, nc_tot - 1 - (s - nc_ctx))


def _causal_masks(direction):
    ii = lax.broadcasted_iota(jnp.int32, (CHUNK, CHUNK), 0)
    jj = lax.broadcasted_iota(jnp.int32, (CHUNK, CHUNK), 1)
    rel = ii - jj if direction == 0 else jj - ii
    return rel >= 0, rel > 0


def _unit_triangular_inverses(mats):
    ii = lax.broadcasted_iota(jnp.int32, (CHUNK, CHUNK), 0)
    jj = lax.broadcasted_iota(jnp.int32, (CHUNK, CHUNK), 1)
    eye = (ii == jj).astype(F32)

    def same_block(width):
        return (ii // width) == (jj // width)

    inner = same_block(8)
    d1 = [jnp.where(inner, a, 0.0) for a in mats]
    d2 = [_dot(x, x) for x in d1]
    d4 = [_dot(x, x) for x in d2]
    p = [_dot(eye - x, eye + y) for x, y in zip(d1, d2)]
    p = [_dot(x, eye + y) for x, y in zip(p, d4)]
    for width in (16, 32, 64):
        outer = same_block(width)
        ring = jnp.logical_and(outer, jnp.logical_not(inner))
        t = [_dot(x, jnp.where(ring, a, 0.0)) for x, a in zip(p, mats)]
        p = [x - _dot(y, x) for x, y in zip(p, t)]
        inner = outer
    return p


def _gdn_scan_kernel(qf_ref, kf_ref, vf_ref, btf_ref, af_ref, qb_ref, kb_ref, vb_ref, btb_ref, ab_ref,
                     alog_ref, dtb_ref, of_ref, ob_ref, s_sc, *, heads, dk, dv):
    @pl.when(pl.program_id(0) == 0)
    def _():
        s_sc[...] = jnp.zeros_like(s_sc)

    chains = []
    for direction, (q_ref, k_ref, v_ref, bt_ref, a_ref, o_ref) in enumerate(
            ((qf_ref, kf_ref, vf_ref, btf_ref, af_ref, of_ref), (qb_ref, kb_ref, vb_ref, btb_ref, ab_ref, ob_ref))):
        incl, strict = _causal_masks(direction)
        beta = _sigmoid(bt_ref[...])
        glog = -jnp.exp(alog_ref[direction]) * _softplus(a_ref[...] + dtb_ref[direction])
        gcum = _dot_exact(incl.astype(F32), glog)
        gtot = gcum[CHUNK - 1:CHUNK, :] if direction == 0 else gcum[0:1, :]
        e_in = jnp.exp(gcum)
        e_out = jnp.exp(gtot - gcum)
        g_end = jnp.exp(gtot)
        gcum_t = gcum.T
        for h in range(heads):
            q = q_ref[:, h * dk:(h + 1) * dk]
            k = k_ref[:, h * dk:(h + 1) * dk]
            q = q * lax.rsqrt(jnp.sum(q * q, axis=-1, keepdims=True) + L2_EPS) * (dk ** -0.5)
            k = k * lax.rsqrt(jnp.sum(k * k, axis=-1, keepdims=True) + L2_EPS)
            b_col = beta[:, h:h + 1]
            kb = k * b_col
            chains.append(dict(
                q=q, k=k, kb=kb, strict=strict, o_ref=o_ref, h=h, slot=direction * heads + h,
                decay=jnp.exp(jnp.where(incl, gcum[:, h:h + 1] - gcum_t[h:h + 1, :], -jnp.inf)),
                rhs=jnp.concatenate([v_ref[:, h * dv:(h + 1) * dv] * b_col, kb * e_in[:, h:h + 1]], axis=1),
                q_in=q * e_in[:, h:h + 1], k_out=k * e_out[:, h:h + 1], g_end=g_end[:, h:h + 1]))

    a_mats = [jnp.where(c["strict"], _dot_nt(c["kb"], c["k"]) * c["decay"], 0.0) for c in chains]
    qk = [_dot_nt(c["q"], c["k"]) * c["decay"] for c in chains]
    t_inv = _unit_triangular_inverses(a_mats)
    sol = [_dot(t, c["rhs"]) for t, c in zip(t_inv, chains)]
    states = [s_sc[c["slot"]] for c in chains]
    v_new = [x[:, :dv] - _dot(x[:, dv:], st) for x, st in zip(sol, states)]
    o_inter = [_dot(c["q_in"], st) for c, st in zip(chains, states)]
    o_intra = [_dot(x, y) for x, y in zip(qk, v_new)]
    s_upd = [_dot_tn(c["k_out"], y) for c, y in zip(chains, v_new)]
    for c, st, x, y, z in zip(chains, states, o_inter, o_intra, s_upd):
        c["o_ref"][:, c["h"] * dv:(c["h"] + 1) * dv] = x + y
        s_sc[c["slot"]] = st * c["g_end"] + z


def _gdn_scan_call(qkv, p, beta_cb, alog, dtb, heads, dk, dv, nc_ctx):
    m = qkv.shape[0]
    nc = m // CHUNK
    qw, vw = heads * dk, heads * dv
    assert qw == vw
    bidx = functools.partial(_bwd_chunk, nc_ctx=nc_ctx, nc_tot=nc)

    def specs(row):
        return [pl.BlockSpec((CHUNK, qw), lambda s: (row(s), 0)),
                pl.BlockSpec((CHUNK, qw), lambda s: (row(s), 1)),
                pl.BlockSpec((CHUNK, vw), lambda s: (row(s), 2))]

    def gate_specs(row, direction):
        return [pl.BlockSpec((CHUNK, LANES), lambda s: (row(s), beta_cb + 2 * direction)),
                pl.BlockSpec((CHUNK, LANES), lambda s: (row(s), beta_cb + 2 * direction + 1))]

    fwd = lambda s: s
    return pl.pallas_call(
        functools.partial(_gdn_scan_kernel, heads=heads, dk=dk, dv=dv),
        grid=(nc,),
        in_specs=(specs(fwd) + gate_specs(fwd, 0) + specs(bidx) + gate_specs(bidx, 1)
                  + [pl.BlockSpec((2, 1, LANES), lambda s: (0, 0, 0)),
                     pl.BlockSpec((2, 1, LANES), lambda s: (0, 0, 0))]),
        out_specs=[pl.BlockSpec((CHUNK, vw), lambda s: (s, 0)),
                   pl.BlockSpec((CHUNK, vw), lambda s: (bidx(s), 0))],
        out_shape=[jax.ShapeDtypeStruct((m, vw), F32)] * 2,
        scratch_shapes=[pltpu.VMEM((2 * heads, dk, dv), F32)],
        compiler_params=_cparams(1),
        name="gdn_scan",
    )(qkv, qkv, qkv, p, p, qkv, qkv, qkv, p, p, alog, dtb)


def _mlstm_scan_kernel(qf_ref, kf_ref, vf_ref, if_ref, ff_ref, qb_ref, kb_ref, vb_ref, ib_ref, fb_ref,
                       ibias_ref, fbias_ref, of_ref, ob_ref, c_sc, m_sc, *, heads, dqk, dv):
    @pl.when(pl.program_id(0) == 0)
    def _():
        c_sc[...] = jnp.zeros_like(c_sc)
        m_sc[...] = jnp.zeros_like(m_sc)

    ones_col = (lax.broadcasted_iota(jnp.int32, (CHUNK, LANES), 1) == 0).astype(F32)
    chains = []
    for direction, (q_ref, k_ref, v_ref, i_ref, f_ref, o_ref) in enumerate(
            ((qf_ref, kf_ref, vf_ref, if_ref, ff_ref, of_ref), (qb_ref, kb_ref, vb_ref, ib_ref, fb_ref, ob_ref))):
        incl, _ = _causal_masks(direction)
        log_i = i_ref[...] + ibias_ref[direction]
        log_f = -_softplus(-(f_ref[...] + fbias_ref[direction]))
        bcum = _dot_exact(incl.astype(F32), log_f)
        btot = bcum[CHUNK - 1:CHUNK, :] if direction == 0 else bcum[0:1, :]
        log_end = btot - bcum + log_i
        m_st = m_sc[direction]
        m_new = jnp.maximum(btot + m_st, jnp.max(log_end, axis=0, keepdims=True))
        m_sc[direction] = m_new
        carry = jnp.exp(btot + m_st - m_new)
        k_scale = jnp.exp(log_end - m_new)
        b_inter = bcum + m_st
        bcum_t = bcum.T
        log_i_t = log_i.T
        for h in range(heads):
            log_d = jnp.where(incl, bcum[:, h:h + 1] - bcum_t[h:h + 1, :] + log_i_t[h:h + 1, :], -jnp.inf)
            m_t = jnp.maximum(b_inter[:, h:h + 1], jnp.max(log_d, axis=-1, keepdims=True))
            chains.append(dict(
                q=q_ref[:, h * dqk:(h + 1) * dqk] * (dqk ** -0.5), k=k_ref[:, h * dqk:(h + 1) * dqk],
                v_ext=jnp.concatenate([v_ref[:, h * dv:(h + 1) * dv], ones_col], axis=1),
                p_intra=jnp.exp(log_d - m_t), w_inter=jnp.exp(b_inter[:, h:h + 1] - m_t), floor=jnp.exp(-m_t),
                k_scale=k_scale[:, h:h + 1], carry=carry[:, h:h + 1], o_ref=o_ref, h=h, slot=direction * heads + h))

    s = [_dot_nt(c["q"], c["k"]) * c["p_intra"] for c in chains]
    states = [c_sc[c["slot"]] for c in chains]
    inter = [_dot(c["q"], st) for c, st in zip(chains, states)]
    intra = [_dot(x, c["v_ext"]) for x, c in zip(s, chains)]
    upd = [_dot_tn(c["k"] * c["k_scale"], c["v_ext"]) for c in chains]
    for c, st, x, y, z in zip(chains, states, inter, intra, upd):
        out = c["w_inter"] * x + y
        den = jnp.maximum(jnp.abs(out[:, dv:dv + 1]), c["floor"])
        c["o_ref"][:, c["h"] * dv:(c["h"] + 1) * dv] = out[:, :dv] / den
        c_sc[c["slot"]] = c["carry"] * st + z


def _mlstm_scan_call(p, q_cb, gate_cb, ib, fb, heads, dqk, dv, nc_ctx):
    m = p.shape[0]
    nc = m // CHUNK
    qw, vw = heads * dqk, heads * dv
    assert q_cb % qw == 0 and (q_cb + 2 * qw) % vw == 0
    bidx = functools.partial(_bwd_chunk, nc_ctx=nc_ctx, nc_tot=nc)
    qb, kb, vb = q_cb // qw, q_cb // qw + 1, (q_cb + 2 * qw) // vw
    gb = gate_cb // LANES

    def specs(row, direction):
        return [pl.BlockSpec((CHUNK, qw), lambda s: (row(s), qb)),
                pl.BlockSpec((CHUNK, qw), lambda s: (row(s), kb)),
                pl.BlockSpec((CHUNK, vw), lambda s: (row(s), vb)),
                pl.BlockSpec((CHUNK, LANES), lambda s: (row(s), gb + 2 * direction)),
                pl.BlockSpec((CHUNK, LANES), lambda s: (row(s), gb + 2 * direction + 1))]

    return pl.pallas_call(
        functools.partial(_mlstm_scan_kernel, heads=heads, dqk=dqk, dv=dv),
        grid=(nc,),
        in_specs=(specs(lambda s: s, 0) + specs(bidx, 1)
                  + [pl.BlockSpec((2, 1, LANES), lambda s: (0, 0, 0)),
                     pl.BlockSpec((2, 1, LANES), lambda s: (0, 0, 0))]),
        out_specs=[pl.BlockSpec((CHUNK, vw), lambda s: (s, 0)),
                   pl.BlockSpec((CHUNK, vw), lambda s: (bidx(s), 0))],
        out_shape=[jax.ShapeDtypeStruct((m, vw), F32)] * 2,
        scratch_shapes=[pltpu.VMEM((2 * heads, dqk, dv + LANES), F32), pltpu.VMEM((2, 1, LANES), F32)],
        compiler_params=_cparams(1),
        name="mlstm_scan",
    )(p, p, p, p, p, p, p, p, p, p, ib, fb)


def _filter_kernel(feat_ref, w1_ref, b1_ref, w2_ref, b2_ref, w3_ref, freq_ref, dec_ref, o_ref):
    feats = feat_ref[...]
    h = jnp.sin(freq_ref[0:1, :] * (_dot(feats, w1_ref[...]) + b1_ref[...]))
    h = jnp.sin(freq_ref[1:2, :] * (_dot(h, w2_ref[...]) + b2_ref[...]))
    o_ref[...] = _dot(h, w3_ref[...]) * jnp.exp(-feats[:, 0:1] * jnp.abs(dec_ref[...]))


def _filter_call(feats, w1, b1, w2, b2, w3, freq, dec, tl, tn):
    length, fp = feats.shape
    hid = w2.shape[0]
    n = w3.shape[1]
    return pl.pallas_call(
        _filter_kernel,
        grid=(length // tl, n // tn),
        in_specs=[pl.BlockSpec((tl, fp), lambda i, j: (i, 0)),
                  pl.BlockSpec((fp, hid), lambda i, j: (0, 0)),
                  pl.BlockSpec((1, hid), lambda i, j: (0, 0)),
                  pl.BlockSpec((hid, hid), lambda i, j: (0, 0)),
                  pl.BlockSpec((1, hid), lambda i, j: (0, 0)),
                  pl.BlockSpec((hid, tn), lambda i, j: (0, j)),
                  pl.BlockSpec((2, hid), lambda i, j: (0, 0)),
                  pl.BlockSpec((1, tn), lambda i, j: (0, j))],
        out_specs=pl.BlockSpec((tl, tn), lambda i, j: (i, j)),
        out_shape=jax.ShapeDtypeStruct((length, n), F32),
        compiler_params=_cparams(2),
        name="hyena_filter",
    )(feats, w1, b1, w2, b2, w3, freq, dec)


def _dft_constants(length):
    n2 = DFT_N2
    n = 2 * length
    n1 = n // n2
    kh = n1 // 2
    na = kh + 1

    def cis(num, den):
        ang = (2.0 * math.pi / den) * (num % den).astype(F32)
        return jnp.cos(ang), jnp.sin(ang)

    ar = lambda size: jnp.arange(size, dtype=jnp.int32)
    c, s = cis(ar(na)[None, :, None] * (n2 * ar(kh)[None, None, :] + ar(n2)[:, None, None]), n)
    m1 = jnp.stack([c, -s], axis=2).reshape(n2, 2 * na, kh)
    c, s = cis(ar(n2)[:, None] * ar(n2)[None, :], n2)
    g2 = jnp.block([[c, s], [-s, c]])
    c, s = cis(ar(n2)[None, :, None] * (ar(na)[:, None, None] + n1 * ar(n2)[None, None, :]), n)
    minv = jnp.concatenate([jnp.concatenate([c, -s], axis=2), jnp.concatenate([s, c], axis=2)], axis=1)
    c, s = cis(ar(kh)[:, None] * ar(na)[None, :], n1)
    weight = jnp.where(jnp.logical_or(ar(na) == 0, ar(na) == kh), 1.0, 2.0)[None, :] / n
    pinv = jnp.stack([c * weight, -s * weight], axis=2).reshape(kh, 2 * na)
    return m1.astype(BF16), g2.astype(BF16), minv.astype(BF16), pinv.astype(BF16)


def _dft1_kernel(x_ref, m_ref, o_ref):
    for j in range(SUBLANES):
        o_ref[:, j, :] = _dot(m_ref[j], x_ref[:, j, :])


def _dft1_call(x3, col_blk0, width, m1, tc):
    kh = x3.shape[0]
    n2, rows, _ = m1.shape
    return pl.pallas_call(
        _dft1_kernel,
        grid=(width // tc, n2 // SUBLANES),
        in_specs=[pl.BlockSpec((kh, SUBLANES, tc), lambda c, g: (0, g, col_blk0 + c)),
                  pl.BlockSpec((SUBLANES, rows, kh), lambda c, g: (g, 0, 0))],
        out_specs=pl.BlockSpec((rows, SUBLANES, tc), lambda c, g: (0, g, c)),
        out_shape=jax.ShapeDtypeStruct((rows, n2, width), F32),
        compiler_params=_cparams(2),
        name="dft_stage1",
    )(x3, m1)


def _filter_spectrum_kernel(af_ref, ab_ref, g_ref, o_ref):
    sf = _dot(g_ref[...], af_ref[...])
    sb = _dot(g_ref[...], ab_ref[...])
    sign = jnp.where(lax.broadcasted_iota(jnp.int32, (2 * DFT_N2, 1), 0) < DFT_N2, 1.0, -1.0)
    o_ref[...] = sf + sign * sb


def _filter_spectrum_call(a, g2, tc):
    rows, width = a.shape
    half = width // 2
    blk = 2 * DFT_N2
    return pl.pallas_call(
        _filter_spectrum_kernel,
        grid=(rows // blk, half // tc),
        in_specs=[pl.BlockSpec((blk, tc), lambda a_, j: (a_, j)),
                  pl.BlockSpec((blk, tc), lambda a_, j: (a_, half // tc + j)),
                  pl.BlockSpec((blk, blk), lambda a_, j: (0, 0))],
        out_specs=pl.BlockSpec((blk, tc), lambda a_, j: (a_, j)),
        out_shape=jax.ShapeDtypeStruct((rows, half), F32),
        compiler_params=_cparams(2),
        name="hyena_filter_spectrum",
    )(a, a, g2)


def _complex_mul(x, kf, n):
    xr, xi = x[:n], x[n:]
    kr, ki = kf[:n], kf[n:]
    return jnp.concatenate([xr * kr - xi * ki, xr * ki + xi * kr], axis=0)


def _spectral_mid_kernel(a_ref, g_ref, kf_ref, minv_ref, o_ref):
    x = _dot(g_ref[...], a_ref[...])
    o_ref[...] = _dot(minv_ref[0], _complex_mul(x, kf_ref[...], DFT_N2))


def _spectral_mid_call(a, g2, kf, kf_cb, minv, tc):
    rows, width = a.shape
    blk = 2 * DFT_N2
    return pl.pallas_call(
        _spectral_mid_kernel,
        grid=(rows // blk, width // tc),
        in_specs=[pl.BlockSpec((blk, tc), lambda a_, j: (a_, j)),
                  pl.BlockSpec((blk, blk), lambda a_, j: (0, 0)),
                  pl.BlockSpec((blk, tc), lambda a_, j: (a_, kf_cb + j)),
                  pl.BlockSpec((1, blk, blk), lambda a_, j: (a_, 0, 0))],
        out_specs=pl.BlockSpec((blk, tc), lambda a_, j: (a_, j)),
        out_shape=jax.ShapeDtypeStruct((rows, width), F32),
        compiler_params=_cparams(2),
        name="hyena_spectral_mid",
    )(a, g2, kf, minv)


def _idft2_kernel(z_ref, p_ref, v_ref, gate_ref, bias_ref, o_ref):
    for j in range(SUBLANES):
        y = _dot(p_ref[...], z_ref[:, j, :])
        o_ref[:, j, :] = gate_ref[:, j, :] * (y + v_ref[:, j, :] * bias_ref[...])


def _idft2_call(z3, pinv, v3, v_cb, gate3, gate_cb, bias, width, tc):
    rows, n2, _ = z3.shape
    kh = pinv.shape[0]
    return pl.pallas_call(
        _idft2_kernel,
        grid=(width // tc, n2 // SUBLANES),
        in_specs=[pl.BlockSpec((rows, SUBLANES, tc), lambda c, g: (0, g, c)),
                  pl.BlockSpec((kh, rows), lambda c, g: (0, 0)),
                  pl.BlockSpec((kh, SUBLANES, tc), lambda c, g: (0, g, v_cb + c)),
                  pl.BlockSpec((kh, SUBLANES, tc), lambda c, g: (0, g, gate_cb + c)),
                  pl.BlockSpec((1, tc), lambda c, g: (0, c))],
        out_specs=pl.BlockSpec((kh, SUBLANES, tc), lambda c, g: (0, g, c)),
        out_shape=jax.ShapeDtypeStruct((kh, n2, width), F32),
        compiler_params=_cparams(2),
        name="idft_stage2_gate",
    )(z3, pinv, v3, gate3, bias)


def _long_conv(v3, v_cb, gate3, gate_cb, bias, kf, kf_cb, consts, width, tc):
    m1, g2, minv, pinv = consts
    a = _dft1_call(v3, v_cb, width, m1, tc)
    rows = a.shape[0]
    z = _spectral_mid_call(a.reshape(rows * DFT_N2, width), g2, kf, kf_cb, minv, tc)
    return _idft2_call(z.reshape(rows, DFT_N2, width), pinv, v3, v_cb, gate3, gate_cb, bias, width, tc)


def _dense_dft_constants(length):
    n = 2 * length
    k = jnp.arange(n, dtype=jnp.int32)[:, None]
    t = jnp.arange(length, dtype=jnp.int32)[None, :]
    ang = (2.0 * math.pi / n) * ((k * t) % n).astype(F32)
    c, s = jnp.cos(ang), jnp.sin(ang)
    fwd = jnp.concatenate([c, -s], axis=0)
    inv = jnp.concatenate([c.T, -s.T], axis=1) / n
    return fwd.astype(BF16), inv.astype(BF16)


def _dense_spectrum_kernel(tf_ref, tb_ref, f_ref, o_ref, *, n):
    sign = jnp.where(lax.broadcasted_iota(jnp.int32, (2 * n, 1), 0) < n, 1.0, -1.0)
    o_ref[...] = _dot(f_ref[...], tf_ref[...]) + sign * _dot(f_ref[...], tb_ref[...])


def _dense_spectrum_call(taps, fwd, tc):
    length, width = taps.shape
    half = width // 2
    rows = fwd.shape[0]
    return pl.pallas_call(
        functools.partial(_dense_spectrum_kernel, n=rows // 2),
        grid=(half // tc,),
        in_specs=[pl.BlockSpec((length, tc), lambda j: (0, j)),
                  pl.BlockSpec((length, tc), lambda j: (0, half // tc + j)),
                  pl.BlockSpec((rows, length), lambda j: (0, 0))],
        out_specs=pl.BlockSpec((rows, tc), lambda j: (0, j)),
        out_shape=jax.ShapeDtypeStruct((rows, half), F32),
        compiler_params=_cparams(1),
        name="hyena_filter_spectrum_dense",
    )(taps, taps, fwd)


def _dense_hyena_kernel(v_ref, x1_ref, x2_ref, kf0_ref, kf1_ref, f_ref, inv_ref, b0_ref, b1_ref, o_ref, *, n):
    def conv(u, kf_ref, b_ref):
        y = _dot(inv_ref[...], _complex_mul(_dot(f_ref[...], u), kf_ref[...], n))
        return y + u * b_ref[...]

    z = x1_ref[...] * conv(v_ref[...], kf0_ref, b0_ref)
    o_ref[...] = x2_ref[...] * conv(z, kf1_ref, b1_ref)


def _dense_hyena_call(hc, length, width, kf, fwd, inv, bias0, bias1, tc):
    rows = fwd.shape[0]
    per = width // tc
    return pl.pallas_call(
        functools.partial(_dense_hyena_kernel, n=rows // 2),
        grid=(per,),
        in_specs=[pl.BlockSpec((length, tc), lambda j: (0, j)),
                  pl.BlockSpec((length, tc), lambda j: (0, per + j)),
                  pl.BlockSpec((length, tc), lambda j: (0, 2 * per + j)),
                  pl.BlockSpec((rows, tc), lambda j: (0, j)),
                  pl.BlockSpec((rows, tc), lambda j: (0, per + j)),
                  pl.BlockSpec((rows, length), lambda j: (0, 0)),
                  pl.BlockSpec((length, rows), lambda j: (0, 0)),
                  pl.BlockSpec((1, tc), lambda j: (0, j)),
                  pl.BlockSpec((1, tc), lambda j: (0, j))],
        out_specs=pl.BlockSpec((length, tc), lambda j: (0, j)),
        out_shape=jax.ShapeDtypeStruct((length, width), F32),
        compiler_params=_cparams(1),
        name="hyena_dense",
    )(hc, hc, hc, kf, kf, fwd, inv, bias0, bias1)


def _filter_features(length):
    t = jnp.linspace(0.0, 1.0, length, dtype=F32)[:, None]
    w = (2.0 * math.pi / length) * jnp.arange(length, dtype=F32)[:, None]
    bands = jnp.linspace(1e-4, HYENA_BANDS - 1, HYENA_BANDS, dtype=F32)[None, :]
    feats = jnp.concatenate([t, jnp.cos(bands * w), -jnp.sin(bands * w)], axis=-1)
    return jnp.pad(feats, ((0, 0), (0, LANES - feats.shape[1])))


def _head_rms(x, heads, width):
    outs = []
    for h in range(heads):
        xh = x[:, h * width:(h + 1) * width]
        outs.append(xh * lax.rsqrt(jnp.mean(xh * xh, axis=-1, keepdims=True) + RMS_EPS))
    return jnp.concatenate(outs, axis=1)


def _merge_kernel(ogf_ref, ogb_ref, omf_ref, omb_ref, z_ref, o_ref, hy_ref, g0_ref, g1_ref, g2_ref, gn_ref, mn_ref,
                  wa_ref, wb_ref, wc_ref, out_ref, *, gdn_heads, gdn_dv, ml_heads, ml_dv):
    a = _head_rms(ogf_ref[...] + ogb_ref[...], gdn_heads, gdn_dv) * gn_ref[...] * _silu(z_ref[...])
    b = _head_rms(omf_ref[...] + omb_ref[...], ml_heads, ml_dv) * mn_ref[...] * _sigmoid(o_ref[...])
    out_ref[...] = (_sigmoid(g0_ref[...]) * _dot(a, wa_ref[...]) + _sigmoid(g1_ref[...]) * _dot(b, wb_ref[...])
                    + _sigmoid(g2_ref[...]) * _dot(hy_ref[...], wc_ref[...]))


def _merge_call(og, om, p, z_cb, o_cb, hy, gate_cb, gn, mn, wa, wb, wc, layer, heads, tm):
    m, d = p.shape[0], wa.shape[-1]
    gw, mw, hw = og[0].shape[-1], om[0].shape[-1], hy.shape[-1]
    gdn_heads, gdn_dv, ml_heads, ml_dv = heads
    zb, ob, gb = z_cb // gw, o_cb // mw, gate_cb // d
    return pl.pallas_call(
        functools.partial(_merge_kernel, gdn_heads=gdn_heads, gdn_dv=gdn_dv, ml_heads=ml_heads, ml_dv=ml_dv),
        grid=(m // tm,),
        in_specs=[pl.BlockSpec((tm, gw), lambda i: (i, 0)),
                  pl.BlockSpec((tm, gw), lambda i: (i, 0)),
                  pl.BlockSpec((tm, mw), lambda i: (i, 0)),
                  pl.BlockSpec((tm, mw), lambda i: (i, 0)),
                  pl.BlockSpec((tm, gw), lambda i: (i, zb)),
                  pl.BlockSpec((tm, mw), lambda i: (i, ob)),
                  pl.BlockSpec((tm, hw), lambda i: (i, 0)),
                  pl.BlockSpec((tm, d), lambda i: (i, gb)),
                  pl.BlockSpec((tm, d), lambda i: (i, gb + 1)),
                  pl.BlockSpec((tm, d), lambda i: (i, gb + 2)),
                  pl.BlockSpec((1, gw), lambda i: (0, 0)),
                  pl.BlockSpec((1, mw), lambda i: (0, 0)),
                  pl.BlockSpec((None, gw, d), lambda i: (layer, 0, 0)),
                  pl.BlockSpec((None, mw, d), lambda i: (layer, 0, 0)),
                  pl.BlockSpec((None, hw, d), lambda i: (layer, 0, 0))],
        out_specs=pl.BlockSpec((tm, d), lambda i: (i, 0)),
        out_shape=jax.ShapeDtypeStruct((m, d), F32),
        compiler_params=_cparams(1),
        name="branch_merge",
    )(og[0], og[1], om[0], om[1], p, p, hy, p, p, p, gn, mn, wa, wb, wc)


def _outproj_kernel(y_ref, x_ref, m_ref, w_ref, o_ref, *, tm, ctx_len):
    gate = jnp.where(_row_is_ctx(pl.program_id(0) * tm, tm, ctx_len), m_ref[0:1, :], m_ref[1:2, :])
    o_ref[...] = x_ref[...] + gate * _dot(y_ref[...], w_ref[...])


def _outproj_call(y, x, mods, w, layer, ctx_len, tm):
    m, d = x.shape
    return pl.pallas_call(
        functools.partial(_outproj_kernel, tm=tm, ctx_len=ctx_len),
        grid=(m // tm,),
        in_specs=[pl.BlockSpec((tm, d), lambda i: (i, 0)),
                  pl.BlockSpec((tm, d), lambda i: (i, 0)),
                  pl.BlockSpec((SUBLANES, d), lambda i: (0, 0)),
                  pl.BlockSpec((None, d, d), lambda i: (layer, 0, 0))],
        out_specs=pl.BlockSpec((tm, d), lambda i: (i, 0)),
        out_shape=jax.ShapeDtypeStruct((m, d), F32),
        compiler_params=_cparams(1),
        name="mixer_out_proj",
    )(y, x, mods, w)


def _final_norm_kernel(x_ref, g_ref, o_ref):
    x = x_ref[...]
    o_ref[...] = x * lax.rsqrt(jnp.mean(x * x, axis=-1, keepdims=True) + RMS_EPS) * g_ref[...]


def _final_norm_call(x, g, row0, rows, tm):
    d = x.shape[1]
    return pl.pallas_call(
        _final_norm_kernel,
        grid=(rows // tm,),
        in_specs=[pl.BlockSpec((tm, d), lambda i: (row0 // tm + i, 0)),
                  pl.BlockSpec((1, d), lambda i: (0, 0))],
        out_specs=pl.BlockSpec((tm, d), lambda i: (i, 0)),
        out_shape=jax.ShapeDtypeStruct((rows, d), F32),
        compiler_params=_cparams(1),
        name="final_rms_norm",
    )(x, g)


def _pack_rows(rows, d):
    out = jnp.concatenate([r.reshape(1, d) for r in rows], axis=0)
    return jnp.pad(out, ((0, SUBLANES - out.shape[0]), (0, 0)))


def _lane_row(vals, lanes=LANES):
    return jnp.pad(vals.astype(F32), ((0, 0), (0, lanes - vals.shape[1])))[:, None, :]


def _pick_tile(total, candidates):
    for c in candidates:
        if total % c == 0:
            return c
    raise ValueError(f"no tile for {total} among {candidates}")


def kernel(x, c, ctx, c_ctx, w_ada, b_ada, norm_g, ffn_w_gate, ffn_w_up, ffn_w_down, w_in, gdn_conv, gdn_a_log,
           gdn_dt_bias, gdn_norm, mlstm_i_bias, mlstm_f_bias, mlstm_norm, hy_short_w, hy_short_b, hf_w1, hf_b1,
           hf_w2, hf_b2, hf_w3, hf_freq, hf_decay, hy_bias, w_a_out, w_b_out, w_c_out, w_out, final_norm):
    batch, seq, d = x.shape
    assert batch == 1 and c.shape[0] == 1 and ctx.shape[0] == 1
    ctx_len = ctx.shape[1]
    depth = w_ada.shape[0]
    m = ctx_len + seq
    assert ctx_len % CHUNK == 0 and seq % CHUNK == 0

    gdn_heads = gdn_a_log.shape[-1]
    gdn_vw = w_a_out.shape[1]
    gdn_qkw = (gdn_conv.shape[1] - gdn_vw) // 2
    gdn_dk, gdn_dv = gdn_qkw // gdn_heads, gdn_vw // gdn_heads
    ml_heads = mlstm_i_bias.shape[-1]
    ml_vw = w_b_out.shape[1]
    hy_w = w_c_out.shape[1]
    n_in = w_in.shape[-1]
    ml_qkw = (n_in - (2 * gdn_qkw + gdn_vw) - 4 * gdn_heads - 4 * ml_heads - ml_vw - gdn_vw - ml_vw
              - 3 * hy_w - 3 * d) // 2
    ml_dqk, ml_dv = ml_qkw // ml_heads, ml_vw // ml_heads
    assert gdn_dk == LANES and gdn_dv == LANES and ml_dqk == LANES and ml_dv % LANES == 0

    o_gqkv = 0
    o_gbeta = o_gqkv + 2 * gdn_qkw + gdn_vw
    o_ga = o_gbeta + 2 * gdn_heads
    o_mqkv = o_ga + 2 * gdn_heads
    o_mi = o_mqkv + 2 * ml_qkw + ml_vw
    o_mf = o_mi + 2 * ml_heads
    o_z = o_mf + 2 * ml_heads
    o_o = o_z + gdn_vw
    o_hy = o_o + ml_vw
    o_gates = o_hy + 3 * hy_w
    assert o_gates + 3 * d == n_in

    def small(off, heads):
        return [jnp.pad(w_in[:, :, off + dd * heads:off + (dd + 1) * heads], ((0, 0), (0, 0), (0, LANES - heads)))
                for dd in range(2)]

    gb, ga = small(o_gbeta, gdn_heads), small(o_ga, gdn_heads)
    mi, mf = small(o_mi, ml_heads), small(o_mf, ml_heads)
    groups = [w_in[:, :, o_gqkv:o_gbeta], w_in[:, :, o_mqkv:o_mi], w_in[:, :, o_z:o_o], w_in[:, :, o_o:o_hy],
              w_in[:, :, o_hy:o_gates], w_in[:, :, o_gates:],
              gb[0], ga[0], gb[1], ga[1], mi[0], mf[0], mi[1], mf[1]]
    w_in_p = jnp.concatenate(groups, axis=-1).astype(BF16)
    c_gqkv = 0
    c_mqkv = c_gqkv + 2 * gdn_qkw + gdn_vw
    c_z = c_mqkv + 2 * ml_qkw + ml_vw
    c_o = c_z + gdn_vw
    c_hy = c_o + ml_vw
    c_gates = c_hy + 3 * hy_w
    c_gsmall = c_gates + 3 * d
    c_msmall = c_gsmall + 4 * LANES
    n_p = c_msmall + 4 * LANES
    assert w_in_p.shape[-1] == n_p

    f_hidden = ffn_w_gate.shape[-1]
    tf = 256
    f_pad = -(-f_hidden // tf) * tf
    wg = jnp.pad(ffn_w_gate, ((0, 0), (0, 0), (0, 0), (0, f_pad - f_hidden))).astype(BF16)
    wu = jnp.pad(ffn_w_up, ((0, 0), (0, 0), (0, 0), (0, f_pad - f_hidden))).astype(BF16)
    wd = jnp.pad(ffn_w_down, ((0, 0), (0, 0), (0, f_pad - f_hidden), (0, 0))).astype(BF16)
    wa, wb, wc, wo = (t.astype(BF16) for t in (w_a_out, w_b_out, w_c_out, w_out))

    tm_big = _pick_tile(m, (768, 512, 384, 256, 128, 64))
    tm_mid = _pick_tile(m, (256, 128, 64))
    tm_small = _pick_tile(m, (128, 64))
    tn_in = _pick_tile(n_p, (512, 256, 128))
    nc_ctx = ctx_len // CHUNK

    cond = jnp.pad(jnp.concatenate([c_ctx[None, :], c], axis=0), ((0, SUBLANES - 2), (0, 0)))
    mods = _modulation_call(cond, w_ada, b_ada)

    dft_consts = _dft_constants(seq)
    dense_fwd, dense_inv = _dense_dft_constants(ctx_len)
    feats_lat = _filter_features(seq)
    feats_ctx = _filter_features(ctx_len)
    tl = _pick_tile(seq, (512, 256, 128, 64))

    s = jnp.concatenate([ctx[0], x[0]], axis=0)
    for l in range(depth):
        last = l == depth - 1

        def mod(idx):
            return [mods[l, 0, idx * d:(idx + 1) * d], mods[l, 1, idx * d:(idx + 1) * d]]

        s = _ffn_call(s, _pack_rows(mod(0) + mod(1) + mod(2), d), norm_g[l, 0][None, :], wg, wu, wd, l, 0,
                      ctx_len, tm_big, tf)
        p = _inproj_call(s, _pack_rows(mod(3) + mod(4), d), norm_g[l, 1][None, :], w_in_p[l], ctx_len, tm_big, tn_in)

        gqkv = _dwconv_call(p, c_gqkv, 2 * gdn_qkw + gdn_vw, jnp.transpose(gdn_conv[l]),
                            jnp.zeros((1, 2 * gdn_qkw + gdn_vw), F32), ctx_len, tm_mid, 512, True)
        hc = _dwconv_call(p, c_hy, 3 * hy_w, jnp.transpose(hy_short_w[l]), hy_short_b[l][None, :],
                          ctx_len, tm_mid, 512, False)

        og = _gdn_scan_call(gqkv, p, c_gsmall // LANES, _lane_row(gdn_a_log[l]), _lane_row(gdn_dt_bias[l]),
                            gdn_heads, gdn_dk, gdn_dv, nc_ctx)
        om = _mlstm_scan_call(p, c_mqkv, c_msmall, _lane_row(mlstm_i_bias[l]), _lane_row(mlstm_f_bias[l]),
                              ml_heads, ml_dqk, ml_dv, nc_ctx)

        fargs = (jnp.pad(hf_w1[l], ((0, LANES - hf_w1.shape[1]), (0, 0))), hf_b1[l][None, :], hf_w2[l],
                 hf_b2[l][None, :], hf_w3[l], hf_freq[l], jnp.tile(hf_decay[l], 2)[None, :])
        taps = _filter_call(feats_lat, *fargs, tl, 1024)
        m1, g2, minv, pinv = dft_consts
        kh = seq // DFT_N2
        ta = _dft1_call(taps.reshape(kh, DFT_N2, taps.shape[1]), 0, taps.shape[1], m1, 1024)
        kf = _filter_spectrum_call(ta.reshape(ta.shape[0] * DFT_N2, taps.shape[1]), g2, 1024)
        hl3 = hc[ctx_len:].reshape(kh, DFT_N2, 3 * hy_w)
        tc_h = min(hy_w, 1024)
        per = hy_w // tc_h
        z1 = _long_conv(hl3, 0, hl3, per, hy_bias[l, 0][None, :], kf, 0, dft_consts, hy_w, tc_h)
        z2 = _long_conv(z1, 0, hl3, 2 * per, hy_bias[l, 1][None, :], kf, per, dft_consts, hy_w, tc_h)
        z2 = z2.reshape(seq, hy_w)
        if not last:
            taps_c = _filter_call(feats_ctx, *fargs, _pick_tile(ctx_len, (256, 128, 64)), 1024)
            kf_c = _dense_spectrum_call(taps_c, dense_fwd, 512)
            zc = _dense_hyena_call(hc[:ctx_len], ctx_len, hy_w, kf_c, dense_fwd, dense_inv, hy_bias[l, 0][None, :],
                                   hy_bias[l, 1][None, :], 256)
        else:
            zc = jnp.zeros((ctx_len, hy_w), F32)
        hy = jnp.concatenate([zc, z2], axis=0)

        merged = _merge_call(og, om, p, c_z, c_o, hy, c_gates, jnp.tile(gdn_norm[l], gdn_heads)[None, :],
                             mlstm_norm[l][None, :], wa, wb, wc, l, (gdn_heads, gdn_dv, ml_heads, ml_dv), tm_small)
        s = _outproj_call(merged, s, _pack_rows(mod(5), d), wo, l, ctx_len, tm_mid)
        s = _ffn_call(s, _pack_rows(mod(6) + mod(7) + mod(8), d), norm_g[l, 2][None, :], wg, wu, wd, l, 1,
                      ctx_len, tm_big, tf)

    tm_fin = _pick_tile(math.gcd(ctx_len, seq), (256, 128, 64))
    out = _final_norm_call(s, final_norm[None, :], ctx_len, seq, tm_fin)
    return out[None]
```

```python
import functools
import math

import jax
import jax.numpy as jnp
import numpy as np
from jax import lax
from jax.experimental import pallas as pl
from jax.experimental.pallas import tpu as pltpu

F32 = jnp.float32
BF16 = jnp.bfloat16

N_MOD = 9
RMS_EPS = 1e-6
L2_EPS = 1e-6
CHUNK = 64
GDN_CONV = 5
HYENA_SHORT = 3
HYENA_BANDS = 16
LANES = 128
SUBLANES = 8
DFT_N2 = 128
VMEM_LIMIT = 56 * 1024 * 1024


def _cparams(n_axes):
    return pltpu.CompilerParams(dimension_semantics=("arbitrary",) * n_axes,
                                vmem_limit_bytes=VMEM_LIMIT)


def _sigmoid(x):
    return jax.nn.sigmoid(x)


def _silu(x):
    return x * jax.nn.sigmoid(x)


def _softplus(x):
    return jnp.maximum(x, 0.0) + jnp.log(1.0 + jnp.exp(-jnp.abs(x)))


def _dot(a, b):
    return jnp.dot(a.astype(BF16), b.astype(BF16), preferred_element_type=F32)


def _dot_nt(a, b):
    return lax.dot_general(a.astype(BF16), b.astype(BF16), (((1,), (1,)), ((), ())),
                           preferred_element_type=F32)


def _dot_tn(a, b):
    return lax.dot_general(a.astype(BF16), b.astype(BF16), (((0,), (0,)), ((), ())),
                           preferred_element_type=F32)


def _dot_exact(a, b):
    return jnp.dot(a, b, preferred_element_type=F32, precision=lax.Precision.HIGHEST)


def _row_is_ctx(row0, rows, ctx_len):
    r = row0 + lax.broadcasted_iota(jnp.int32, (rows, 1), 0)
    return r < ctx_len


def _modnorm(x, g, shift, scale):
    y = x * lax.rsqrt(jnp.mean(x * x, axis=-1, keepdims=True) + RMS_EPS) * g
    return y * (1.0 + scale) + shift


ROW_CHUNK = 128


def _for_row_chunks(tm, fn):
    rc = math.gcd(tm, ROW_CHUNK)

    def body(c, carry):
        fn(pl.multiple_of(c * rc, rc), rc)
        return carry

    lax.fori_loop(0, tm // rc, body, 0)


def _store_modulated(x_ref, m_ref, g_ref, h_sc, row0, tm, ctx_len):
    def chunk(r0, rc):
        is_ctx = _row_is_ctx(row0 + r0, rc, ctx_len)
        shift = jnp.where(is_ctx, m_ref[0:1, :], m_ref[1:2, :])
        scale = jnp.where(is_ctx, m_ref[2:3, :], m_ref[3:4, :])
        h_sc[pl.ds(r0, rc), :] = _modnorm(x_ref[pl.ds(r0, rc), :], g_ref[...], shift, scale).astype(h_sc.dtype)

    _for_row_chunks(tm, chunk)


def _mod_kernel(c_ref, w_ref, b_ref, o_ref):
    o_ref[0] = _dot(_silu(c_ref[...]), w_ref[0]) + b_ref[0]


def _modulation_call(cond, w_ada, b_ada):
    depth, d, n = w_ada.shape
    tn = _pick_tile(n, (1024, 512, 256, 128))
    return pl.pallas_call(
        _mod_kernel,
        grid=(depth, n // tn),
        in_specs=[pl.BlockSpec((SUBLANES, d), lambda l, j: (0, 0)),
                  pl.BlockSpec((1, d, tn), lambda l, j: (l, 0, j)),
                  pl.BlockSpec((1, 1, tn), lambda l, j: (l, 0, j))],
        out_specs=pl.BlockSpec((1, SUBLANES, tn), lambda l, j: (l, 0, j)),
        out_shape=jax.ShapeDtypeStruct((depth, SUBLANES, n), F32),
        compiler_params=_cparams(2),
        name="adaln_modulation",
    )(cond, w_ada, b_ada.reshape(depth, 1, n))


def _ffn_kernel(x_ref, m_ref, g_ref, wg_ref, wu_ref, wd_ref, o_ref, h_sc, acc_sc, *, tm, ctx_len):
    row0 = pl.program_id(0) * tm
    f = pl.program_id(1)

    @pl.when(f == 0)
    def _():
        _store_modulated(x_ref, m_ref, g_ref, h_sc, row0, tm, ctx_len)
        acc_sc[...] = jnp.zeros_like(acc_sc)

    h = h_sc[...]
    a = _silu(_dot(h, wg_ref[...])) * _dot(h, wu_ref[...])
    acc_sc[...] += _dot(a, wd_ref[...])

    @pl.when(f == pl.num_programs(1) - 1)
    def _():
        def chunk(r0, rc):
            gate = jnp.where(_row_is_ctx(row0 + r0, rc, ctx_len), m_ref[4:5, :], m_ref[5:6, :])
            rows = pl.ds(r0, rc)
            o_ref[rows, :] = x_ref[rows, :] + 0.5 * gate * acc_sc[rows, :]

        _for_row_chunks(tm, chunk)


def _ffn_call(x, mods, g, wg, wu, wd, layer, which, ctx_len, tm, tf):
    m, d = x.shape
    fp = wg.shape[-1]
    return pl.pallas_call(
        functools.partial(_ffn_kernel, tm=tm, ctx_len=ctx_len),
        grid=(m // tm, fp // tf),
        in_specs=[pl.BlockSpec((tm, d), lambda i, f: (i, 0)),
                  pl.BlockSpec((SUBLANES, d), lambda i, f: (0, 0)),
                  pl.BlockSpec((1, d), lambda i, f: (0, 0)),
                  pl.BlockSpec((None, None, d, tf), lambda i, f: (layer, which, 0, f)),
                  pl.BlockSpec((None, None, d, tf), lambda i, f: (layer, which, 0, f)),
                  pl.BlockSpec((None, None, tf, d), lambda i, f: (layer, which, f, 0))],
        out_specs=pl.BlockSpec((tm, d), lambda i, f: (i, 0)),
        out_shape=jax.ShapeDtypeStruct((m, d), F32),
        scratch_shapes=[pltpu.VMEM((tm, d), BF16), pltpu.VMEM((tm, d), F32)],
        compiler_params=_cparams(2),
        name="macaron_swiglu",
    )(x, mods, g, wg, wu, wd)


def _inproj_kernel(x_ref, m_ref, g_ref, w_ref, o_ref, h_sc, *, tm, ctx_len):
    @pl.when(pl.program_id(1) == 0)
    def _():
        _store_modulated(x_ref, m_ref, g_ref, h_sc, pl.program_id(0) * tm, tm, ctx_len)

    o_ref[...] = _dot(h_sc[...], w_ref[...])


def _inproj_call(x, mods, g, w, ctx_len, tm, tn):
    m, d = x.shape
    n = w.shape[1]
    return pl.pallas_call(
        functools.partial(_inproj_kernel, tm=tm, ctx_len=ctx_len),
        grid=(m // tm, n // tn),
        in_specs=[pl.BlockSpec((tm, d), lambda i, j: (i, 0)),
                  pl.BlockSpec((SUBLANES, d), lambda i, j: (0, 0)),
                  pl.BlockSpec((1, d), lambda i, j: (0, 0)),
                  pl.BlockSpec((d, tn), lambda i, j: (0, j))],
        out_specs=pl.BlockSpec((tm, tn), lambda i, j: (i, j)),
        out_shape=jax.ShapeDtypeStruct((m, n), F32),
        scratch_shapes=[pltpu.VMEM((tm, d), BF16)],
        compiler_params=_cparams(2),
        name="mixer_in_proj",
    )(x, mods, g, w)


def _dwconv_kernel(prev_ref, cur_ref, next_ref, w_ref, b_ref, o_ref, ext_sc, *, tm, taps, ctx_len, m_rows, act):
    row0 = pl.program_id(0) * tm
    at_start = jnp.logical_or(row0 == 0, row0 == ctx_len)
    at_end = jnp.logical_or(row0 + tm == ctx_len, row0 + tm == m_rows)
    ext_sc[0:SUBLANES, :] = jnp.where(at_start, 0.0, prev_ref[...])
    ext_sc[SUBLANES:SUBLANES + tm, :] = cur_ref[...]
    ext_sc[SUBLANES + tm:2 * SUBLANES + tm, :] = jnp.where(at_end, 0.0, next_ref[...])
    acc = jnp.zeros(o_ref.shape, F32) + b_ref[...]
    for t in range(taps):
        acc = acc + ext_sc[pl.ds(SUBLANES + t - taps // 2, tm), :] * w_ref[t:t + 1, :]
    o_ref[...] = _silu(acc) if act else acc


def _dwconv_call(p, col0, ncols, w_t, b, ctx_len, tm, tc, act):
    m = p.shape[0]
    taps = w_t.shape[0]
    assert ctx_len % tm == 0 and m % tm == 0 and col0 % tc == 0 and taps // 2 <= SUBLANES
    cb0 = col0 // tc
    rb = tm // SUBLANES
    last = m // SUBLANES - 1
    return pl.pallas_call(
        functools.partial(_dwconv_kernel, tm=tm, taps=taps, ctx_len=ctx_len, m_rows=m, act=act),
        grid=(m // tm, ncols // tc),
        in_specs=[pl.BlockSpec((SUBLANES, tc), lambda i, j: (jnp.maximum(i * rb - 1, 0), cb0 + j)),
                  pl.BlockSpec((tm, tc), lambda i, j: (i, cb0 + j)),
                  pl.BlockSpec((SUBLANES, tc), lambda i, j: (jnp.minimum((i + 1) * rb, last), cb0 + j)),
                  pl.BlockSpec((taps, tc), lambda i, j: (0, j)),
                  pl.BlockSpec((1, tc), lambda i, j: (0, j))],
        out_specs=pl.BlockSpec((tm, tc), lambda i, j: (i, j)),
        out_shape=jax.ShapeDtypeStruct((m, ncols), F32),
        scratch_shapes=[pltpu.VMEM((tm + 2 * SUBLANES, tc), F32)],
        compiler_params=_cparams(2),
        name="depthwise_conv",
    )(p, p, p, w_t, b)


def _bwd_chunk(s, nc_ctx, nc_tot):
    return jnp.where(s < nc_ctx, nc_ctx - 1 - s, nc_tot - 1 - (s - nc_ctx))


def _causal_masks(direction):
    ii = lax.broadcasted_iota(jnp.int32, (CHUNK, CHUNK), 0)
    jj = lax.broadcasted_iota(jnp.int32, (CHUNK, CHUNK), 1)
    rel = ii - jj if direction == 0 else jj - ii
    return rel >= 0, rel > 0


def _unit_triangular_inverses(mats):
    ii = lax.broadcasted_iota(jnp.int32, (CHUNK, CHUNK), 0)
    jj = lax.broadcasted_iota(jnp.int32, (CHUNK, CHUNK), 1)
    eye = (ii == jj).astype(F32)

    def same_block(width):
        return (ii // width) == (jj // width)

    inner = same_block(8)
    d1 = [jnp.where(inner, a, 0.0) for a in mats]
    d2 = [_dot(x, x) for x in d1]
    d4 = [_dot(x, x) for x in d2]
    p = [_dot(eye - x, eye + y) for x, y in zip(d1, d2)]
    p = [_dot(x, eye + y) for x, y in zip(p, d4)]
    for width in (16, 32, 64):
        outer = same_block(width)
        ring = jnp.logical_and(outer, jnp.logical_not(inner))
        t = [_dot(x, jnp.where(ring, a, 0.0)) for x, a in zip(p, mats)]
        p = [x - _dot(y, x) for x, y in zip(p, t)]
        inner = outer
    return p


def _gdn_scan_kernel(qf_ref, kf_ref, vf_ref, btf_ref, af_ref, qb_ref, kb_ref, vb_ref, btb_ref, ab_ref,
                     alog_ref, dtb_ref, of_ref, ob_ref, s_sc, *, heads, dk, dv):
    @pl.when(pl.program_id(0) == 0)
    def _():
        s_sc[...] = jnp.zeros_like(s_sc)

    chains = []
    for direction, (q_ref, k_ref, v_ref, bt_ref, a_ref, o_ref) in enumerate(
            ((qf_ref, kf_ref, vf_ref, btf_ref, af_ref, of_ref), (qb_ref, kb_ref, vb_ref, btb_ref, ab_ref, ob_ref))):
        incl, strict = _causal_masks(direction)
        beta = _sigmoid(bt_ref[...])
        glog = -jnp.exp(alog_ref[direction]) * _softplus(a_ref[...] + dtb_ref[direction])
        gcum = _dot_exact(incl.astype(F32), glog)
        gtot = gcum[CHUNK - 1:CHUNK, :] if direction == 0 else gcum[0:1, :]
        e_in = jnp.exp(gcum)
        e_out = jnp.exp(gtot - gcum)
        g_end = jnp.exp(gtot)
        gcum_t = gcum.T
        for h in range(heads):
            q = q_ref[:, h * dk:(h + 1) * dk]
            k = k_ref[:, h * dk:(h + 1) * dk]
            q = q * lax.rsqrt(jnp.sum(q * q, axis=-1, keepdims=True) + L2_EPS) * (dk ** -0.5)
            k = k * lax.rsqrt(jnp.sum(k * k, axis=-1, keepdims=True) + L2_EPS)
            b_col = beta[:, h:h + 1]
            kb = k * b_col
            chains.append(dict(
                q=q, k=k, kb=kb, strict=strict, o_ref=o_ref, h=h, slot=direction * heads + h,
                decay=jnp.exp(jnp.where(incl, gcum[:, h:h + 1] - gcum_t[h:h + 1, :], -jnp.inf)),
                rhs=jnp.concatenate([v_ref[:, h * dv:(h + 1) * dv] * b_col, kb * e_in[:, h:h + 1]], axis=1),
                q_in=q * e_in[:, h:h + 1], k_out=k * e_out[:, h:h + 1], g_end=g_end[:, h:h + 1]))

    a_mats = [jnp.where(c["strict"], _dot_nt(c["kb"], c["k"]) * c["decay"], 0.0) for c in chains]
    qk = [_dot_nt(c["q"], c["k"]) * c["decay"] for c in chains]
    t_inv = _unit_triangular_inverses(a_mats)
    sol = [_dot(t, c["rhs"]) for t, c in zip(t_inv, chains)]
    states = [s_sc[c["slot"]] for c in chains]
    v_new = [x[:, :dv] - _dot(x[:, dv:], st) for x, st in zip(sol, states)]
    o_inter = [_dot(c["q_in"], st) for c, st in zip(chains, states)]
    o_intra = [_dot(x, y) for x, y in zip(qk, v_new)]
    s_upd = [_dot_tn(c["k_out"], y) for c, y in zip(chains, v_new)]
    for c, st, x, y, z in zip(chains, states, o_inter, o_intra, s_upd):
        c["o_ref"][:, c["h"] * dv:(c["h"] + 1) * dv] = x + y
        s_sc[c["slot"]] = st * c["g_end"] + z


def _gdn_scan_call(qkv, p, beta_cb, alog, dtb, heads, dk, dv, nc_ctx):
    m = qkv.shape[0]
    nc = m // CHUNK
    qw, vw = heads * dk, heads * dv
    assert qw == vw
    bidx = functools.partial(_bwd_chunk, nc_ctx=nc_ctx, nc_tot=nc)

    def specs(row):
        return [pl.BlockSpec((CHUNK, qw), lambda s: (row(s), 0)),
                pl.BlockSpec((CHUNK, qw), lambda s: (row(s), 1)),
                pl.BlockSpec((CHUNK, vw), lambda s: (row(s), 2))]

    def gate_specs(row, direction):
        return [pl.BlockSpec((CHUNK, LANES), lambda s: (row(s), beta_cb + 2 * direction)),
                pl.BlockSpec((CHUNK, LANES), lambda s: (row(s), beta_cb + 2 * direction + 1))]

    fwd = lambda s: s
    return pl.pallas_call(
        functools.partial(_gdn_scan_kernel, heads=heads, dk=dk, dv=dv),
        grid=(nc,),
        in_specs=(specs(fwd) + gate_specs(fwd, 0) + specs(bidx) + gate_specs(bidx, 1)
                  + [pl.BlockSpec((2, 1, LANES), lambda s: (0, 0, 0)),
                     pl.BlockSpec((2, 1, LANES), lambda s: (0, 0, 0))]),
        out_specs=[pl.BlockSpec((CHUNK, vw), lambda s: (s, 0)),
                   pl.BlockSpec((CHUNK, vw), lambda s: (bidx(s), 0))],
        out_shape=[jax.ShapeDtypeStruct((m, vw), F32)] * 2,
        scratch_shapes=[pltpu.VMEM((2 * heads, dk, dv), F32)],
        compiler_params=_cparams(1),
        name="gdn_scan",
    )(qkv, qkv, qkv, p, p, qkv, qkv, qkv, p, p, alog, dtb)


def _mlstm_scan_kernel(qf_ref, kf_ref, vf_ref, if_ref, ff_ref, qb_ref, kb_ref, vb_ref, ib_ref, fb_ref,
                       ibias_ref, fbias_ref, of_ref, ob_ref, c_sc, m_sc, *, heads, dqk, dv):
    @pl.when(pl.program_id(0) == 0)
    def _():
        c_sc[...] = jnp.zeros_like(c_sc)
        m_sc[...] = jnp.zeros_like(m_sc)

    ones_col = (lax.broadcasted_iota(jnp.int32, (CHUNK, LANES), 1) == 0).astype(F32)
    chains = []
    for direction, (q_ref, k_ref, v_ref, i_ref, f_ref, o_ref) in enumerate(
            ((qf_ref, kf_ref, vf_ref, if_ref, ff_ref, of_ref), (qb_ref, kb_ref, vb_ref, ib_ref, fb_ref, ob_ref))):
        incl, _ = _causal_masks(direction)
        log_i = i_ref[...] + ibias_ref[direction]
        log_f = -_softplus(-(f_ref[...] + fbias_ref[direction]))
        bcum = _dot_exact(incl.astype(F32), log_f)
        btot = bcum[CHUNK - 1:CHUNK, :] if direction == 0 else bcum[0:1, :]
        log_end = btot - bcum + log_i
        m_st = m_sc[direction]
        m_new = jnp.maximum(btot + m_st, jnp.max(log_end, axis=0, keepdims=True))
        m_sc[direction] = m_new
        carry = jnp.exp(btot + m_st - m_new)
        k_scale = jnp.exp(log_end - m_new)
        b_inter = bcum + m_st
        bcum_t = bcum.T
        log_i_t = log_i.T
        for h in range(heads):
            log_d = jnp.where(incl, bcum[:, h:h + 1] - bcum_t[h:h + 1, :] + log_i_t[h:h + 1, :], -jnp.inf)
            m_t = jnp.maximum(b_inter[:, h:h + 1], jnp.max(log_d, axis=-1, keepdims=True))
            chains.append(dict(
                q=q_ref[:, h * dqk:(h + 1) * dqk] * (dqk ** -0.5), k=k_ref[:, h * dqk:(h + 1) * dqk],
                v_ext=jnp.concatenate([v_ref[:, h * dv:(h + 1) * dv], ones_col], axis=1),
                p_intra=jnp.exp(log_d - m_t), w_inter=jnp.exp(b_inter[:, h:h + 1] - m_t), floor=jnp.exp(-m_t),
                k_scale=k_scale[:, h:h + 1], carry=carry[:, h:h + 1], o_ref=o_ref, h=h, slot=direction * heads + h))

    s = [_dot_nt(c["q"], c["k"]) * c["p_intra"] for c in chains]
    states = [c_sc[c["slot"]] for c in chains]
    inter = [_dot(c["q"], st) for c, st in zip(chains, states)]
    intra = [_dot(x, c["v_ext"]) for x, c in zip(s, chains)]
    upd = [_dot_tn(c["k"] * c["k_scale"], c["v_ext"]) for c in chains]
    for c, st, x, y, z in zip(chains, states, inter, intra, upd):
        out = c["w_inter"] * x + y
        den = jnp.maximum(jnp.abs(out[:, dv:dv + 1]), c["floor"])
        c["o_ref"][:, c["h"] * dv:(c["h"] + 1) * dv] = out[:, :dv] / den
        c_sc[c["slot"]] = c["carry"] * st + z


def _mlstm_scan_call(p, q_cb, gate_cb, ib, fb, heads, dqk, dv, nc_ctx):
    m = p.shape[0]
    nc = m // CHUNK
    qw, vw = heads * dqk, heads * dv
    assert q_cb % qw == 0 and (q_cb + 2 * qw) % vw == 0
    bidx = functools.partial(_bwd_chunk, nc_ctx=nc_ctx, nc_tot=nc)
    qb, kb, vb = q_cb // qw, q_cb // qw + 1, (q_cb + 2 * qw) // vw
    gb = gate_cb // LANES

    def specs(row, direction):
        return [pl.BlockSpec((CHUNK, qw), lambda s: (row(s), qb)),
                pl.BlockSpec((CHUNK, qw), lambda s: (row(s), kb)),
                pl.BlockSpec((CHUNK, vw), lambda s: (row(s), vb)),
                pl.BlockSpec((CHUNK, LANES), lambda s: (row(s), gb + 2 * direction)),
                pl.BlockSpec((CHUNK, LANES), lambda s: (row(s), gb + 2 * direction + 1))]

    return pl.pallas_call(
        functools.partial(_mlstm_scan_kernel, heads=heads, dqk=dqk, dv=dv),
        grid=(nc,),
        in_specs=(specs(lambda s: s, 0) + specs(bidx, 1)
                  + [pl.BlockSpec((2, 1, LANES), lambda s: (0, 0, 0)),
                     pl.BlockSpec((2, 1, LANES), lambda s: (0, 0, 0))]),
        out_specs=[pl.BlockSpec((CHUNK, vw), lambda s: (s, 0)),
                   pl.BlockSpec((CHUNK, vw), lambda s: (bidx(s), 0))],
        out_shape=[jax.ShapeDtypeStruct((m, vw), F32)] * 2,
        scratch_shapes=[pltpu.VMEM((2 * heads, dqk, dv + LANES), F32), pltpu.VMEM((2, 1, LANES), F32)],
        compiler_params=_cparams(1),
        name="mlstm_scan",
    )(p, p, p, p, p, p, p, p, p, p, ib, fb)


def _filter_kernel(feat_ref, w1_ref, b1_ref, w2_ref, b2_ref, w3_ref, freq_ref, dec_ref, o_ref, h_sc):
    @pl.when(pl.program_id(1) == 0)
    def _():
        h = jnp.sin(freq_ref[0:1, :] * (_dot(feat_ref[...], w1_ref[...]) + b1_ref[...]))
        h_sc[...] = jnp.sin(freq_ref[1:2, :] * (_dot(h, w2_ref[...]) + b2_ref[...]))

    o_ref[...] = _dot(h_sc[...], w3_ref[...]) * jnp.exp(-feat_ref[:, 0:1] * jnp.abs(dec_ref[...]))


def _filter_call(feats, w1, b1, w2, b2, w3, freq, dec, tl, tn):
    length, fp = feats.shape
    hid = w2.shape[0]
    n = w3.shape[1]
    return pl.pallas_call(
        _filter_kernel,
        grid=(length // tl, n // tn),
        in_specs=[pl.BlockSpec((tl, fp), lambda i, j: (i, 0)),
                  pl.BlockSpec((fp, hid), lambda i, j: (0, 0)),
                  pl.BlockSpec((1, hid), lambda i, j: (0, 0)),
                  pl.BlockSpec((hid, hid), lambda i, j: (0, 0)),
                  pl.BlockSpec((1, hid), lambda i, j: (0, 0)),
                  pl.BlockSpec((hid, tn), lambda i, j: (0, j)),
                  pl.BlockSpec((2, hid), lambda i, j: (0, 0)),
                  pl.BlockSpec((1, tn), lambda i, j: (0, j))],
        out_specs=pl.BlockSpec((tl, tn), lambda i, j: (i, j)),
        out_shape=jax.ShapeDtypeStruct((length, n), F32),
        scratch_shapes=[pltpu.VMEM((tl, hid), F32)],
        compiler_params=_cparams(2),
        name="hyena_filter",
    )(feats, w1, b1, w2, b2, w3, freq, dec)


def _dft_constants(length):
    n2 = DFT_N2
    n = 2 * length
    n1 = n // n2
    kh = n1 // 2
    na = kh + 1

    def cis(num, den):
        ang = (2.0 * math.pi / den) * (num % den).astype(F32)
        return jnp.cos(ang), jnp.sin(ang)

    ar = lambda size: jnp.arange(size, dtype=jnp.int32)
    c, s = cis(ar(na)[None, :, None] * (n2 * ar(kh)[None, None, :] + ar(n2)[:, None, None]), n)
    m1 = jnp.stack([c, -s], axis=2).reshape(n2, 2 * na, kh)
    c, s = cis(ar(n2)[:, None] * ar(n2)[None, :], n2)
    g2 = jnp.block([[c, s], [-s, c]])
    c, s = cis(ar(n2)[None, :, None] * (ar(na)[:, None, None] + n1 * ar(n2)[None, None, :]), n)
    minv = jnp.concatenate([jnp.concatenate([c, -s], axis=2), jnp.concatenate([s, c], axis=2)], axis=1)
    c, s = cis(ar(kh)[:, None] * ar(na)[None, :], n1)
    weight = jnp.where(jnp.logical_or(ar(na) == 0, ar(na) == kh), 1.0, 2.0)[None, :] / n
    pinv = jnp.stack([c * weight, -s * weight], axis=2).reshape(kh, 2 * na)
    return m1.astype(BF16), g2.astype(BF16), minv.astype(BF16), pinv.astype(BF16)


def _dft1_kernel(x_ref, m_ref, o_ref, *, n_lo_major):
    for j in range(SUBLANES):
        o_ref[:, j, :] = _dot(m_ref[j], x_ref[j] if n_lo_major else x_ref[:, j, :])


def _dft1_call(x3, col_blk0, width, m1, tc, n_lo_major=False):
    n2, rows, kh = m1.shape
    if n_lo_major:
        x_spec = pl.BlockSpec((SUBLANES, kh, tc), lambda c, g: (g, 0, col_blk0 + c))
    else:
        x_spec = pl.BlockSpec((kh, SUBLANES, tc), lambda c, g: (0, g, col_blk0 + c))
    return pl.pallas_call(
        functools.partial(_dft1_kernel, n_lo_major=n_lo_major),
        grid=(width // tc, n2 // SUBLANES),
        in_specs=[x_spec,
                  pl.BlockSpec((SUBLANES, rows, kh), lambda c, g: (g, 0, 0))],
        out_specs=pl.BlockSpec((rows, SUBLANES, tc), lambda c, g: (0, g, c)),
        out_shape=jax.ShapeDtypeStruct((rows, n2, width), F32),
        compiler_params=_cparams(2),
        name="dft_stage1",
    )(x3, m1)


def _filter_spectrum_kernel(af_ref, ab_ref, g_ref, o_ref):
    sf = _dot(g_ref[...], af_ref[...])
    sb = _dot(g_ref[...], ab_ref[...])
    sign = jnp.where(lax.broadcasted_iota(jnp.int32, (2 * DFT_N2, 1), 0) < DFT_N2, 1.0, -1.0)
    o_ref[...] = sf + sign * sb


def _filter_spectrum_call(a, g2, tc):
    rows, width = a.shape
    half = width // 2
    blk = 2 * DFT_N2
    return pl.pallas_call(
        _filter_spectrum_kernel,
        grid=(rows // blk, half // tc),
        in_specs=[pl.BlockSpec((blk, tc), lambda a_, j: (a_, j)),
                  pl.BlockSpec((blk, tc), lambda a_, j: (a_, half // tc + j)),
                  pl.BlockSpec((blk, blk), lambda a_, j: (0, 0))],
        out_specs=pl.BlockSpec((blk, tc), lambda a_, j: (a_, j)),
        out_shape=jax.ShapeDtypeStruct((rows, half), F32),
        compiler_params=_cparams(2),
        name="hyena_filter_spectrum",
    )(a, a, g2)


def _complex_mul(x, kf, n):
    xr, xi = x[:n], x[n:]
    kr, ki = kf[:n], kf[n:]
    return jnp.concatenate([xr * kr - xi * ki, xr * ki + xi * kr], axis=0)


def _spectral_mid_kernel(a_ref, g_ref, kf_ref, minv_ref, o_ref):
    x = _dot(g_ref[...], a_ref[...])
    o_ref[...] = _dot(minv_ref[0], _complex_mul(x, kf_ref[...], DFT_N2))


def _spectral_mid_call(a, g2, kf, kf_cb, minv, tc):
    rows, width = a.shape
    blk = 2 * DFT_N2
    return pl.pallas_call(
        _spectral_mid_kernel,
        grid=(rows // blk, width // tc),
        in_specs=[pl.BlockSpec((blk, tc), lambda a_, j: (a_, j)),
                  pl.BlockSpec((blk, blk), lambda a_, j: (0, 0)),
                  pl.BlockSpec((blk, tc), lambda a_, j: (a_, kf_cb + j)),
                  pl.BlockSpec((1, blk, blk), lambda a_, j: (a_, 0, 0))],
        out_specs=pl.BlockSpec((blk, tc), lambda a_, j: (a_, j)),
        out_shape=jax.ShapeDtypeStruct((rows, width), F32),
        compiler_params=_cparams(2),
        name="hyena_spectral_mid",
    )(a, g2, kf, minv)


def _idft2_kernel(z_ref, p_ref, v_ref, gate_ref, bias_ref, o_ref, *, v_major, out_major):
    for j in range(SUBLANES):
        y = _dot(p_ref[...], z_ref[:, j, :])
        v = v_ref[j] if v_major else v_ref[:, j, :]
        res = gate_ref[:, j, :] * (y + v * bias_ref[...])
        if out_major:
            o_ref[j] = res
        else:
            o_ref[:, j, :] = res


def _idft2_call(z3, pinv, v3, v_cb, v_major, gate3, gate_cb, bias, width, tc, out_major):
    rows, n2, _ = z3.shape
    kh = pinv.shape[0]

    def time_spec(major, cb):
        if major:
            return pl.BlockSpec((SUBLANES, kh, tc), lambda c, g: (g, 0, cb + c))
        return pl.BlockSpec((kh, SUBLANES, tc), lambda c, g: (0, g, cb + c))

    return pl.pallas_call(
        functools.partial(_idft2_kernel, v_major=v_major, out_major=out_major),
        grid=(width // tc, n2 // SUBLANES),
        in_specs=[pl.BlockSpec((rows, SUBLANES, tc), lambda c, g: (0, g, c)),
                  pl.BlockSpec((kh, rows), lambda c, g: (0, 0)),
                  time_spec(v_major, v_cb),
                  time_spec(False, gate_cb),
                  pl.BlockSpec((1, tc), lambda c, g: (0, c))],
        out_specs=time_spec(out_major, 0),
        out_shape=jax.ShapeDtypeStruct((n2, kh, width) if out_major else (kh, n2, width), F32),
        compiler_params=_cparams(2),
        name="idft_stage2_gate",
    )(z3, pinv, v3, gate3, bias)


def _long_conv(v3, v_cb, v_major, gate3, gate_cb, bias, kf, kf_cb, consts, width, tc, out_major):
    m1, g2, minv, pinv = consts
    a = _dft1_call(v3, v_cb, width, m1, tc, n_lo_major=v_major)
    rows = a.shape[0]
    z = _spectral_mid_call(a.reshape(rows * DFT_N2, width), g2, kf, kf_cb, minv, tc)
    return _idft2_call(z.reshape(rows, DFT_N2, width), pinv, v3, v_cb, v_major, gate3, gate_cb, bias, width, tc,
                       out_major)


def _dense_dft_constants(length):
    n = 2 * length
    k = jnp.arange(n, dtype=jnp.int32)[:, None]
    t = jnp.arange(length, dtype=jnp.int32)[None, :]
    ang = (2.0 * math.pi / n) * ((k * t) % n).astype(F32)
    c, s = jnp.cos(ang), jnp.sin(ang)
    fwd = jnp.concatenate([c, -s], axis=0)
    inv = jnp.concatenate([c.T, -s.T], axis=1) / n
    return fwd.astype(BF16), inv.astype(BF16)


def _dense_spectrum_kernel(tf_ref, tb_ref, f_ref, o_ref, *, n):
    sign = jnp.where(lax.broadcasted_iota(jnp.int32, (2 * n, 1), 0) < n, 1.0, -1.0)
    o_ref[...] = _dot(f_ref[...], tf_ref[...]) + sign * _dot(f_ref[...], tb_ref[...])


def _dense_spectrum_call(taps, fwd, tc):
    length, width = taps.shape
    half = width // 2
    rows = fwd.shape[0]
    return pl.pallas_call(
        functools.partial(_dense_spectrum_kernel, n=rows // 2),
        grid=(half // tc,),
        in_specs=[pl.BlockSpec((length, tc), lambda j: (0, j)),
                  pl.BlockSpec((length, tc), lambda j: (0, half // tc + j)),
                  pl.BlockSpec((rows, length), lambda j: (0, 0))],
        out_specs=pl.BlockSpec((rows, tc), lambda j: (0, j)),
        out_shape=jax.ShapeDtypeStruct((rows, half), F32),
        compiler_params=_cparams(1),
        name="hyena_filter_spectrum_dense",
    )(taps, taps, fwd)


def _dense_hyena_kernel(v_ref, x1_ref, x2_ref, kf0_ref, kf1_ref, f_ref, inv_ref, b0_ref, b1_ref, o_ref, *, n):
    def conv(u, kf_ref, b_ref):
        y = _dot(inv_ref[...], _complex_mul(_dot(f_ref[...], u), kf_ref[...], n))
        return y + u * b_ref[...]

    z = x1_ref[...] * conv(v_ref[...], kf0_ref, b0_ref)
    o_ref[...] = x2_ref[...] * conv(z, kf1_ref, b1_ref)


def _dense_hyena_call(hc, length, width, kf, fwd, inv, bias0, bias1, tc):
    rows = fwd.shape[0]
    per = width // tc
    return pl.pallas_call(
        functools.partial(_dense_hyena_kernel, n=rows // 2),
        grid=(per,),
        in_specs=[pl.BlockSpec((length, tc), lambda j: (0, j)),
                  pl.BlockSpec((length, tc), lambda j: (0, per + j)),
                  pl.BlockSpec((length, tc), lambda j: (0, 2 * per + j)),
                  pl.BlockSpec((rows, tc), lambda j: (0, j)),
                  pl.BlockSpec((rows, tc), lambda j: (0, per + j)),
                  pl.BlockSpec((rows, length), lambda j: (0, 0)),
                  pl.BlockSpec((length, rows), lambda j: (0, 0)),
                  pl.BlockSpec((1, tc), lambda j: (0, j)),
                  pl.BlockSpec((1, tc), lambda j: (0, j))],
        out_specs=pl.BlockSpec((length, tc), lambda j: (0, j)),
        out_shape=jax.ShapeDtypeStruct((length, width), F32),
        compiler_params=_cparams(1),
        name="hyena_dense",
    )(hc, hc, hc, kf, kf, fwd, inv, bias0, bias1)


def _filter_features(length):
    t = jnp.linspace(0.0, 1.0, length, dtype=F32)[:, None]
    w = (2.0 * math.pi / length) * jnp.arange(length, dtype=F32)[:, None]
    bands = jnp.linspace(1e-4, HYENA_BANDS - 1, HYENA_BANDS, dtype=F32)[None, :]
    feats = jnp.concatenate([t, jnp.cos(bands * w), -jnp.sin(bands * w)], axis=-1)
    return jnp.pad(feats, ((0, 0), (0, LANES - feats.shape[1])))


def _head_rms(x, heads, width):
    outs = []
    for h in range(heads):
        xh = x[:, h * width:(h + 1) * width]
        outs.append(xh * lax.rsqrt(jnp.mean(xh * xh, axis=-1, keepdims=True) + RMS_EPS))
    return jnp.concatenate(outs, axis=1)


def _merge_kernel(ogf_ref, ogb_ref, omf_ref, omb_ref, z_ref, o_ref, hy_ref, g0_ref, g1_ref, g2_ref, gn_ref, mn_ref,
                  wa_ref, wb_ref, wc_ref, out_ref, *, gdn_heads, gdn_dv, ml_heads, ml_dv):
    a = _head_rms(ogf_ref[...] + ogb_ref[...], gdn_heads, gdn_dv) * gn_ref[...] * _silu(z_ref[...])
    b = _head_rms(omf_ref[...] + omb_ref[...], ml_heads, ml_dv) * mn_ref[...] * _sigmoid(o_ref[...])
    out_ref[...] = (_sigmoid(g0_ref[...]) * _dot(a, wa_ref[...]) + _sigmoid(g1_ref[...]) * _dot(b, wb_ref[...])
                    + _sigmoid(g2_ref[...]) * _dot(hy_ref[...], wc_ref[...]))


def _merge_call(og, om, p, z_cb, o_cb, hy, gate_cb, gn, mn, wa, wb, wc, layer, heads, tm):
    m, d = p.shape[0], wa.shape[-1]
    gw, mw, hw = og[0].shape[-1], om[0].shape[-1], hy.shape[-1]
    gdn_heads, gdn_dv, ml_heads, ml_dv = heads
    zb, ob, gb = z_cb // gw, o_cb // mw, gate_cb // d
    return pl.pallas_call(
        functools.partial(_merge_kernel, gdn_heads=gdn_heads, gdn_dv=gdn_dv, ml_heads=ml_heads, ml_dv=ml_dv),
        grid=(m // tm,),
        in_specs=[pl.BlockSpec((tm, gw), lambda i: (i, 0)),
                  pl.BlockSpec((tm, gw), lambda i: (i, 0)),
                  pl.BlockSpec((tm, mw), lambda i: (i, 0)),
                  pl.BlockSpec((tm, mw), lambda i: (i, 0)),
                  pl.BlockSpec((tm, gw), lambda i: (i, zb)),
                  pl.BlockSpec((tm, mw), lambda i: (i, ob)),
                  pl.BlockSpec((tm, hw), lambda i: (i, 0)),
                  pl.BlockSpec((tm, d), lambda i: (i, gb)),
                  pl.BlockSpec((tm, d), lambda i: (i, gb + 1)),
                  pl.BlockSpec((tm, d), lambda i: (i, gb + 2)),
                  pl.BlockSpec((1, gw), lambda i: (0, 0)),
                  pl.BlockSpec((1, mw), lambda i: (0, 0)),
                  pl.BlockSpec((None, gw, d), lambda i: (layer, 0, 0)),
                  pl.BlockSpec((None, mw, d), lambda i: (layer, 0, 0)),
                  pl.BlockSpec((None, hw, d), lambda i: (layer, 0, 0))],
        out_specs=pl.BlockSpec((tm, d), lambda i: (i, 0)),
        out_shape=jax.ShapeDtypeStruct((m, d), F32),
        compiler_params=_cparams(1),
        name="branch_merge",
    )(og[0], og[1], om[0], om[1], p, p, hy, p, p, p, gn, mn, wa, wb, wc)


def _outproj_kernel(y_ref, x_ref, m_ref, w_ref, o_ref, *, tm, ctx_len):
    gate = jnp.where(_row_is_ctx(pl.program_id(0) * tm, tm, ctx_len), m_ref[0:1, :], m_ref[1:2, :])
    o_ref[...] = x_ref[...] + gate * _dot(y_ref[...], w_ref[...])


def _outproj_call(y, x, mods, w, layer, ctx_len, tm):
    m, d = x.shape
    return pl.pallas_call(
        functools.partial(_outproj_kernel, tm=tm, ctx_len=ctx_len),
        grid=(m // tm,),
        in_specs=[pl.BlockSpec((tm, d), lambda i: (i, 0)),
                  pl.BlockSpec((tm, d), lambda i: (i, 0)),
                  pl.BlockSpec((SUBLANES, d), lambda i: (0, 0)),
                  pl.BlockSpec((None, d, d), lambda i: (layer, 0, 0))],
        out_specs=pl.BlockSpec((tm, d), lambda i: (i, 0)),
        out_shape=jax.ShapeDtypeStruct((m, d), F32),
        compiler_params=_cparams(1),
        name="mixer_out_proj",
    )(y, x, mods, w)


def _final_norm_kernel(x_ref, g_ref, o_ref):
    x = x_ref[...]
    o_ref[...] = x * lax.rsqrt(jnp.mean(x * x, axis=-1, keepdims=True) + RMS_EPS) * g_ref[...]


def _final_norm_call(x, g, row0, rows, tm):
    d = x.shape[1]
    return pl.pallas_call(
        _final_norm_kernel,
        grid=(rows // tm,),
        in_specs=[pl.BlockSpec((tm, d), lambda i: (row0 // tm + i, 0)),
                  pl.BlockSpec((1, d), lambda i: (0, 0))],
        out_specs=pl.BlockSpec((tm, d), lambda i: (i, 0)),
        out_shape=jax.ShapeDtypeStruct((rows, d), F32),
        compiler_params=_cparams(1),
        name="final_rms_norm",
    )(x, g)


def _pack_rows(rows, d):
    out = jnp.concatenate([r.reshape(1, d) for r in rows], axis=0)
    return jnp.pad(out, ((0, SUBLANES - out.shape[0]), (0, 0)))


def _lane_row(vals, lanes=LANES):
    return jnp.pad(vals.astype(F32), ((0, 0), (0, lanes - vals.shape[1])))[:, None, :]


def _pick_tile(total, candidates):
    for c in candidates:
        if total % c == 0:
            return c
    raise ValueError(f"no tile for {total} among {candidates}")


def kernel(x, c, ctx, c_ctx, w_ada, b_ada, norm_g, ffn_w_gate, ffn_w_up, ffn_w_down, w_in, gdn_conv, gdn_a_log,
           gdn_dt_bias, gdn_norm, mlstm_i_bias, mlstm_f_bias, mlstm_norm, hy_short_w, hy_short_b, hf_w1, hf_b1,
           hf_w2, hf_b2, hf_w3, hf_freq, hf_decay, hy_bias, w_a_out, w_b_out, w_c_out, w_out, final_norm):
    batch, seq, d = x.shape
    assert batch == 1 and c.shape[0] == 1 and ctx.shape[0] == 1
    ctx_len = ctx.shape[1]
    depth = w_ada.shape[0]
    m = ctx_len + seq
    assert ctx_len % CHUNK == 0 and seq % CHUNK == 0

    gdn_heads = gdn_a_log.shape[-1]
    gdn_vw = w_a_out.shape[1]
    gdn_qkw = (gdn_conv.shape[1] - gdn_vw) // 2
    gdn_dk, gdn_dv = gdn_qkw // gdn_heads, gdn_vw // gdn_heads
    ml_heads = mlstm_i_bias.shape[-1]
    ml_vw = w_b_out.shape[1]
    hy_w = w_c_out.shape[1]
    n_in = w_in.shape[-1]
    ml_qkw = (n_in - (2 * gdn_qkw + gdn_vw) - 4 * gdn_heads - 4 * ml_heads - ml_vw - gdn_vw - ml_vw
              - 3 * hy_w - 3 * d) // 2
    ml_dqk, ml_dv = ml_qkw // ml_heads, ml_vw // ml_heads
    assert gdn_dk == LANES and gdn_dv == LANES and ml_dqk == LANES and ml_dv % LANES == 0

    o_gqkv = 0
    o_gbeta = o_gqkv + 2 * gdn_qkw + gdn_vw
    o_ga = o_gbeta + 2 * gdn_heads
    o_mqkv = o_ga + 2 * gdn_heads
    o_mi = o_mqkv + 2 * ml_qkw + ml_vw
    o_mf = o_mi + 2 * ml_heads
    o_z = o_mf + 2 * ml_heads
    o_o = o_z + gdn_vw
    o_hy = o_o + ml_vw
    o_gates = o_hy + 3 * hy_w
    assert o_gates + 3 * d == n_in

    def small(off, heads):
        return [jnp.pad(w_in[:, :, off + dd * heads:off + (dd + 1) * heads], ((0, 0), (0, 0), (0, LANES - heads)))
                for dd in range(2)]

    gb, ga = small(o_gbeta, gdn_heads), small(o_ga, gdn_heads)
    mi, mf = small(o_mi, ml_heads), small(o_mf, ml_heads)
    groups = [w_in[:, :, o_gqkv:o_gbeta], w_in[:, :, o_mqkv:o_mi], w_in[:, :, o_z:o_o], w_in[:, :, o_o:o_hy],
              w_in[:, :, o_hy:o_gates], w_in[:, :, o_gates:],
              gb[0], ga[0], gb[1], ga[1], mi[0], mf[0], mi[1], mf[1]]
    w_in_p = jnp.concatenate(groups, axis=-1).astype(BF16)
    c_gqkv = 0
    c_mqkv = c_gqkv + 2 * gdn_qkw + gdn_vw
    c_z = c_mqkv + 2 * ml_qkw + ml_vw
    c_o = c_z + gdn_vw
    c_hy = c_o + ml_vw
    c_gates = c_hy + 3 * hy_w
    c_gsmall = c_gates + 3 * d
    c_msmall = c_gsmall + 4 * LANES
    n_p = c_msmall + 4 * LANES
    assert w_in_p.shape[-1] == n_p

    f_hidden = ffn_w_gate.shape[-1]
    tf = 512
    f_pad = -(-f_hidden // tf) * tf
    wg = jnp.pad(ffn_w_gate, ((0, 0), (0, 0), (0, 0), (0, f_pad - f_hidden))).astype(BF16)
    wu = jnp.pad(ffn_w_up, ((0, 0), (0, 0), (0, 0), (0, f_pad - f_hidden))).astype(BF16)
    wd = jnp.pad(ffn_w_down, ((0, 0), (0, 0), (0, f_pad - f_hidden), (0, 0))).astype(BF16)
    wa, wb, wc, wo = (t.astype(BF16) for t in (w_a_out, w_b_out, w_c_out, w_out))

    tm_big = _pick_tile(m, (768, 512, 384, 256, 128, 64))
    tm_mid = _pick_tile(m, (256, 128, 64))
    tm_small = _pick_tile(m, (128, 64))
    tm_seg = _pick_tile(math.gcd(ctx_len, seq), (256, 128, 64))
    tn_in = _pick_tile(n_p, (1024, 512, 256, 128))
    nc_ctx = ctx_len // CHUNK

    cond = jnp.pad(jnp.concatenate([c_ctx[None, :], c], axis=0), ((0, SUBLANES - 2), (0, 0)))
    mods = _modulation_call(cond, w_ada, b_ada)

    dft_consts = _dft_constants(seq)
    dense_fwd, dense_inv = _dense_dft_constants(ctx_len)
    feats_lat = _filter_features(seq)
    feats_lat = feats_lat.reshape(seq // DFT_N2, DFT_N2, LANES).swapaxes(0, 1).reshape(seq, LANES)
    feats_ctx = _filter_features(ctx_len)
    tl = _pick_tile(seq, (512, 256, 128, 64))

    s = jnp.concatenate([ctx[0], x[0]], axis=0)
    for l in range(depth):
        last = l == depth - 1

        def mod(idx):
            return [mods[l, 0, idx * d:(idx + 1) * d], mods[l, 1, idx * d:(idx + 1) * d]]

        s = _ffn_call(s, _pack_rows(mod(0) + mod(1) + mod(2), d), norm_g[l, 0][None, :], wg, wu, wd, l, 0,
                      ctx_len, tm_big, tf)
        p = _inproj_call(s, _pack_rows(mod(3) + mod(4), d), norm_g[l, 1][None, :], w_in_p[l], ctx_len, tm_big, tn_in)

        gqkv = _dwconv_call(p, c_gqkv, 2 * gdn_qkw + gdn_vw, jnp.transpose(gdn_conv[l]),
                            jnp.zeros((1, 2 * gdn_qkw + gdn_vw), F32), ctx_len, tm_seg, 1024, True)
        hc = _dwconv_call(p, c_hy, 3 * hy_w, jnp.transpose(hy_short_w[l]), hy_short_b[l][None, :],
                          ctx_len, tm_seg, 1024, False)

        og = _gdn_scan_call(gqkv, p, c_gsmall // LANES, _lane_row(gdn_a_log[l]), _lane_row(gdn_dt_bias[l]),
                            gdn_heads, gdn_dk, gdn_dv, nc_ctx)
        om = _mlstm_scan_call(p, c_mqkv, c_msmall, _lane_row(mlstm_i_bias[l]), _lane_row(mlstm_f_bias[l]),
                              ml_heads, ml_dqk, ml_dv, nc_ctx)

        fargs = (jnp.pad(hf_w1[l], ((0, LANES - hf_w1.shape[1]), (0, 0))), hf_b1[l][None, :], hf_w2[l],
                 hf_b2[l][None, :], hf_w3[l], hf_freq[l], jnp.tile(hf_decay[l], 2)[None, :])
        taps = _filter_call(feats_lat, *fargs, tl, 1024)
        m1, g2, minv, pinv = dft_consts
        kh = seq // DFT_N2
        ta = _dft1_call(taps.reshape(DFT_N2, kh, taps.shape[1]), 0, taps.shape[1], m1, 1024, n_lo_major=True)
        kf = _filter_spectrum_call(ta.reshape(ta.shape[0] * DFT_N2, taps.shape[1]), g2, 1024)
        hl3 = hc[ctx_len:].reshape(kh, DFT_N2, 3 * hy_w)
        tc_h = min(hy_w, 1024)
        per = hy_w // tc_h
        z1 = _long_conv(hl3, 0, False, hl3, per, hy_bias[l, 0][None, :], kf, 0, dft_consts, hy_w, tc_h, True)
        z2 = _long_conv(z1, 0, True, hl3, 2 * per, hy_bias[l, 1][None, :], kf, per, dft_consts, hy_w, tc_h, False)
        z2 = z2.reshape(seq, hy_w)
        if not last:
            taps_c = _filter_call(feats_ctx, *fargs, _pick_tile(ctx_len, (256, 128, 64)), 1024)
            kf_c = _dense_spectrum_call(taps_c, dense_fwd, 512)
            zc = _dense_hyena_call(hc[:ctx_len], ctx_len, hy_w, kf_c, dense_fwd, dense_inv, hy_bias[l, 0][None, :],
                                   hy_bias[l, 1][None, :], 256)
        else:
            zc = jnp.zeros((ctx_len, hy_w), F32)
        hy = jnp.concatenate([zc, z2], axis=0)

        merged = _merge_call(og, om, p, c_z, c_o, hy, c_gates, jnp.tile(gdn_norm[l], gdn_heads)[None, :],
                             mlstm_norm[l][None, :], wa, wb, wc, l, (gdn_heads, gdn_dv, ml_heads, ml_dv), tm_small)
        s = _outproj_call(merged, s, _pack_rows(mod(5), d), wo, l, ctx_len, tm_mid)
        s = _ffn_call(s, _pack_rows(mod(6) + mod(7) + mod(8), d), norm_g[l, 2][None, :], wg, wu, wd, l, 1,
                      ctx_len, tm_big, tf)

    tm_fin = _pick_tile(math.gcd(ctx_len, seq), (256, 128, 64))
    out = _final_norm_call(s, final_norm[None, :], ctx_len, seq, tm_fin)
    return out[None]
```

```python
import functools
import math

import jax
import jax.numpy as jnp
import numpy as np
from jax import lax
from jax.experimental import pallas as pl
from jax.experimental.pallas import tpu as pltpu

F32 = jnp.float32
BF16 = jnp.bfloat16

N_MOD = 9
RMS_EPS = 1e-6
L2_EPS = 1e-6
CHUNK = 64
GDN_CONV = 5
HYENA_SHORT = 3
HYENA_BANDS = 16
LANES = 128
SUBLANES = 8
DFT_N2 = 128
VMEM_LIMIT = 56 * 1024 * 1024


def _cparams(n_axes):
    return pltpu.CompilerParams(dimension_semantics=("arbitrary",) * n_axes,
                                vmem_limit_bytes=VMEM_LIMIT)


def _sigmoid(x):
    return jax.nn.sigmoid(x)


def _silu(x):
    return x * jax.nn.sigmoid(x)


def _softplus(x):
    return jnp.maximum(x, 0.0) + jnp.log(1.0 + jnp.exp(-jnp.abs(x)))


def _dot(a, b):
    return jnp.dot(a.astype(BF16), b.astype(BF16), preferred_element_type=F32)


def _dot_nt(a, b):
    return lax.dot_general(a.astype(BF16), b.astype(BF16), (((1,), (1,)), ((), ())),
                           preferred_element_type=F32)


def _dot_tn(a, b):
    return lax.dot_general(a.astype(BF16), b.astype(BF16), (((0,), (0,)), ((), ())),
                           preferred_element_type=F32)


def _dot_exact(a, b):
    return jnp.dot(a, b, preferred_element_type=F32, precision=lax.Precision.HIGHEST)


def _row_is_ctx(row0, rows, lat_len):
    r = row0 + lax.broadcasted_iota(jnp.int32, (rows, 1), 0)
    return r >= lat_len


def _modnorm(x, g, shift, scale):
    y = x * lax.rsqrt(jnp.mean(x * x, axis=-1, keepdims=True) + RMS_EPS) * g
    return y * (1.0 + scale) + shift


ROW_CHUNK = 128


def _for_row_chunks(tm, fn):
    rc = math.gcd(tm, ROW_CHUNK)

    def body(c, carry):
        fn(pl.multiple_of(c * rc, rc), rc)
        return carry

    lax.fori_loop(0, tm // rc, body, 0)


def _store_modulated(x_ref, m_ref, g_ref, h_sc, row0, tm, lat_len):
    def chunk(r0, rc):
        is_ctx = _row_is_ctx(row0 + r0, rc, lat_len)
        shift = jnp.where(is_ctx, m_ref[0:1, :], m_ref[1:2, :])
        scale = jnp.where(is_ctx, m_ref[2:3, :], m_ref[3:4, :])
        h_sc[pl.ds(r0, rc), :] = _modnorm(x_ref[pl.ds(r0, rc), :], g_ref[...], shift, scale).astype(h_sc.dtype)

    _for_row_chunks(tm, chunk)


def _mod_kernel(c_ref, w_ref, b_ref, o_ref):
    o_ref[0] = _dot(_silu(c_ref[...]), w_ref[0]) + b_ref[0]


def _modulation_call(cond, w_ada, b_ada):
    depth, d, n = w_ada.shape
    tn = _pick_tile(n, (1024, 512, 256, 128))
    return pl.pallas_call(
        _mod_kernel,
        grid=(depth, n // tn),
        in_specs=[pl.BlockSpec((SUBLANES, d), lambda l, j: (0, 0)),
                  pl.BlockSpec((1, d, tn), lambda l, j: (l, 0, j)),
                  pl.BlockSpec((1, 1, tn), lambda l, j: (l, 0, j))],
        out_specs=pl.BlockSpec((1, SUBLANES, tn), lambda l, j: (l, 0, j)),
        out_shape=jax.ShapeDtypeStruct((depth, SUBLANES, n), F32),
        compiler_params=_cparams(2),
        name="adaln_modulation",
    )(cond, w_ada, b_ada.reshape(depth, 1, n))


def _ffn_kernel(x_ref, m_ref, g_ref, wg_ref, wu_ref, wd_ref, o_ref, h_sc, acc_sc, *, tm, lat_len):
    row0 = pl.program_id(0) * tm
    f = pl.program_id(1)

    @pl.when(f == 0)
    def _():
        _store_modulated(x_ref, m_ref, g_ref, h_sc, row0, tm, lat_len)
        acc_sc[...] = jnp.zeros_like(acc_sc)

    h = h_sc[...]
    a = _silu(_dot(h, wg_ref[...])) * _dot(h, wu_ref[...])
    acc_sc[...] += _dot(a, wd_ref[...])

    @pl.when(f == pl.num_programs(1) - 1)
    def _():
        def chunk(r0, rc):
            gate = jnp.where(_row_is_ctx(row0 + r0, rc, lat_len), m_ref[4:5, :], m_ref[5:6, :])
            rows = pl.ds(r0, rc)
            o_ref[rows, :] = x_ref[rows, :] + 0.5 * gate * acc_sc[rows, :]

        _for_row_chunks(tm, chunk)


def _ffn_call(x, mods, g, wg, wu, wd, layer, which, lat_len, tm, tf):
    m, d = x.shape
    fp = wg.shape[-1]
    return pl.pallas_call(
        functools.partial(_ffn_kernel, tm=tm, lat_len=lat_len),
        grid=(m // tm, fp // tf),
        in_specs=[pl.BlockSpec((tm, d), lambda i, f: (i, 0)),
                  pl.BlockSpec((SUBLANES, d), lambda i, f: (0, 0)),
                  pl.BlockSpec((1, d), lambda i, f: (0, 0)),
                  pl.BlockSpec((None, None, d, tf), lambda i, f: (layer, which, 0, f)),
                  pl.BlockSpec((None, None, d, tf), lambda i, f: (layer, which, 0, f)),
                  pl.BlockSpec((None, None, tf, d), lambda i, f: (layer, which, f, 0))],
        out_specs=pl.BlockSpec((tm, d), lambda i, f: (i, 0)),
        out_shape=jax.ShapeDtypeStruct((m, d), F32),
        scratch_shapes=[pltpu.VMEM((tm, d), BF16), pltpu.VMEM((tm, d), F32)],
        compiler_params=_cparams(2),
        name="macaron_swiglu",
    )(x, mods, g, wg, wu, wd)


def _inproj_kernel(x_ref, m_ref, g_ref, w_ref, o_ref, h_sc, *, tm, lat_len):
    @pl.when(pl.program_id(1) == 0)
    def _():
        _store_modulated(x_ref, m_ref, g_ref, h_sc, pl.program_id(0) * tm, tm, lat_len)

    o_ref[...] = _dot(h_sc[...], w_ref[...])


def _inproj_call(x, mods, g, w, lat_len, tm, tn):
    m, d = x.shape
    n = w.shape[1]
    return pl.pallas_call(
        functools.partial(_inproj_kernel, tm=tm, lat_len=lat_len),
        grid=(m // tm, n // tn),
        in_specs=[pl.BlockSpec((tm, d), lambda i, j: (i, 0)),
                  pl.BlockSpec((SUBLANES, d), lambda i, j: (0, 0)),
                  pl.BlockSpec((1, d), lambda i, j: (0, 0)),
                  pl.BlockSpec((d, tn), lambda i, j: (0, j))],
        out_specs=pl.BlockSpec((tm, tn), lambda i, j: (i, j)),
        out_shape=jax.ShapeDtypeStruct((m, n), F32),
        scratch_shapes=[pltpu.VMEM((tm, d), BF16)],
        compiler_params=_cparams(2),
        name="mixer_in_proj",
    )(x, mods, g, w)


def _dwconv_kernel(prev_ref, cur_ref, next_ref, w_ref, b_ref, o_ref, ext_sc, *, tm, taps, split, m_rows, act):
    row0 = pl.program_id(0) * tm
    at_start = jnp.logical_or(row0 == 0, row0 == split)
    at_end = jnp.logical_or(row0 + tm == split, row0 + tm == m_rows)
    ext_sc[0:SUBLANES, :] = jnp.where(at_start, 0.0, prev_ref[...])
    ext_sc[SUBLANES:SUBLANES + tm, :] = cur_ref[...]
    ext_sc[SUBLANES + tm:2 * SUBLANES + tm, :] = jnp.where(at_end, 0.0, next_ref[...])
    acc = jnp.zeros(o_ref.shape, F32) + b_ref[...]
    for t in range(taps):
        acc = acc + ext_sc[pl.ds(SUBLANES + t - taps // 2, tm), :] * w_ref[t:t + 1, :]
    o_ref[...] = _silu(acc) if act else acc


def _dwconv_call(p, col0, ncols, w_t, b, split, tm, tc, act):
    m = p.shape[0]
    taps = w_t.shape[0]
    assert split % tm == 0 and m % tm == 0 and col0 % tc == 0 and taps // 2 <= SUBLANES
    cb0 = col0 // tc
    rb = tm // SUBLANES
    last = m // SUBLANES - 1
    return pl.pallas_call(
        functools.partial(_dwconv_kernel, tm=tm, taps=taps, split=split, m_rows=m, act=act),
        grid=(m // tm, ncols // tc),
        in_specs=[pl.BlockSpec((SUBLANES, tc), lambda i, j: (jnp.maximum(i * rb - 1, 0), cb0 + j)),
                  pl.BlockSpec((tm, tc), lambda i, j: (i, cb0 + j)),
                  pl.BlockSpec((SUBLANES, tc), lambda i, j: (jnp.minimum((i + 1) * rb, last), cb0 + j)),
                  pl.BlockSpec((taps, tc), lambda i, j: (0, j)),
                  pl.BlockSpec((1, tc), lambda i, j: (0, j))],
        out_specs=pl.BlockSpec((tm, tc), lambda i, j: (i, j)),
        out_shape=jax.ShapeDtypeStruct((m, ncols), F32),
        scratch_shapes=[pltpu.VMEM((tm + 2 * SUBLANES, tc), F32)],
        compiler_params=_cparams(2),
        name="depthwise_conv",
    )(p, p, p, w_t, b)


def _fwd_chunk(s, nc_ctx, nc_tot):
    return jnp.where(s < nc_ctx, nc_tot - nc_ctx + s, s - nc_ctx)


def _bwd_chunk(s, nc_ctx, nc_tot):
    del nc_ctx
    return nc_tot - 1 - s


def _causal_masks(direction):
    ii = lax.broadcasted_iota(jnp.int32, (CHUNK, CHUNK), 0)
    jj = lax.broadcasted_iota(jnp.int32, (CHUNK, CHUNK), 1)
    rel = ii - jj if direction == 0 else jj - ii
    return rel >= 0, rel > 0


def _unit_triangular_inverses(mats):
    ii = lax.broadcasted_iota(jnp.int32, (CHUNK, CHUNK), 0)
    jj = lax.broadcasted_iota(jnp.int32, (CHUNK, CHUNK), 1)
    eye = (ii == jj).astype(F32)

    def same_block(width):
        return (ii // width) == (jj // width)

    inner = same_block(8)
    d1 = [jnp.where(inner, a, 0.0) for a in mats]
    d2 = [_dot(x, x) for x in d1]
    d4 = [_dot(x, x) for x in d2]
    p = [_dot(eye - x, eye + y) for x, y in zip(d1, d2)]
    p = [_dot(x, eye + y) for x, y in zip(p, d4)]
    for width in (16, 32, 64):
        outer = same_block(width)
        ring = jnp.logical_and(outer, jnp.logical_not(inner))
        t = [_dot(x, jnp.where(ring, a, 0.0)) for x, a in zip(p, mats)]
        p = [x - _dot(y, x) for x, y in zip(p, t)]
        inner = outer
    return p


def _gdn_scan_kernel(qf_ref, kf_ref, vf_ref, btf_ref, af_ref, qb_ref, kb_ref, vb_ref, btb_ref, ab_ref,
                     alog_ref, dtb_ref, of_ref, ob_ref, s_sc, *, heads, dk, dv):
    @pl.when(pl.program_id(0) == 0)
    def _():
        s_sc[...] = jnp.zeros_like(s_sc)

    chains = []
    for direction, (q_ref, k_ref, v_ref, bt_ref, a_ref, o_ref) in enumerate(
            ((qf_ref, kf_ref, vf_ref, btf_ref, af_ref, of_ref), (qb_ref, kb_ref, vb_ref, btb_ref, ab_ref, ob_ref))):
        incl, strict = _causal_masks(direction)
        beta = _sigmoid(bt_ref[...])
        glog = -jnp.exp(alog_ref[direction]) * _softplus(a_ref[...] + dtb_ref[direction])
        gcum = _dot_exact(incl.astype(F32), glog)
        gtot = gcum[CHUNK - 1:CHUNK, :] if direction == 0 else gcum[0:1, :]
        e_in = jnp.exp(gcum)
        e_out = jnp.exp(gtot - gcum)
        g_end = jnp.exp(gtot)
        gcum_t = gcum.T
        for h in range(heads):
            q = q_ref[:, h * dk:(h + 1) * dk]
            k = k_ref[:, h * dk:(h + 1) * dk]
            q = q * lax.rsqrt(jnp.sum(q * q, axis=-1, keepdims=True) + L2_EPS) * (dk ** -0.5)
            k = k * lax.rsqrt(jnp.sum(k * k, axis=-1, keepdims=True) + L2_EPS)
            b_col = beta[:, h:h + 1]
            kb = k * b_col
            chains.append(dict(
                q=q, k=k, kb=kb, strict=strict, o_ref=o_ref, h=h, slot=direction * heads + h,
                decay=jnp.exp(jnp.where(incl, gcum[:, h:h + 1] - gcum_t[h:h + 1, :], -jnp.inf)),
                rhs=jnp.concatenate([v_ref[:, h * dv:(h + 1) * dv] * b_col, kb * e_in[:, h:h + 1]], axis=1),
                q_in=q * e_in[:, h:h + 1], k_out=k * e_out[:, h:h + 1], g_end=g_end[:, h:h + 1]))

    a_mats = [jnp.where(c["strict"], _dot_nt(c["kb"], c["k"]) * c["decay"], 0.0) for c in chains]
    qk = [_dot_nt(c["q"], c["k"]) * c["decay"] for c in chains]
    t_inv = _unit_triangular_inverses(a_mats)
    sol = [_dot(t, c["rhs"]) for t, c in zip(t_inv, chains)]
    states = [s_sc[c["slot"]] for c in chains]
    v_new = [x[:, :dv] - _dot(x[:, dv:], st) for x, st in zip(sol, states)]
    o_inter = [_dot(c["q_in"], st) for c, st in zip(chains, states)]
    o_intra = [_dot(x, y) for x, y in zip(qk, v_new)]
    s_upd = [_dot_tn(c["k_out"], y) for c, y in zip(chains, v_new)]
    for c, st, x, y, z in zip(chains, states, o_inter, o_intra, s_upd):
        c["o_ref"][:, c["h"] * dv:(c["h"] + 1) * dv] = x + y
        s_sc[c["slot"]] = st * c["g_end"] + z


def _gdn_scan_call(qkv, p, beta_cb, alog, dtb, heads, dk, dv, nc_ctx):
    m = qkv.shape[0]
    nc = m // CHUNK
    qw, vw = heads * dk, heads * dv
    assert qw == vw
    bidx = functools.partial(_bwd_chunk, nc_ctx=nc_ctx, nc_tot=nc)

    def specs(row):
        return [pl.BlockSpec((CHUNK, qw), lambda s: (row(s), 0)),
                pl.BlockSpec((CHUNK, qw), lambda s: (row(s), 1)),
                pl.BlockSpec((CHUNK, vw), lambda s: (row(s), 2))]

    def gate_specs(row, direction):
        return [pl.BlockSpec((CHUNK, LANES), lambda s: (row(s), beta_cb + 2 * direction)),
                pl.BlockSpec((CHUNK, LANES), lambda s: (row(s), beta_cb + 2 * direction + 1))]

    fwd = functools.partial(_fwd_chunk, nc_ctx=nc_ctx, nc_tot=nc)
    return pl.pallas_call(
        functools.partial(_gdn_scan_kernel, heads=heads, dk=dk, dv=dv),
        grid=(nc,),
        in_specs=(specs(fwd) + gate_specs(fwd, 0) + specs(bidx) + gate_specs(bidx, 1)
                  + [pl.BlockSpec((2, 1, LANES), lambda s: (0, 0, 0)),
                     pl.BlockSpec((2, 1, LANES), lambda s: (0, 0, 0))]),
        out_specs=[pl.BlockSpec((CHUNK, vw), lambda s: (fwd(s), 0)),
                   pl.BlockSpec((CHUNK, vw), lambda s: (bidx(s), 0))],
        out_shape=[jax.ShapeDtypeStruct((m, vw), F32)] * 2,
        scratch_shapes=[pltpu.VMEM((2 * heads, dk, dv), F32)],
        compiler_params=_cparams(1),
        name="gdn_scan",
    )(qkv, qkv, qkv, p, p, qkv, qkv, qkv, p, p, alog, dtb)


def _mlstm_scan_kernel(qf_ref, kf_ref, vf_ref, if_ref, ff_ref, qb_ref, kb_ref, vb_ref, ib_ref, fb_ref,
                       ibias_ref, fbias_ref, of_ref, ob_ref, c_sc, m_sc, *, heads, dqk, dv):
    @pl.when(pl.program_id(0) == 0)
    def _():
        c_sc[...] = jnp.zeros_like(c_sc)
        m_sc[...] = jnp.zeros_like(m_sc)

    ones_col = (lax.broadcasted_iota(jnp.int32, (CHUNK, LANES), 1) == 0).astype(F32)
    chains = []
    for direction, (q_ref, k_ref, v_ref, i_ref, f_ref, o_ref) in enumerate(
            ((qf_ref, kf_ref, vf_ref, if_ref, ff_ref, of_ref), (qb_ref, kb_ref, vb_ref, ib_ref, fb_ref, ob_ref))):
        incl, _ = _causal_masks(direction)
        log_i = i_ref[...] + ibias_ref[direction]
        log_f = -_softplus(-(f_ref[...] + fbias_ref[direction]))
        bcum = _dot_exact(incl.astype(F32), log_f)
        btot = bcum[CHUNK - 1:CHUNK, :] if direction == 0 else bcum[0:1, :]
        log_end = btot - bcum + log_i
        m_st = m_sc[direction]
        m_new = jnp.maximum(btot + m_st, jnp.max(log_end, axis=0, keepdims=True))
        m_sc[direction] = m_new
        carry = jnp.exp(btot + m_st - m_new)
        k_scale = jnp.exp(log_end - m_new)
        b_inter = bcum + m_st
        bcum_t = bcum.T
        log_i_t = log_i.T
        for h in range(heads):
            log_d = jnp.where(incl, bcum[:, h:h + 1] - bcum_t[h:h + 1, :] + log_i_t[h:h + 1, :], -jnp.inf)
            m_t = jnp.maximum(b_inter[:, h:h + 1], jnp.max(log_d, axis=-1, keepdims=True))
            chains.append(dict(
                q=q_ref[:, h * dqk:(h + 1) * dqk] * (dqk ** -0.5), k=k_ref[:, h * dqk:(h + 1) * dqk],
                v_ext=jnp.concatenate([v_ref[:, h * dv:(h + 1) * dv], ones_col], axis=1),
                p_intra=jnp.exp(log_d - m_t), w_inter=jnp.exp(b_inter[:, h:h + 1] - m_t), floor=jnp.exp(-m_t),
                k_scale=k_scale[:, h:h + 1], carry=carry[:, h:h + 1], o_ref=o_ref, h=h, slot=direction * heads + h))

    s = [_dot_nt(c["q"], c["k"]) * c["p_intra"] for c in chains]
    states = [c_sc[c["slot"]] for c in chains]
    inter = [_dot(c["q"], st) for c, st in zip(chains, states)]
    intra = [_dot(x, c["v_ext"]) for x, c in zip(s, chains)]
    upd = [_dot_tn(c["k"] * c["k_scale"], c["v_ext"]) for c in chains]
    for c, st, x, y, z in zip(chains, states, inter, intra, upd):
        out = c["w_inter"] * x + y
        den = jnp.maximum(jnp.abs(out[:, dv:dv + 1]), c["floor"])
        c["o_ref"][:, c["h"] * dv:(c["h"] + 1) * dv] = out[:, :dv] / den
        c_sc[c["slot"]] = c["carry"] * st + z


def _mlstm_scan_call(p, q_cb, gate_cb, ib, fb, heads, dqk, dv, nc_ctx):
    m = p.shape[0]
    nc = m // CHUNK
    qw, vw = heads * dqk, heads * dv
    assert q_cb % qw == 0 and (q_cb + 2 * qw) % vw == 0
    bidx = functools.partial(_bwd_chunk, nc_ctx=nc_ctx, nc_tot=nc)
    fwd = functools.partial(_fwd_chunk, nc_ctx=nc_ctx, nc_tot=nc)
    qb, kb, vb = q_cb // qw, q_cb // qw + 1, (q_cb + 2 * qw) // vw
    gb = gate_cb // LANES

    def specs(row, direction):
        return [pl.BlockSpec((CHUNK, qw), lambda s: (row(s), qb)),
                pl.BlockSpec((CHUNK, qw), lambda s: (row(s), kb)),
                pl.BlockSpec((CHUNK, vw), lambda s: (row(s), vb)),
                pl.BlockSpec((CHUNK, LANES), lambda s: (row(s), gb + 2 * direction)),
                pl.BlockSpec((CHUNK, LANES), lambda s: (row(s), gb + 2 * direction + 1))]

    return pl.pallas_call(
        functools.partial(_mlstm_scan_kernel, heads=heads, dqk=dqk, dv=dv),
        grid=(nc,),
        in_specs=(specs(fwd, 0) + specs(bidx, 1)
                  + [pl.BlockSpec((2, 1, LANES), lambda s: (0, 0, 0)),
                     pl.BlockSpec((2, 1, LANES), lambda s: (0, 0, 0))]),
        out_specs=[pl.BlockSpec((CHUNK, vw), lambda s: (fwd(s), 0)),
                   pl.BlockSpec((CHUNK, vw), lambda s: (bidx(s), 0))],
        out_shape=[jax.ShapeDtypeStruct((m, vw), F32)] * 2,
        scratch_shapes=[pltpu.VMEM((2 * heads, dqk, dv + LANES), F32), pltpu.VMEM((2, 1, LANES), F32)],
        compiler_params=_cparams(1),
        name="mlstm_scan",
    )(p, p, p, p, p, p, p, p, p, p, ib, fb)


def _filter_kernel(feat_ref, w1_ref, b1_ref, w2_ref, b2_ref, w3_ref, freq_ref, dec_ref, o_ref, h_sc):
    @pl.when(pl.program_id(1) == 0)
    def _():
        h = jnp.sin(freq_ref[0:1, :] * (_dot(feat_ref[...], w1_ref[...]) + b1_ref[...]))
        h_sc[...] = jnp.sin(freq_ref[1:2, :] * (_dot(h, w2_ref[...]) + b2_ref[...]))

    o_ref[...] = _dot(h_sc[...], w3_ref[...]) * jnp.exp(-feat_ref[:, 0:1] * jnp.abs(dec_ref[...]))


def _filter_call(feats, w1, b1, w2, b2, w3, freq, dec, tl, tn):
    length, fp = feats.shape
    hid = w2.shape[0]
    n = w3.shape[1]
    return pl.pallas_call(
        _filter_kernel,
        grid=(length // tl, n // tn),
        in_specs=[pl.BlockSpec((tl, fp), lambda i, j: (i, 0)),
                  pl.BlockSpec((fp, hid), lambda i, j: (0, 0)),
                  pl.BlockSpec((1, hid), lambda i, j: (0, 0)),
                  pl.BlockSpec((hid, hid), lambda i, j: (0, 0)),
                  pl.BlockSpec((1, hid), lambda i, j: (0, 0)),
                  pl.BlockSpec((hid, tn), lambda i, j: (0, j)),
                  pl.BlockSpec((2, hid), lambda i, j: (0, 0)),
                  pl.BlockSpec((1, tn), lambda i, j: (0, j))],
        out_specs=pl.BlockSpec((tl, tn), lambda i, j: (i, j)),
        out_shape=jax.ShapeDtypeStruct((length, n), F32),
        scratch_shapes=[pltpu.VMEM((tl, hid), F32)],
        compiler_params=_cparams(2),
        name="hyena_filter",
    )(feats, w1, b1, w2, b2, w3, freq, dec)


def _dft_constants(length):
    n2 = DFT_N2
    n = 2 * length
    n1 = n // n2
    kh = n1 // 2
    na = kh + 1

    def cis(num, den):
        ang = (2.0 * math.pi / den) * (num % den).astype(F32)
        return jnp.cos(ang), jnp.sin(ang)

    ar = lambda size: jnp.arange(size, dtype=jnp.int32)
    c, s = cis(ar(na)[None, :, None] * (n2 * ar(kh)[None, None, :] + ar(n2)[:, None, None]), n)
    m1 = jnp.stack([c, -s], axis=2).reshape(n2, 2 * na, kh)
    c, s = cis(ar(n2)[:, None] * ar(n2)[None, :], n2)
    g2 = jnp.block([[c, s], [-s, c]])
    c, s = cis(ar(n2)[None, :, None] * (ar(na)[:, None, None] + n1 * ar(n2)[None, None, :]), n)
    minv = jnp.concatenate([jnp.concatenate([c, -s], axis=2), jnp.concatenate([s, c], axis=2)], axis=1)
    c, s = cis(ar(kh)[:, None] * ar(na)[None, :], n1)
    weight = jnp.where(jnp.logical_or(ar(na) == 0, ar(na) == kh), 1.0, 2.0)[None, :] / n
    pinv = jnp.stack([c * weight, -s * weight], axis=2).reshape(kh, 2 * na)
    return m1.astype(BF16), g2.astype(BF16), minv.astype(BF16), pinv.astype(BF16)


def _dft1_kernel(x_ref, m_ref, o_ref, *, n_lo_major):
    for j in range(SUBLANES):
        o_ref[:, j, :] = _dot(m_ref[j], x_ref[j] if n_lo_major else x_ref[:, j, :])


def _dft1_call(x3, col_blk0, width, m1, tc, n_lo_major=False):
    n2, rows, kh = m1.shape
    if n_lo_major:
        x_spec = pl.BlockSpec((SUBLANES, kh, tc), lambda c, g: (g, 0, col_blk0 + c))
    else:
        x_spec = pl.BlockSpec((kh, SUBLANES, tc), lambda c, g: (0, g, col_blk0 + c))
    return pl.pallas_call(
        functools.partial(_dft1_kernel, n_lo_major=n_lo_major),
        grid=(width // tc, n2 // SUBLANES),
        in_specs=[x_spec,
                  pl.BlockSpec((SUBLANES, rows, kh), lambda c, g: (g, 0, 0))],
        out_specs=pl.BlockSpec((rows, SUBLANES, tc), lambda c, g: (0, g, c)),
        out_shape=jax.ShapeDtypeStruct((rows, n2, width), F32),
        compiler_params=_cparams(2),
        name="dft_stage1",
    )(x3, m1)


def _filter_spectrum_kernel(af_ref, ab_ref, g_ref, o_ref):
    sf = _dot(g_ref[...], af_ref[...])
    sb = _dot(g_ref[...], ab_ref[...])
    sign = jnp.where(lax.broadcasted_iota(jnp.int32, (2 * DFT_N2, 1), 0) < DFT_N2, 1.0, -1.0)
    o_ref[...] = sf + sign * sb


def _filter_spectrum_call(a, g2, tc):
    rows, width = a.shape
    half = width // 2
    blk = 2 * DFT_N2
    return pl.pallas_call(
        _filter_spectrum_kernel,
        grid=(rows // blk, half // tc),
        in_specs=[pl.BlockSpec((blk, tc), lambda a_, j: (a_, j)),
                  pl.BlockSpec((blk, tc), lambda a_, j: (a_, half // tc + j)),
                  pl.BlockSpec((blk, blk), lambda a_, j: (0, 0))],
        out_specs=pl.BlockSpec((blk, tc), lambda a_, j: (a_, j)),
        out_shape=jax.ShapeDtypeStruct((rows, half), F32),
        compiler_params=_cparams(2),
        name="hyena_filter_spectrum",
    )(a, a, g2)


def _complex_mul(x, kf, n):
    xr, xi = x[:n], x[n:]
    kr, ki = kf[:n], kf[n:]
    return jnp.concatenate([xr * kr - xi * ki, xr * ki + xi * kr], axis=0)


def _spectral_mid_kernel(a_ref, g_ref, kf_ref, minv_ref, o_ref):
    x = _dot(g_ref[...], a_ref[...])
    o_ref[...] = _dot(minv_ref[0], _complex_mul(x, kf_ref[...], DFT_N2))


def _spectral_mid_call(a, g2, kf, kf_cb, minv, tc):
    rows, width = a.shape
    blk = 2 * DFT_N2
    return pl.pallas_call(
        _spectral_mid_kernel,
        grid=(rows // blk, width // tc),
        in_specs=[pl.BlockSpec((blk, tc), lambda a_, j: (a_, j)),
                  pl.BlockSpec((blk, blk), lambda a_, j: (0, 0)),
                  pl.BlockSpec((blk, tc), lambda a_, j: (a_, kf_cb + j)),
                  pl.BlockSpec((1, blk, blk), lambda a_, j: (a_, 0, 0))],
        out_specs=pl.BlockSpec((blk, tc), lambda a_, j: (a_, j)),
        out_shape=jax.ShapeDtypeStruct((rows, width), F32),
        compiler_params=_cparams(2),
        name="hyena_spectral_mid",
    )(a, g2, kf, minv)


def _idft2_kernel(z_ref, p_ref, v_ref, gate_ref, bias_ref, o_ref, *, v_major, out_major):
    for j in range(SUBLANES):
        y = _dot(p_ref[...], z_ref[:, j, :])
        v = v_ref[j] if v_major else v_ref[:, j, :]
        res = gate_ref[:, j, :] * (y + v * bias_ref[...])
        if out_major:
            o_ref[j] = res
        else:
            o_ref[:, j, :] = res


def _idft2_call(z3, pinv, v3, v_cb, v_major, gate3, gate_cb, bias, width, tc, out_major):
    rows, n2, _ = z3.shape
    kh = pinv.shape[0]

    def time_spec(major, cb):
        if major:
            return pl.BlockSpec((SUBLANES, kh, tc), lambda c, g: (g, 0, cb + c))
        return pl.BlockSpec((kh, SUBLANES, tc), lambda c, g: (0, g, cb + c))

    return pl.pallas_call(
        functools.partial(_idft2_kernel, v_major=v_major, out_major=out_major),
        grid=(width // tc, n2 // SUBLANES),
        in_specs=[pl.BlockSpec((rows, SUBLANES, tc), lambda c, g: (0, g, c)),
                  pl.BlockSpec((kh, rows), lambda c, g: (0, 0)),
                  time_spec(v_major, v_cb),
                  time_spec(False, gate_cb),
                  pl.BlockSpec((1, tc), lambda c, g: (0, c))],
        out_specs=time_spec(out_major, 0),
        out_shape=jax.ShapeDtypeStruct((n2, kh, width) if out_major else (kh, n2, width), F32),
        compiler_params=_cparams(2),
        name="idft_stage2_gate",
    )(z3, pinv, v3, gate3, bias)


def _long_conv(v3, v_cb, v_major, gate3, gate_cb, bias, kf, kf_cb, consts, width, tc, out_major):
    m1, g2, minv, pinv = consts
    a = _dft1_call(v3, v_cb, width, m1, tc, n_lo_major=v_major)
    rows = a.shape[0]
    z = _spectral_mid_call(a.reshape(rows * DFT_N2, width), g2, kf, kf_cb, minv, tc)
    return _idft2_call(z.reshape(rows, DFT_N2, width), pinv, v3, v_cb, v_major, gate3, gate_cb, bias, width, tc,
                       out_major)


def _dense_dft_constants(length):
    n = 2 * length
    k = jnp.arange(n, dtype=jnp.int32)[:, None]
    t = jnp.arange(length, dtype=jnp.int32)[None, :]
    ang = (2.0 * math.pi / n) * ((k * t) % n).astype(F32)
    c, s = jnp.cos(ang), jnp.sin(ang)
    fwd = jnp.concatenate([c, -s], axis=0)
    inv = jnp.concatenate([c.T, -s.T], axis=1) / n
    return fwd.astype(BF16), inv.astype(BF16)


def _dense_spectrum_kernel(tf_ref, tb_ref, f_ref, o_ref, *, n):
    sign = jnp.where(lax.broadcasted_iota(jnp.int32, (2 * n, 1), 0) < n, 1.0, -1.0)
    o_ref[...] = _dot(f_ref[...], tf_ref[...]) + sign * _dot(f_ref[...], tb_ref[...])


def _dense_spectrum_call(taps, fwd, tc):
    length, width = taps.shape
    half = width // 2
    rows = fwd.shape[0]
    return pl.pallas_call(
        functools.partial(_dense_spectrum_kernel, n=rows // 2),
        grid=(half // tc,),
        in_specs=[pl.BlockSpec((length, tc), lambda j: (0, j)),
                  pl.BlockSpec((length, tc), lambda j: (0, half // tc + j)),
                  pl.BlockSpec((rows, length), lambda j: (0, 0))],
        out_specs=pl.BlockSpec((rows, tc), lambda j: (0, j)),
        out_shape=jax.ShapeDtypeStruct((rows, half), F32),
        compiler_params=_cparams(1),
        name="hyena_filter_spectrum_dense",
    )(taps, taps, fwd)


def _dense_hyena_kernel(v_ref, x1_ref, x2_ref, kf0_ref, kf1_ref, f_ref, inv_ref, b0_ref, b1_ref, o_ref, *, n):
    def conv(u, kf_ref, b_ref):
        y = _dot(inv_ref[...], _complex_mul(_dot(f_ref[...], u), kf_ref[...], n))
        return y + u * b_ref[...]

    z = x1_ref[...] * conv(v_ref[...], kf0_ref, b0_ref)
    o_ref[...] = x2_ref[...] * conv(z, kf1_ref, b1_ref)


def _dense_hyena_call(hc, length, width, kf, fwd, inv, bias0, bias1, tc):
    rows = fwd.shape[0]
    per = width // tc
    return pl.pallas_call(
        functools.partial(_dense_hyena_kernel, n=rows // 2),
        grid=(per,),
        in_specs=[pl.BlockSpec((length, tc), lambda j: (0, j)),
                  pl.BlockSpec((length, tc), lambda j: (0, per + j)),
                  pl.BlockSpec((length, tc), lambda j: (0, 2 * per + j)),
                  pl.BlockSpec((rows, tc), lambda j: (0, j)),
                  pl.BlockSpec((rows, tc), lambda j: (0, per + j)),
                  pl.BlockSpec((rows, length), lambda j: (0, 0)),
                  pl.BlockSpec((length, rows), lambda j: (0, 0)),
                  pl.BlockSpec((1, tc), lambda j: (0, j)),
                  pl.BlockSpec((1, tc), lambda j: (0, j))],
        out_specs=pl.BlockSpec((length, tc), lambda j: (0, j)),
        out_shape=jax.ShapeDtypeStruct((length, width), F32),
        compiler_params=_cparams(1),
        name="hyena_dense",
    )(hc, hc, hc, kf, kf, fwd, inv, bias0, bias1)


def _filter_features(length):
    t = jnp.linspace(0.0, 1.0, length, dtype=F32)[:, None]
    w = (2.0 * math.pi / length) * jnp.arange(length, dtype=F32)[:, None]
    bands = jnp.linspace(1e-4, HYENA_BANDS - 1, HYENA_BANDS, dtype=F32)[None, :]
    feats = jnp.concatenate([t, jnp.cos(bands * w), -jnp.sin(bands * w)], axis=-1)
    return jnp.pad(feats, ((0, 0), (0, LANES - feats.shape[1])))


def _head_rms(x, heads, width):
    outs = []
    for h in range(heads):
        xh = x[:, h * width:(h + 1) * width]
        outs.append(xh * lax.rsqrt(jnp.mean(xh * xh, axis=-1, keepdims=True) + RMS_EPS))
    return jnp.concatenate(outs, axis=1)


def _merge_kernel(ogf_ref, ogb_ref, omf_ref, omb_ref, z_ref, o_ref, hy_ref, g0_ref, g1_ref, g2_ref, gn_ref, mn_ref,
                  wa_ref, wb_ref, wc_ref, out_ref, *, gdn_heads, gdn_dv, ml_heads, ml_dv):
    a = _head_rms(ogf_ref[...] + ogb_ref[...], gdn_heads, gdn_dv) * gn_ref[...] * _silu(z_ref[...])
    b = _head_rms(omf_ref[...] + omb_ref[...], ml_heads, ml_dv) * mn_ref[...] * _sigmoid(o_ref[...])
    out_ref[...] = (_sigmoid(g0_ref[...]) * _dot(a, wa_ref[...]) + _sigmoid(g1_ref[...]) * _dot(b, wb_ref[...])
                    + _sigmoid(g2_ref[...]) * _dot(hy_ref[...], wc_ref[...]))


def _merge_call(og, om, p, z_cb, o_cb, hy, gate_cb, gn, mn, wa, wb, wc, layer, heads, tm):
    m, d = p.shape[0], wa.shape[-1]
    gw, mw, hw = og[0].shape[-1], om[0].shape[-1], hy.shape[-1]
    gdn_heads, gdn_dv, ml_heads, ml_dv = heads
    zb, ob, gb = z_cb // gw, o_cb // mw, gate_cb // d
    return pl.pallas_call(
        functools.partial(_merge_kernel, gdn_heads=gdn_heads, gdn_dv=gdn_dv, ml_heads=ml_heads, ml_dv=ml_dv),
        grid=(m // tm,),
        in_specs=[pl.BlockSpec((tm, gw), lambda i: (i, 0)),
                  pl.BlockSpec((tm, gw), lambda i: (i, 0)),
                  pl.BlockSpec((tm, mw), lambda i: (i, 0)),
                  pl.BlockSpec((tm, mw), lambda i: (i, 0)),
                  pl.BlockSpec((tm, gw), lambda i: (i, zb)),
                  pl.BlockSpec((tm, mw), lambda i: (i, ob)),
                  pl.BlockSpec((tm, hw), lambda i: (i, 0)),
                  pl.BlockSpec((tm, d), lambda i: (i, gb)),
                  pl.BlockSpec((tm, d), lambda i: (i, gb + 1)),
                  pl.BlockSpec((tm, d), lambda i: (i, gb + 2)),
                  pl.BlockSpec((1, gw), lambda i: (0, 0)),
                  pl.BlockSpec((1, mw), lambda i: (0, 0)),
                  pl.BlockSpec((None, gw, d), lambda i: (layer, 0, 0), pipeline_mode=pl.Buffered(1)),
                  pl.BlockSpec((None, mw, d), lambda i: (layer, 0, 0), pipeline_mode=pl.Buffered(1)),
                  pl.BlockSpec((None, hw, d), lambda i: (layer, 0, 0), pipeline_mode=pl.Buffered(1))],
        out_specs=pl.BlockSpec((tm, d), lambda i: (i, 0)),
        out_shape=jax.ShapeDtypeStruct((m, d), F32),
        compiler_params=_cparams(1),
        name="branch_merge",
    )(og[0], og[1], om[0], om[1], p, p, hy, p, p, p, gn, mn, wa, wb, wc)


def _outproj_kernel(y_ref, x_ref, m_ref, w_ref, o_ref, *, tm, lat_len):
    gate = jnp.where(_row_is_ctx(pl.program_id(0) * tm, tm, lat_len), m_ref[0:1, :], m_ref[1:2, :])
    o_ref[...] = x_ref[...] + gate * _dot(y_ref[...], w_ref[...])


def _outproj_call(y, x, mods, w, layer, lat_len, tm):
    m, d = x.shape
    return pl.pallas_call(
        functools.partial(_outproj_kernel, tm=tm, lat_len=lat_len),
        grid=(m // tm,),
        in_specs=[pl.BlockSpec((tm, d), lambda i: (i, 0)),
                  pl.BlockSpec((tm, d), lambda i: (i, 0)),
                  pl.BlockSpec((SUBLANES, d), lambda i: (0, 0)),
                  pl.BlockSpec((None, d, d), lambda i: (layer, 0, 0))],
        out_specs=pl.BlockSpec((tm, d), lambda i: (i, 0)),
        out_shape=jax.ShapeDtypeStruct((m, d), F32),
        compiler_params=_cparams(1),
        name="mixer_out_proj",
    )(y, x, mods, w)


def _final_norm_kernel(x_ref, g_ref, o_ref):
    x = x_ref[...]
    o_ref[...] = x * lax.rsqrt(jnp.mean(x * x, axis=-1, keepdims=True) + RMS_EPS) * g_ref[...]


def _final_norm_call(x, g, row0, rows, tm):
    d = x.shape[1]
    return pl.pallas_call(
        _final_norm_kernel,
        grid=(rows // tm,),
        in_specs=[pl.BlockSpec((tm, d), lambda i: (row0 // tm + i, 0)),
                  pl.BlockSpec((1, d), lambda i: (0, 0))],
        out_specs=pl.BlockSpec((tm, d), lambda i: (i, 0)),
        out_shape=jax.ShapeDtypeStruct((rows, d), F32),
        compiler_params=_cparams(1),
        name="final_rms_norm",
    )(x, g)


def _pack_rows(rows, d):
    out = jnp.concatenate([r.reshape(1, d) for r in rows], axis=0)
    return jnp.pad(out, ((0, SUBLANES - out.shape[0]), (0, 0)))


def _lane_row(vals, lanes=LANES):
    return jnp.pad(vals.astype(F32), ((0, 0), (0, lanes - vals.shape[1])))[:, None, :]


def _pick_tile(total, candidates):
    for c in candidates:
        if total % c == 0:
            return c
    raise ValueError(f"no tile for {total} among {candidates}")


def kernel(x, c, ctx, c_ctx, w_ada, b_ada, norm_g, ffn_w_gate, ffn_w_up, ffn_w_down, w_in, gdn_conv, gdn_a_log,
           gdn_dt_bias, gdn_norm, mlstm_i_bias, mlstm_f_bias, mlstm_norm, hy_short_w, hy_short_b, hf_w1, hf_b1,
           hf_w2, hf_b2, hf_w3, hf_freq, hf_decay, hy_bias, w_a_out, w_b_out, w_c_out, w_out, final_norm):
    batch, seq, d = x.shape
    assert batch == 1 and c.shape[0] == 1 and ctx.shape[0] == 1
    ctx_len = ctx.shape[1]
    depth = w_ada.shape[0]
    m = ctx_len + seq
    assert ctx_len % CHUNK == 0 and seq % CHUNK == 0

    gdn_heads = gdn_a_log.shape[-1]
    gdn_vw = w_a_out.shape[1]
    gdn_qkw = (gdn_conv.shape[1] - gdn_vw) // 2
    gdn_dk, gdn_dv = gdn_qkw // gdn_heads, gdn_vw // gdn_heads
    ml_heads = mlstm_i_bias.shape[-1]
    ml_vw = w_b_out.shape[1]
    hy_w = w_c_out.shape[1]
    n_in = w_in.shape[-1]
    ml_qkw = (n_in - (2 * gdn_qkw + gdn_vw) - 4 * gdn_heads - 4 * ml_heads - ml_vw - gdn_vw - ml_vw
              - 3 * hy_w - 3 * d) // 2
    ml_dqk, ml_dv = ml_qkw // ml_heads, ml_vw // ml_heads
    assert gdn_dk == LANES and gdn_dv == LANES and ml_dqk == LANES and ml_dv % LANES == 0

    o_gqkv = 0
    o_gbeta = o_gqkv + 2 * gdn_qkw + gdn_vw
    o_ga = o_gbeta + 2 * gdn_heads
    o_mqkv = o_ga + 2 * gdn_heads
    o_mi = o_mqkv + 2 * ml_qkw + ml_vw
    o_mf = o_mi + 2 * ml_heads
    o_z = o_mf + 2 * ml_heads
    o_o = o_z + gdn_vw
    o_hy = o_o + ml_vw
    o_gates = o_hy + 3 * hy_w
    assert o_gates + 3 * d == n_in

    w_in_b = w_in.astype(BF16)

    def small(off, heads):
        return [jnp.pad(w_in_b[:, :, off + dd * heads:off + (dd + 1) * heads],
                        ((0, 0), (0, 0), (0, LANES - heads))) for dd in range(2)]

    gb, ga = small(o_gbeta, gdn_heads), small(o_ga, gdn_heads)
    mi, mf = small(o_mi, ml_heads), small(o_mf, ml_heads)
    groups = [w_in_b[:, :, o_gqkv:o_gbeta], w_in_b[:, :, o_mqkv:o_mi], w_in_b[:, :, o_z:o_o],
              w_in_b[:, :, o_o:o_hy], w_in_b[:, :, o_hy:o_gates], w_in_b[:, :, o_gates:],
              gb[0], ga[0], gb[1], ga[1], mi[0], mf[0], mi[1], mf[1]]
    w_in_p = jnp.concatenate(groups, axis=-1)
    c_gqkv = 0
    c_mqkv = c_gqkv + 2 * gdn_qkw + gdn_vw
    c_z = c_mqkv + 2 * ml_qkw + ml_vw
    c_o = c_z + gdn_vw
    c_hy = c_o + ml_vw
    c_gates = c_hy + 3 * hy_w
    c_gsmall = c_gates + 3 * d
    c_msmall = c_gsmall + 4 * LANES
    n_p = c_msmall + 4 * LANES
    assert w_in_p.shape[-1] == n_p

    f_hidden = ffn_w_gate.shape[-1]
    tf = 512
    f_pad = -(-f_hidden // tf) * tf
    wg = jnp.pad(ffn_w_gate.astype(BF16), ((0, 0), (0, 0), (0, 0), (0, f_pad - f_hidden)))
    wu = jnp.pad(ffn_w_up.astype(BF16), ((0, 0), (0, 0), (0, 0), (0, f_pad - f_hidden)))
    wd = jnp.pad(ffn_w_down.astype(BF16), ((0, 0), (0, 0), (0, f_pad - f_hidden), (0, 0)))
    wa, wb, wc, wo = (t.astype(BF16) for t in (w_a_out, w_b_out, w_c_out, w_out))

    tm_big = _pick_tile(m, (768, 512, 384, 256, 128, 64))
    tm_mid = _pick_tile(m, (256, 128, 64))
    tm_proj = _pick_tile(m, (1056, 768, 512, 384, 256, 128, 64))
    tm_seg = _pick_tile(math.gcd(ctx_len, seq), (256, 128, 64))
    tn_in = _pick_tile(n_p, (1024, 512, 256, 128))
    nc_ctx = ctx_len // CHUNK

    cond = jnp.pad(jnp.concatenate([c_ctx[None, :], c], axis=0), ((0, SUBLANES - 2), (0, 0)))
    mods = _modulation_call(cond, w_ada, b_ada)

    dft_consts = _dft_constants(seq)
    dense_fwd, dense_inv = _dense_dft_constants(ctx_len)
    feats_lat = _filter_features(seq)
    feats_lat = feats_lat.reshape(seq // DFT_N2, DFT_N2, LANES).swapaxes(0, 1).reshape(seq, LANES)
    feats_ctx = _filter_features(ctx_len)
    tl = _pick_tile(seq, (512, 256, 128, 64))

    s = jnp.concatenate([x[0], ctx[0]], axis=0)
    for l in range(depth):
        last = l == depth - 1

        def mod(idx):
            return [mods[l, 0, idx * d:(idx + 1) * d], mods[l, 1, idx * d:(idx + 1) * d]]

        s = _ffn_call(s, _pack_rows(mod(0) + mod(1) + mod(2), d), norm_g[l, 0][None, :], wg, wu, wd, l, 0,
                      seq, tm_big, tf)
        p = _inproj_call(s, _pack_rows(mod(3) + mod(4), d), norm_g[l, 1][None, :], w_in_p[l], seq, tm_proj, tn_in)

        gqkv = _dwconv_call(p, c_gqkv, 2 * gdn_qkw + gdn_vw, jnp.transpose(gdn_conv[l]),
                            jnp.zeros((1, 2 * gdn_qkw + gdn_vw), F32), seq, tm_seg, 1024, True)
        hc = _dwconv_call(p, c_hy, 3 * hy_w, jnp.transpose(hy_short_w[l]), hy_short_b[l][None, :],
                          seq, tm_seg, 1024, False)

        og = _gdn_scan_call(gqkv, p, c_gsmall // LANES, _lane_row(gdn_a_log[l]), _lane_row(gdn_dt_bias[l]),
                            gdn_heads, gdn_dk, gdn_dv, nc_ctx)
        om = _mlstm_scan_call(p, c_mqkv, c_msmall, _lane_row(mlstm_i_bias[l]), _lane_row(mlstm_f_bias[l]),
                              ml_heads, ml_dqk, ml_dv, nc_ctx)

        fargs = (jnp.pad(hf_w1[l], ((0, LANES - hf_w1.shape[1]), (0, 0))), hf_b1[l][None, :], hf_w2[l],
                 hf_b2[l][None, :], hf_w3[l], hf_freq[l], jnp.tile(hf_decay[l], 2)[None, :])
        taps = _filter_call(feats_lat, *fargs, tl, 1024)
        m1, g2, minv, pinv = dft_consts
        kh = seq // DFT_N2
        ta = _dft1_call(taps.reshape(DFT_N2, kh, taps.shape[1]), 0, taps.shape[1], m1, 1024, n_lo_major=True)
        kf = _filter_spectrum_call(ta.reshape(ta.shape[0] * DFT_N2, taps.shape[1]), g2, 1024)
        hl3 = hc.reshape(m // DFT_N2, DFT_N2, 3 * hy_w)
        tc_h = min(hy_w, 1024)
        per = hy_w // tc_h
        z1 = _long_conv(hl3, 0, False, hl3, per, hy_bias[l, 0][None, :], kf, 0, dft_consts, hy_w, tc_h, True)
        z2 = _long_conv(z1, 0, True, hl3, 2 * per, hy_bias[l, 1][None, :], kf, per, dft_consts, hy_w, tc_h, False)
        z2 = z2.reshape(seq, hy_w)
        if not last:
            taps_c = _filter_call(feats_ctx, *fargs, _pick_tile(ctx_len, (256, 128, 64)), 1024)
            kf_c = _dense_spectrum_call(taps_c, dense_fwd, 512)
            zc = _dense_hyena_call(hc[seq:], ctx_len, hy_w, kf_c, dense_fwd, dense_inv, hy_bias[l, 0][None, :],
                                   hy_bias[l, 1][None, :], 256)
        else:
            zc = jnp.zeros((ctx_len, hy_w), F32)
        hy = jnp.concatenate([z2, zc], axis=0)

        merged = _merge_call(og, om, p, c_z, c_o, hy, c_gates, jnp.tile(gdn_norm[l], gdn_heads)[None, :],
                             mlstm_norm[l][None, :], wa, wb, wc, l, (gdn_heads, gdn_dv, ml_heads, ml_dv), tm_mid)
        s = _outproj_call(merged, s, _pack_rows(mod(5), d), wo, l, seq, tm_mid)
        s = _ffn_call(s, _pack_rows(mod(6) + mod(7) + mod(8), d), norm_g[l, 2][None, :], wg, wu, wd, l, 1,
                      seq, tm_big, tf)

    out = _final_norm_call(s, final_norm[None, :], 0, seq, tm_seg)
    return out[None]
```

```python
import functools
import math

import jax
import jax.numpy as jnp
import numpy as np
from jax import lax
from jax.experimental import pallas as pl
from jax.experimental.pallas import tpu as pltpu

F32 = jnp.float32
BF16 = jnp.bfloat16

N_MOD = 9
RMS_EPS = 1e-6
L2_EPS = 1e-6
CHUNK = 64
GDN_CONV = 5
HYENA_SHORT = 3
HYENA_BANDS = 16
LANES = 128
SUBLANES = 8
DFT_N2 = 128
VMEM_LIMIT = 56 * 1024 * 1024


def _cparams(n_axes):
    return pltpu.CompilerParams(dimension_semantics=("arbitrary",) * n_axes,
                                vmem_limit_bytes=VMEM_LIMIT)


def _sigmoid(x):
    return jax.nn.sigmoid(x)


def _silu(x):
    return x * jax.nn.sigmoid(x)


def _softplus(x):
    return jnp.maximum(x, 0.0) + jnp.log(1.0 + jnp.exp(-jnp.abs(x)))


def _dot(a, b):
    return jnp.dot(a.astype(BF16), b.astype(BF16), preferred_element_type=F32)


def _dot_nt(a, b):
    return lax.dot_general(a.astype(BF16), b.astype(BF16), (((1,), (1,)), ((), ())),
                           preferred_element_type=F32)


def _dot_tn(a, b):
    return lax.dot_general(a.astype(BF16), b.astype(BF16), (((0,), (0,)), ((), ())),
                           preferred_element_type=F32)


def _dot_exact(a, b):
    return jnp.dot(a, b, preferred_element_type=F32, precision=lax.Precision.HIGHEST)


def _row_is_ctx(row0, rows, lat_len):
    r = row0 + lax.broadcasted_iota(jnp.int32, (rows, 1), 0)
    return r >= lat_len


def _modnorm(x, g, shift, scale):
    y = x * lax.rsqrt(jnp.mean(x * x, axis=-1, keepdims=True) + RMS_EPS) * g
    return y * (1.0 + scale) + shift


ROW_CHUNK = 128


def _for_row_chunks(tm, fn):
    rc = math.gcd(tm, ROW_CHUNK)

    def body(c, carry):
        fn(pl.multiple_of(c * rc, rc), rc)
        return carry

    lax.fori_loop(0, tm // rc, body, 0)


def _store_modulated(x_ref, m_ref, g_ref, h_sc, row0, tm, lat_len):
    def chunk(r0, rc):
        is_ctx = _row_is_ctx(row0 + r0, rc, lat_len)
        shift = jnp.where(is_ctx, m_ref[0:1, :], m_ref[1:2, :])
        scale = jnp.where(is_ctx, m_ref[2:3, :], m_ref[3:4, :])
        h_sc[pl.ds(r0, rc), :] = _modnorm(x_ref[pl.ds(r0, rc), :], g_ref[...], shift, scale).astype(h_sc.dtype)

    _for_row_chunks(tm, chunk)


def _mod_kernel(c_ref, w_ref, b_ref, o_ref):
    o_ref[0] = _dot(_silu(c_ref[...]), w_ref[0]) + b_ref[0]


def _modulation_call(cond, w_ada, b_ada):
    depth, d, n = w_ada.shape
    tn = _pick_tile(n, (1024, 512, 256, 128))
    return pl.pallas_call(
        _mod_kernel,
        grid=(depth, n // tn),
        in_specs=[pl.BlockSpec((SUBLANES, d), lambda l, j: (0, 0)),
                  pl.BlockSpec((1, d, tn), lambda l, j: (l, 0, j)),
                  pl.BlockSpec((1, 1, tn), lambda l, j: (l, 0, j))],
        out_specs=pl.BlockSpec((1, SUBLANES, tn), lambda l, j: (l, 0, j)),
        out_shape=jax.ShapeDtypeStruct((depth, SUBLANES, n), F32),
        compiler_params=_cparams(2),
        name="adaln_modulation",
    )(cond, w_ada, b_ada.reshape(depth, 1, n))


def _ffn_kernel(x_ref, m_ref, g_ref, wg_ref, wu_ref, wd_ref, o_ref, h_sc, acc_sc, *, tm, lat_len):
    row0 = pl.program_id(0) * tm
    f = pl.program_id(1)

    @pl.when(f == 0)
    def _():
        _store_modulated(x_ref, m_ref, g_ref, h_sc, row0, tm, lat_len)
        acc_sc[...] = jnp.zeros_like(acc_sc)

    h = h_sc[...]
    a = _silu(_dot(h, wg_ref[...])) * _dot(h, wu_ref[...])
    acc_sc[...] += _dot(a, wd_ref[...])

    @pl.when(f == pl.num_programs(1) - 1)
    def _():
        def chunk(r0, rc):
            gate = jnp.where(_row_is_ctx(row0 + r0, rc, lat_len), m_ref[4:5, :], m_ref[5:6, :])
            rows = pl.ds(r0, rc)
            o_ref[rows, :] = x_ref[rows, :] + 0.5 * gate * acc_sc[rows, :]

        _for_row_chunks(tm, chunk)


def _ffn_call(x, mods, g, wg, wu, wd, layer, which, lat_len, tm, tf):
    m, d = x.shape
    fp = wg.shape[-1]
    return pl.pallas_call(
        functools.partial(_ffn_kernel, tm=tm, lat_len=lat_len),
        grid=(m // tm, fp // tf),
        in_specs=[pl.BlockSpec((tm, d), lambda i, f: (i, 0)),
                  pl.BlockSpec((SUBLANES, d), lambda i, f: (0, 0)),
                  pl.BlockSpec((1, d), lambda i, f: (0, 0)),
                  pl.BlockSpec((None, None, d, tf), lambda i, f: (layer, which, 0, f)),
                  pl.BlockSpec((None, None, d, tf), lambda i, f: (layer, which, 0, f)),
                  pl.BlockSpec((None, None, tf, d), lambda i, f: (layer, which, f, 0))],
        out_specs=pl.BlockSpec((tm, d), lambda i, f: (i, 0)),
        out_shape=jax.ShapeDtypeStruct((m, d), F32),
        scratch_shapes=[pltpu.VMEM((tm, d), BF16), pltpu.VMEM((tm, d), F32)],
        compiler_params=_cparams(2),
        name="macaron_swiglu",
    )(x, mods, g, wg, wu, wd)


def _inproj_kernel(x_ref, m_ref, g_ref, w_ref, o_ref, h_sc, *, tm, lat_len):
    @pl.when(pl.program_id(1) == 0)
    def _():
        _store_modulated(x_ref, m_ref, g_ref, h_sc, pl.program_id(0) * tm, tm, lat_len)

    o_ref[...] = _dot(h_sc[...], w_ref[...])


def _inproj_call(x, mods, g, w, lat_len, tm, tn):
    m, d = x.shape
    n = w.shape[1]
    return pl.pallas_call(
        functools.partial(_inproj_kernel, tm=tm, lat_len=lat_len),
        grid=(m // tm, n // tn),
        in_specs=[pl.BlockSpec((tm, d), lambda i, j: (i, 0)),
                  pl.BlockSpec((SUBLANES, d), lambda i, j: (0, 0)),
                  pl.BlockSpec((1, d), lambda i, j: (0, 0)),
                  pl.BlockSpec((d, tn), lambda i, j: (0, j))],
        out_specs=pl.BlockSpec((tm, tn), lambda i, j: (i, j)),
        out_shape=jax.ShapeDtypeStruct((m, n), F32),
        scratch_shapes=[pltpu.VMEM((tm, d), BF16)],
        compiler_params=_cparams(2),
        name="mixer_in_proj",
    )(x, mods, g, w)


def _dwconv_kernel(prev_ref, cur_ref, next_ref, w_ref, b_ref, o_ref, ext_sc, *, tm, taps, split, m_rows, act):
    row0 = pl.program_id(0) * tm
    at_start = jnp.logical_or(row0 == 0, row0 == split)
    at_end = jnp.logical_or(row0 + tm == split, row0 + tm == m_rows)
    ext_sc[0:SUBLANES, :] = jnp.where(at_start, 0.0, prev_ref[...])
    ext_sc[SUBLANES:SUBLANES + tm, :] = cur_ref[...]
    ext_sc[SUBLANES + tm:2 * SUBLANES + tm, :] = jnp.where(at_end, 0.0, next_ref[...])
    acc = jnp.zeros(o_ref.shape, F32) + b_ref[...]
    for t in range(taps):
        acc = acc + ext_sc[pl.ds(SUBLANES + t - taps // 2, tm), :] * w_ref[t:t + 1, :]
    o_ref[...] = _silu(acc) if act else acc


def _dwconv_call(p, col0, ncols, w_t, b, split, tm, tc, act):
    m = p.shape[0]
    taps = w_t.shape[0]
    assert split % tm == 0 and m % tm == 0 and col0 % tc == 0 and taps // 2 <= SUBLANES
    cb0 = col0 // tc
    rb = tm // SUBLANES
    last = m // SUBLANES - 1
    return pl.pallas_call(
        functools.partial(_dwconv_kernel, tm=tm, taps=taps, split=split, m_rows=m, act=act),
        grid=(m // tm, ncols // tc),
        in_specs=[pl.BlockSpec((SUBLANES, tc), lambda i, j: (jnp.maximum(i * rb - 1, 0), cb0 + j)),
                  pl.BlockSpec((tm, tc), lambda i, j: (i, cb0 + j)),
                  pl.BlockSpec((SUBLANES, tc), lambda i, j: (jnp.minimum((i + 1) * rb, last), cb0 + j)),
                  pl.BlockSpec((taps, tc), lambda i, j: (0, j)),
                  pl.BlockSpec((1, tc), lambda i, j: (0, j))],
        out_specs=pl.BlockSpec((tm, tc), lambda i, j: (i, j)),
        out_shape=jax.ShapeDtypeStruct((m, ncols), F32),
        scratch_shapes=[pltpu.VMEM((tm + 2 * SUBLANES, tc), F32)],
        compiler_params=_cparams(2),
        name="depthwise_conv",
    )(p, p, p, w_t, b)


def _fwd_chunk(s, nc_ctx, nc_tot):
    return jnp.where(s < nc_ctx, nc_tot - nc_ctx + s, s - nc_ctx)


def _bwd_chunk(s, nc_ctx, nc_tot):
    del nc_ctx
    return nc_tot - 1 - s


def _causal_masks(direction):
    ii = lax.broadcasted_iota(jnp.int32, (CHUNK, CHUNK), 0)
    jj = lax.broadcasted_iota(jnp.int32, (CHUNK, CHUNK), 1)
    rel = ii - jj if direction == 0 else jj - ii
    return rel >= 0, rel > 0


def _unit_triangular_inverses(mats):
    ii = lax.broadcasted_iota(jnp.int32, (CHUNK, CHUNK), 0)
    jj = lax.broadcasted_iota(jnp.int32, (CHUNK, CHUNK), 1)
    eye = (ii == jj).astype(F32)

    def same_block(width):
        return (ii // width) == (jj // width)

    inner = same_block(8)
    d1 = [jnp.where(inner, a, 0.0) for a in mats]
    d2 = [_dot(x, x) for x in d1]
    yield
    d4 = [_dot(x, x) for x in d2]
    p = [_dot(eye - x, eye + y) for x, y in zip(d1, d2)]
    yield
    p = [_dot(x, eye + y) for x, y in zip(p, d4)]
    yield
    for width in (16, 32, 64):
        outer = same_block(width)
        ring = jnp.logical_and(outer, jnp.logical_not(inner))
        t = [_dot(x, jnp.where(ring, a, 0.0)) for x, a in zip(p, mats)]
        yield
        p = [x - _dot(y, x) for x, y in zip(p, t)]
        yield
        inner = outer
    return p


def _gdn_scan_kernel(qf_ref, kf_ref, vf_ref, btf_ref, af_ref, qb_ref, kb_ref, vb_ref, btb_ref, ab_ref,
                     alog_ref, dtb_ref, of_ref, ob_ref, s_sc, *, heads, dk, dv):
    @pl.when(pl.program_id(0) == 0)
    def _():
        s_sc[...] = jnp.zeros_like(s_sc)

    chains = []
    for direction, (q_ref, k_ref, v_ref, bt_ref, a_ref, o_ref) in enumerate(
            ((qf_ref, kf_ref, vf_ref, btf_ref, af_ref, of_ref), (qb_ref, kb_ref, vb_ref, btb_ref, ab_ref, ob_ref))):
        incl, strict = _causal_masks(direction)
        beta = _sigmoid(bt_ref[...])
        glog = -jnp.exp(alog_ref[direction]) * _softplus(a_ref[...] + dtb_ref[direction])
        gcum = _dot_exact(incl.astype(F32), glog)
        gtot = gcum[CHUNK - 1:CHUNK, :] if direction == 0 else gcum[0:1, :]
        e_in = jnp.exp(gcum)
        e_out = jnp.exp(gtot - gcum)
        g_end = jnp.exp(gtot)
        gcum_t = gcum.T
        for h in range(heads):
            q = q_ref[:, h * dk:(h + 1) * dk]
            k = k_ref[:, h * dk:(h + 1) * dk]
            q = q * lax.rsqrt(jnp.sum(q * q, axis=-1, keepdims=True) + L2_EPS) * (dk ** -0.5)
            k = k * lax.rsqrt(jnp.sum(k * k, axis=-1, keepdims=True) + L2_EPS)
            b_col = beta[:, h:h + 1]
            kb = k * b_col
            chains.append(dict(
                q=q, k=k, kb=kb, strict=strict, o_ref=o_ref, h=h, slot=direction * heads + h,
                decay=jnp.exp(jnp.where(incl, gcum[:, h:h + 1] - gcum_t[h:h + 1, :], -jnp.inf)),
                rhs=jnp.concatenate([v_ref[:, h * dv:(h + 1) * dv] * b_col, kb * e_in[:, h:h + 1]], axis=1),
                q_in=q * e_in[:, h:h + 1], k_out=k * e_out[:, h:h + 1], g_end=g_end[:, h:h + 1]))

    yield
    a_mats = [jnp.where(c["strict"], _dot_nt(c["kb"], c["k"]) * c["decay"], 0.0) for c in chains]
    qk = [_dot_nt(c["q"], c["k"]) * c["decay"] for c in chains]
    yield
    t_inv = yield from _unit_triangular_inverses(a_mats)
    sol = [_dot(t, c["rhs"]) for t, c in zip(t_inv, chains)]
    yield
    states = [s_sc[c["slot"]] for c in chains]
    v_new = [x[:, :dv] - _dot(x[:, dv:], st) for x, st in zip(sol, states)]
    o_inter = [_dot(c["q_in"], st) for c, st in zip(chains, states)]
    yield
    o_intra = [_dot(x, y) for x, y in zip(qk, v_new)]
    s_upd = [_dot_tn(c["k_out"], y) for c, y in zip(chains, v_new)]
    yield
    for c, st, x, y, z in zip(chains, states, o_inter, o_intra, s_upd):
        c["o_ref"][:, c["h"] * dv:(c["h"] + 1) * dv] = x + y
        s_sc[c["slot"]] = st * c["g_end"] + z


def _gdn_scan_parts(qkv, p, beta_cb, alog, dtb, heads, dk, dv, nc_ctx):
    m = qkv.shape[0]
    nc = m // CHUNK
    qw, vw = heads * dk, heads * dv
    assert qw == vw
    bidx = functools.partial(_bwd_chunk, nc_ctx=nc_ctx, nc_tot=nc)

    def specs(row):
        return [pl.BlockSpec((CHUNK, qw), lambda s: (row(s), 0)),
                pl.BlockSpec((CHUNK, qw), lambda s: (row(s), 1)),
                pl.BlockSpec((CHUNK, vw), lambda s: (row(s), 2))]

    def gate_specs(row, direction):
        return [pl.BlockSpec((CHUNK, LANES), lambda s: (row(s), beta_cb + 2 * direction)),
                pl.BlockSpec((CHUNK, LANES), lambda s: (row(s), beta_cb + 2 * direction + 1))]

    fwd = functools.partial(_fwd_chunk, nc_ctx=nc_ctx, nc_tot=nc)
    return dict(
        kernel=functools.partial(_gdn_scan_kernel, heads=heads, dk=dk, dv=dv),
        in_specs=(specs(fwd) + gate_specs(fwd, 0) + specs(bidx) + gate_specs(bidx, 1)
                  + [pl.BlockSpec((2, 1, LANES), lambda s: (0, 0, 0)),
                     pl.BlockSpec((2, 1, LANES), lambda s: (0, 0, 0))]),
        operands=(qkv, qkv, qkv, p, p, qkv, qkv, qkv, p, p, alog, dtb),
        out_specs=[pl.BlockSpec((CHUNK, vw), lambda s: (fwd(s), 0)),
                   pl.BlockSpec((CHUNK, vw), lambda s: (bidx(s), 0))],
        out_shape=[jax.ShapeDtypeStruct((m, vw), F32)] * 2,
        scratch_shapes=[pltpu.VMEM((2 * heads, dk, dv), F32)])


def _mlstm_scan_kernel(qf_ref, kf_ref, vf_ref, if_ref, ff_ref, qb_ref, kb_ref, vb_ref, ib_ref, fb_ref,
                       ibias_ref, fbias_ref, of_ref, ob_ref, c_sc, m_sc, *, heads, dqk, dv):
    @pl.when(pl.program_id(0) == 0)
    def _():
        c_sc[...] = jnp.zeros_like(c_sc)
        m_sc[...] = jnp.zeros_like(m_sc)

    ones_col = (lax.broadcasted_iota(jnp.int32, (CHUNK, LANES), 1) == 0).astype(F32)
    chains = []
    for direction, (q_ref, k_ref, v_ref, i_ref, f_ref, o_ref) in enumerate(
            ((qf_ref, kf_ref, vf_ref, if_ref, ff_ref, of_ref), (qb_ref, kb_ref, vb_ref, ib_ref, fb_ref, ob_ref))):
        incl, _ = _causal_masks(direction)
        log_i = i_ref[...] + ibias_ref[direction]
        log_f = -_softplus(-(f_ref[...] + fbias_ref[direction]))
        bcum = _dot_exact(incl.astype(F32), log_f)
        btot = bcum[CHUNK - 1:CHUNK, :] if direction == 0 else bcum[0:1, :]
        log_end = btot - bcum + log_i
        m_st = m_sc[direction]
        m_new = jnp.maximum(btot + m_st, jnp.max(log_end, axis=0, keepdims=True))
        m_sc[direction] = m_new
        carry = jnp.exp(btot + m_st - m_new)
        k_scale = jnp.exp(log_end - m_new)
        b_inter = bcum + m_st
        bcum_t = bcum.T
        log_i_t = log_i.T
        for h in range(heads):
            log_d = jnp.where(incl, bcum[:, h:h + 1] - bcum_t[h:h + 1, :] + log_i_t[h:h + 1, :], -jnp.inf)
            m_t = jnp.maximum(b_inter[:, h:h + 1], jnp.max(log_d, axis=-1, keepdims=True))
            chains.append(dict(
                q=q_ref[:, h * dqk:(h + 1) * dqk] * (dqk ** -0.5), k=k_ref[:, h * dqk:(h + 1) * dqk],
                v_ext=jnp.concatenate([v_ref[:, h * dv:(h + 1) * dv], ones_col], axis=1),
                p_intra=jnp.exp(log_d - m_t), w_inter=jnp.exp(b_inter[:, h:h + 1] - m_t), floor=jnp.exp(-m_t),
                k_scale=k_scale[:, h:h + 1], carry=carry[:, h:h + 1], o_ref=o_ref, h=h, slot=direction * heads + h))
        yield

    s = [_dot_nt(c["q"], c["k"]) * c["p_intra"] for c in chains]
    yield
    states = [c_sc[c["slot"]] for c in chains]
    inter = [_dot(c["q"], st) for c, st in zip(chains, states)]
    yield
    intra = [_dot(x, c["v_ext"]) for x, c in zip(s, chains)]
    yield
    upd = [_dot_tn(c["k"] * c["k_scale"], c["v_ext"]) for c in chains]
    yield
    for idx, (c, st, x, y, z) in enumerate(zip(chains, states, inter, intra, upd)):
        out = c["w_inter"] * x + y
        den = jnp.maximum(jnp.abs(out[:, dv:dv + 1]), c["floor"])
        c["o_ref"][:, c["h"] * dv:(c["h"] + 1) * dv] = out[:, :dv] / den
        c_sc[c["slot"]] = c["carry"] * st + z
        if idx % 2 == 1:
            yield


def _mlstm_scan_parts(p, q_cb, gate_cb, ib, fb, heads, dqk, dv, nc_ctx):
    m = p.shape[0]
    nc = m // CHUNK
    qw, vw = heads * dqk, heads * dv
    assert q_cb % qw == 0 and (q_cb + 2 * qw) % vw == 0
    bidx = functools.partial(_bwd_chunk, nc_ctx=nc_ctx, nc_tot=nc)
    fwd = functools.partial(_fwd_chunk, nc_ctx=nc_ctx, nc_tot=nc)
    qb, kb, vb = q_cb // qw, q_cb // qw + 1, (q_cb + 2 * qw) // vw
    gb = gate_cb // LANES

    def specs(row, direction):
        return [pl.BlockSpec((CHUNK, qw), lambda s: (row(s), qb)),
                pl.BlockSpec((CHUNK, qw), lambda s: (row(s), kb)),
                pl.BlockSpec((CHUNK, vw), lambda s: (row(s), vb)),
                pl.BlockSpec((CHUNK, LANES), lambda s: (row(s), gb + 2 * direction)),
                pl.BlockSpec((CHUNK, LANES), lambda s: (row(s), gb + 2 * direction + 1))]

    return dict(
        kernel=functools.partial(_mlstm_scan_kernel, heads=heads, dqk=dqk, dv=dv),
        in_specs=(specs(fwd, 0) + specs(bidx, 1)
                  + [pl.BlockSpec((2, 1, LANES), lambda s: (0, 0, 0)),
                     pl.BlockSpec((2, 1, LANES), lambda s: (0, 0, 0))]),
        operands=(p, p, p, p, p, p, p, p, p, p, ib, fb),
        out_specs=[pl.BlockSpec((CHUNK, vw), lambda s: (fwd(s), 0)),
                   pl.BlockSpec((CHUNK, vw), lambda s: (bidx(s), 0))],
        out_shape=[jax.ShapeDtypeStruct((m, vw), F32)] * 2,
        scratch_shapes=[pltpu.VMEM((2 * heads, dqk, dv + LANES), F32), pltpu.VMEM((2, 1, LANES), F32)])


def _fused_scans_kernel(*refs, parts):
    n_in = sum(part[1] for part in parts)
    n_out = sum(part[2] for part in parts)
    i0, o0, s0 = 0, n_in, n_in + n_out
    running = []
    for fn, ni, no, ns in parts:
        running.append(fn(*refs[i0:i0 + ni], *refs[o0:o0 + no], *refs[s0:s0 + ns]))
        i0, o0, s0 = i0 + ni, o0 + no, s0 + ns
    done = object()
    while running:
        running = [g for g in running if next(g, done) is not done]


def _fused_scans_call(nc, *scans):
    parts = tuple((sc["kernel"], len(sc["in_specs"]), len(sc["out_specs"]), len(sc["scratch_shapes"]))
                  for sc in scans)
    outs = pl.pallas_call(
        functools.partial(_fused_scans_kernel, parts=parts),
        grid=(nc,),
        in_specs=sum((list(sc["in_specs"]) for sc in scans), []),
        out_specs=sum((list(sc["out_specs"]) for sc in scans), []),
        out_shape=sum((list(sc["out_shape"]) for sc in scans), []),
        scratch_shapes=sum((list(sc["scratch_shapes"]) for sc in scans), []),
        compiler_params=_cparams(1),
        name="gdn_mlstm_scans",
    )(*sum((list(sc["operands"]) for sc in scans), []))
    split, res = 0, []
    for sc in scans:
        res.append(tuple(outs[split:split + len(sc["out_specs"])]))
        split += len(sc["out_specs"])
    return res


def _filter_kernel(feat_ref, w1_ref, b1_ref, w2_ref, b2_ref, w3_ref, freq_ref, dec_ref, o_ref, h_sc):
    @pl.when(pl.program_id(1) == 0)
    def _():
        h = jnp.sin(freq_ref[0:1, :] * (_dot(feat_ref[...], w1_ref[...]) + b1_ref[...]))
        h_sc[...] = jnp.sin(freq_ref[1:2, :] * (_dot(h, w2_ref[...]) + b2_ref[...]))

    o_ref[...] = _dot(h_sc[...], w3_ref[...]) * jnp.exp(-feat_ref[:, 0:1] * jnp.abs(dec_ref[...]))


def _filter_call(feats, w1, b1, w2, b2, w3, freq, dec, tl, tn):
    length, fp = feats.shape
    hid = w2.shape[0]
    n = w3.shape[1]
    return pl.pallas_call(
        _filter_kernel,
        grid=(length // tl, n // tn),
        in_specs=[pl.BlockSpec((tl, fp), lambda i, j: (i, 0)),
                  pl.BlockSpec((fp, hid), lambda i, j: (0, 0)),
                  pl.BlockSpec((1, hid), lambda i, j: (0, 0)),
                  pl.BlockSpec((hid, hid), lambda i, j: (0, 0)),
                  pl.BlockSpec((1, hid), lambda i, j: (0, 0)),
                  pl.BlockSpec((hid, tn), lambda i, j: (0, j)),
                  pl.BlockSpec((2, hid), lambda i, j: (0, 0)),
                  pl.BlockSpec((1, tn), lambda i, j: (0, j))],
        out_specs=pl.BlockSpec((tl, tn), lambda i, j: (i, j)),
        out_shape=jax.ShapeDtypeStruct((length, n), F32),
        scratch_shapes=[pltpu.VMEM((tl, hid), F32)],
        compiler_params=_cparams(2),
        name="hyena_filter",
    )(feats, w1, b1, w2, b2, w3, freq, dec)


def _dft_constants(length):
    n2 = DFT_N2
    n = 2 * length
    n1 = n // n2
    kh = n1 // 2
    na = kh + 1

    def cis(num, den):
        ang = (2.0 * math.pi / den) * (num % den).astype(F32)
        return jnp.cos(ang), jnp.sin(ang)

    ar = lambda size: jnp.arange(size, dtype=jnp.int32)
    c, s = cis(ar(na)[None, :, None] * (n2 * ar(kh)[None, None, :] + ar(n2)[:, None, None]), n)
    m1 = jnp.stack([c, -s], axis=2).reshape(n2, 2 * na, kh)
    c, s = cis(ar(n2)[:, None] * ar(n2)[None, :], n2)
    g2 = jnp.block([[c, s], [-s, c]])
    c, s = cis(ar(n2)[None, :, None] * (ar(na)[:, None, None] + n1 * ar(n2)[None, None, :]), n)
    minv = jnp.concatenate([jnp.concatenate([c, -s], axis=2), jnp.concatenate([s, c], axis=2)], axis=1)
    c, s = cis(ar(kh)[:, None] * ar(na)[None, :], n1)
    weight = jnp.where(jnp.logical_or(ar(na) == 0, ar(na) == kh), 1.0, 2.0)[None, :] / n
    pinv = jnp.stack([c * weight, -s * weight], axis=2).reshape(kh, 2 * na)
    return m1.astype(BF16), g2.astype(BF16), minv.astype(BF16), pinv.astype(BF16)


def _dft1_kernel(x_ref, m_ref, o_ref, *, n_lo_major):
    for j in range(SUBLANES):
        o_ref[:, j, :] = _dot(m_ref[j], x_ref[j] if n_lo_major else x_ref[:, j, :])


def _dft1_call(x3, col_blk0, width, m1, tc, n_lo_major=False):
    n2, rows, kh = m1.shape
    if n_lo_major:
        x_spec = pl.BlockSpec((SUBLANES, kh, tc), lambda c, g: (g, 0, col_blk0 + c))
    else:
        x_spec = pl.BlockSpec((kh, SUBLANES, tc), lambda c, g: (0, g, col_blk0 + c))
    return pl.pallas_call(
        functools.partial(_dft1_kernel, n_lo_major=n_lo_major),
        grid=(width // tc, n2 // SUBLANES),
        in_specs=[x_spec,
                  pl.BlockSpec((SUBLANES, rows, kh), lambda c, g: (g, 0, 0))],
        out_specs=pl.BlockSpec((rows, SUBLANES, tc), lambda c, g: (0, g, c)),
        out_shape=jax.ShapeDtypeStruct((rows, n2, width), F32),
        compiler_params=_cparams(2),
        name="dft_stage1",
    )(x3, m1)


def _filter_spectrum_kernel(af_ref, ab_ref, g_ref, o_ref):
    sf = _dot(g_ref[...], af_ref[...])
    sb = _dot(g_ref[...], ab_ref[...])
    sign = jnp.where(lax.broadcasted_iota(jnp.int32, (2 * DFT_N2, 1), 0) < DFT_N2, 1.0, -1.0)
    o_ref[...] = sf + sign * sb


def _filter_spectrum_call(a, g2, tc):
    rows, width = a.shape
    half = width // 2
    blk = 2 * DFT_N2
    return pl.pallas_call(
        _filter_spectrum_kernel,
        grid=(rows // blk, half // tc),
        in_specs=[pl.BlockSpec((blk, tc), lambda a_, j: (a_, j)),
                  pl.BlockSpec((blk, tc), lambda a_, j: (a_, half // tc + j)),
                  pl.BlockSpec((blk, blk), lambda a_, j: (0, 0))],
        out_specs=pl.BlockSpec((blk, tc), lambda a_, j: (a_, j)),
        out_shape=jax.ShapeDtypeStruct((rows, half), F32),
        compiler_params=_cparams(2),
        name="hyena_filter_spectrum",
    )(a, a, g2)


def _complex_mul(x, kf, n):
    xr, xi = x[:n], x[n:]
    kr, ki = kf[:n], kf[n:]
    return jnp.concatenate([xr * kr - xi * ki, xr * ki + xi * kr], axis=0)


def _spectral_mid_kernel(a_ref, g_ref, kf_ref, minv_ref, o_ref):
    x = _dot(g_ref[...], a_ref[...])
    o_ref[...] = _dot(minv_ref[0], _complex_mul(x, kf_ref[...], DFT_N2))


def _spectral_mid_call(a, g2, kf, kf_cb, minv, tc):
    rows, width = a.shape
    blk = 2 * DFT_N2
    return pl.pallas_call(
        _spectral_mid_kernel,
        grid=(rows // blk, width // tc),
        in_specs=[pl.BlockSpec((blk, tc), lambda a_, j: (a_, j)),
                  pl.BlockSpec((blk, blk), lambda a_, j: (0, 0)),
                  pl.BlockSpec((blk, tc), lambda a_, j: (a_, kf_cb + j)),
                  pl.BlockSpec((1, blk, blk), lambda a_, j: (a_, 0, 0))],
        out_specs=pl.BlockSpec((blk, tc), lambda a_, j: (a_, j)),
        out_shape=jax.ShapeDtypeStruct((rows, width), F32),
        compiler_params=_cparams(2),
        name="hyena_spectral_mid",
    )(a, g2, kf, minv)


def _idft2_kernel(z_ref, p_ref, v_ref, gate_ref, bias_ref, o_ref, *, v_major, out_major):
    for j in range(SUBLANES):
        y = _dot(p_ref[...], z_ref[:, j, :])
        v = v_ref[j] if v_major else v_ref[:, j, :]
        res = gate_ref[:, j, :] * (y + v * bias_ref[...])
        if out_major:
            o_ref[j] = res
        else:
            o_ref[:, j, :] = res


def _idft2_call(z3, pinv, v3, v_cb, v_major, gate3, gate_cb, bias, width, tc, out_major):
    rows, n2, _ = z3.shape
    kh = pinv.shape[0]

    def time_spec(major, cb):
        if major:
            return pl.BlockSpec((SUBLANES, kh, tc), lambda c, g: (g, 0, cb + c))
        return pl.BlockSpec((kh, SUBLANES, tc), lambda c, g: (0, g, cb + c))

    return pl.pallas_call(
        functools.partial(_idft2_kernel, v_major=v_major, out_major=out_major),
        grid=(width // tc, n2 // SUBLANES),
        in_specs=[pl.BlockSpec((rows, SUBLANES, tc), lambda c, g: (0, g, c)),
                  pl.BlockSpec((kh, rows), lambda c, g: (0, 0)),
                  time_spec(v_major, v_cb),
                  time_spec(False, gate_cb),
                  pl.BlockSpec((1, tc), lambda c, g: (0, c))],
        out_specs=time_spec(out_major, 0),
        out_shape=jax.ShapeDtypeStruct((n2, kh, width) if out_major else (kh, n2, width), F32),
        compiler_params=_cparams(2),
        name="idft_stage2_gate",
    )(z3, pinv, v3, gate3, bias)


def _long_conv(v3, v_cb, v_major, gate3, gate_cb, bias, kf, kf_cb, consts, width, tc, out_major):
    m1, g2, minv, pinv = consts
    a = _dft1_call(v3, v_cb, width, m1, tc, n_lo_major=v_major)
    rows = a.shape[0]
    z = _spectral_mid_call(a.reshape(rows * DFT_N2, width), g2, kf, kf_cb, minv, tc)
    return _idft2_call(z.reshape(rows, DFT_N2, width), pinv, v3, v_cb, v_major, gate3, gate_cb, bias, width, tc,
                       out_major)


def _dense_dft_constants(length):
    n = 2 * length
    k = jnp.arange(n, dtype=jnp.int32)[:, None]
    t = jnp.arange(length, dtype=jnp.int32)[None, :]
    ang = (2.0 * math.pi / n) * ((k * t) % n).astype(F32)
    c, s = jnp.cos(ang), jnp.sin(ang)
    fwd = jnp.concatenate([c, -s], axis=0)
    inv = jnp.concatenate([c.T, -s.T], axis=1) / n
    return fwd.astype(BF16), inv.astype(BF16)


def _dense_spectrum_kernel(tf_ref, tb_ref, f_ref, o_ref, *, n):
    sign = jnp.where(lax.broadcasted_iota(jnp.int32, (2 * n, 1), 0) < n, 1.0, -1.0)
    o_ref[...] = _dot(f_ref[...], tf_ref[...]) + sign * _dot(f_ref[...], tb_ref[...])


def _dense_spectrum_call(taps, fwd, tc):
    length, width = taps.shape
    half = width // 2
    rows = fwd.shape[0]
    return pl.pallas_call(
        functools.partial(_dense_spectrum_kernel, n=rows // 2),
        grid=(half // tc,),
        in_specs=[pl.BlockSpec((length, tc), lambda j: (0, j)),
                  pl.BlockSpec((length, tc), lambda j: (0, half // tc + j)),
                  pl.BlockSpec((rows, length), lambda j: (0, 0))],
        out_specs=pl.BlockSpec((rows, tc), lambda j: (0, j)),
        out_shape=jax.ShapeDtypeStruct((rows, half), F32),
        compiler_params=_cparams(1),
        name="hyena_filter_spectrum_dense",
    )(taps, taps, fwd)


def _dense_hyena_kernel(v_ref, x1_ref, x2_ref, kf0_ref, kf1_ref, f_ref, inv_ref, b0_ref, b1_ref, o_ref, *, n):
    def conv(u, kf_ref, b_ref):
        y = _dot(inv_ref[...], _complex_mul(_dot(f_ref[...], u), kf_ref[...], n))
        return y + u * b_ref[...]

    z = x1_ref[...] * conv(v_ref[...], kf0_ref, b0_ref)
    o_ref[...] = x2_ref[...] * conv(z, kf1_ref, b1_ref)


def _dense_hyena_call(hc, length, width, kf, fwd, inv, bias0, bias1, tc):
    rows = fwd.shape[0]
    per = width // tc
    return pl.pallas_call(
        functools.partial(_dense_hyena_kernel, n=rows // 2),
        grid=(per,),
        in_specs=[pl.BlockSpec((length, tc), lambda j: (0, j)),
                  pl.BlockSpec((length, tc), lambda j: (0, per + j)),
                  pl.BlockSpec((length, tc), lambda j: (0, 2 * per + j)),
                  pl.BlockSpec((rows, tc), lambda j: (0, j)),
                  pl.BlockSpec((rows, tc), lambda j: (0, per + j)),
                  pl.BlockSpec((rows, length), lambda j: (0, 0)),
                  pl.BlockSpec((length, rows), lambda j: (0, 0)),
                  pl.BlockSpec((1, tc), lambda j: (0, j)),
                  pl.BlockSpec((1, tc), lambda j: (0, j))],
        out_specs=pl.BlockSpec((length, tc), lambda j: (0, j)),
        out_shape=jax.ShapeDtypeStruct((length, width), F32),
        compiler_params=_cparams(1),
        name="hyena_dense",
    )(hc, hc, hc, kf, kf, fwd, inv, bias0, bias1)


def _filter_features(length):
    t = jnp.linspace(0.0, 1.0, length, dtype=F32)[:, None]
    w = (2.0 * math.pi / length) * jnp.arange(length, dtype=F32)[:, None]
    bands = jnp.linspace(1e-4, HYENA_BANDS - 1, HYENA_BANDS, dtype=F32)[None, :]
    feats = jnp.concatenate([t, jnp.cos(bands * w), -jnp.sin(bands * w)], axis=-1)
    return jnp.pad(feats, ((0, 0), (0, LANES - feats.shape[1])))


def _head_rms(x, heads, width):
    outs = []
    for h in range(heads):
        xh = x[:, h * width:(h + 1) * width]
        outs.append(xh * lax.rsqrt(jnp.mean(xh * xh, axis=-1, keepdims=True) + RMS_EPS))
    return jnp.concatenate(outs, axis=1)


def _merge_kernel(ogf_ref, ogb_ref, omf_ref, omb_ref, z_ref, o_ref, hy_ref, g0_ref, g1_ref, g2_ref, gn_ref, mn_ref,
                  wa_ref, wb_ref, wc_ref, out_ref, *, gdn_heads, gdn_dv, ml_heads, ml_dv):
    a = _head_rms(ogf_ref[...] + ogb_ref[...], gdn_heads, gdn_dv) * gn_ref[...] * _silu(z_ref[...])
    b = _head_rms(omf_ref[...] + omb_ref[...], ml_heads, ml_dv) * mn_ref[...] * _sigmoid(o_ref[...])
    out_ref[...] = (_sigmoid(g0_ref[...]) * _dot(a, wa_ref[...]) + _sigmoid(g1_ref[...]) * _dot(b, wb_ref[...])
                    + _sigmoid(g2_ref[...]) * _dot(hy_ref[...], wc_ref[...]))


def _merge_call(og, om, p, z_cb, o_cb, hy, gate_cb, gn, mn, wa, wb, wc, layer, heads, tm):
    m, d = p.shape[0], wa.shape[-1]
    gw, mw, hw = og[0].shape[-1], om[0].shape[-1], hy.shape[-1]
    gdn_heads, gdn_dv, ml_heads, ml_dv = heads
    zb, ob, gb = z_cb // gw, o_cb // mw, gate_cb // d
    return pl.pallas_call(
        functools.partial(_merge_kernel, gdn_heads=gdn_heads, gdn_dv=gdn_dv, ml_heads=ml_heads, ml_dv=ml_dv),
        grid=(m // tm,),
        in_specs=[pl.BlockSpec((tm, gw), lambda i: (i, 0)),
                  pl.BlockSpec((tm, gw), lambda i: (i, 0)),
                  pl.BlockSpec((tm, mw), lambda i: (i, 0)),
                  pl.BlockSpec((tm, mw), lambda i: (i, 0)),
                  pl.BlockSpec((tm, gw), lambda i: (i, zb)),
                  pl.BlockSpec((tm, mw), lambda i: (i, ob)),
                  pl.BlockSpec((tm, hw), lambda i: (i, 0)),
                  pl.BlockSpec((tm, d), lambda i: (i, gb)),
                  pl.BlockSpec((tm, d), lambda i: (i, gb + 1)),
                  pl.BlockSpec((tm, d), lambda i: (i, gb + 2)),
                  pl.BlockSpec((1, gw), lambda i: (0, 0)),
                  pl.BlockSpec((1, mw), lambda i: (0, 0)),
                  pl.BlockSpec((None, gw, d), lambda i: (layer, 0, 0), pipeline_mode=pl.Buffered(1)),
                  pl.BlockSpec((None, mw, d), lambda i: (layer, 0, 0), pipeline_mode=pl.Buffered(1)),
                  pl.BlockSpec((None, hw, d), lambda i: (layer, 0, 0), pipeline_mode=pl.Buffered(1))],
        out_specs=pl.BlockSpec((tm, d), lambda i: (i, 0)),
        out_shape=jax.ShapeDtypeStruct((m, d), F32),
        compiler_params=_cparams(1),
        name="branch_merge",
    )(og[0], og[1], om[0], om[1], p, p, hy, p, p, p, gn, mn, wa, wb, wc)


def _outproj_kernel(y_ref, x_ref, m_ref, w_ref, o_ref, *, tm, lat_len):
    gate = jnp.where(_row_is_ctx(pl.program_id(0) * tm, tm, lat_len), m_ref[0:1, :], m_ref[1:2, :])
    o_ref[...] = x_ref[...] + gate * _dot(y_ref[...], w_ref[...])


def _outproj_call(y, x, mods, w, layer, lat_len, tm):
    m, d = x.shape
    return pl.pallas_call(
        functools.partial(_outproj_kernel, tm=tm, lat_len=lat_len),
        grid=(m // tm,),
        in_specs=[pl.BlockSpec((tm, d), lambda i: (i, 0)),
                  pl.BlockSpec((tm, d), lambda i: (i, 0)),
                  pl.BlockSpec((SUBLANES, d), lambda i: (0, 0)),
                  pl.BlockSpec((None, d, d), lambda i: (layer, 0, 0))],
        out_specs=pl.BlockSpec((tm, d), lambda i: (i, 0)),
        out_shape=jax.ShapeDtypeStruct((m, d), F32),
        compiler_params=_cparams(1),
        name="mixer_out_proj",
    )(y, x, mods, w)


def _final_norm_kernel(x_ref, g_ref, o_ref):
    x = x_ref[...]
    o_ref[...] = x * lax.rsqrt(jnp.mean(x * x, axis=-1, keepdims=True) + RMS_EPS) * g_ref[...]


def _final_norm_call(x, g, row0, rows, tm):
    d = x.shape[1]
    return pl.pallas_call(
        _final_norm_kernel,
        grid=(rows // tm,),
        in_specs=[pl.BlockSpec((tm, d), lambda i: (row0 // tm + i, 0)),
                  pl.BlockSpec((1, d), lambda i: (0, 0))],
        out_specs=pl.BlockSpec((tm, d), lambda i: (i, 0)),
        out_shape=jax.ShapeDtypeStruct((rows, d), F32),
        compiler_params=_cparams(1),
        name="final_rms_norm",
    )(x, g)


def _regroup_kernel(*refs, shift, width, n_blk):
    cat = jnp.concatenate([r[...] for r in refs[:n_blk]], axis=1)
    refs[-1][...] = cat[:, shift:shift + width].astype(refs[-1].dtype)


def _regroup_call(src, buf, src_col, dst_col, ncols, n_total, tcr):
    depth, d, src_cols = src.shape
    q, shift = divmod(src_col, LANES)
    per = tcr // LANES
    n_blk = per + 1
    last = pl.cdiv(src_cols, LANES) - 1
    tb = _pick_tile(d, (1024, 512, 256, 128))
    in_specs = [pl.BlockSpec((None, tb, LANES), lambda l, i, t, kk=kk: (l, i, jnp.minimum(q + per * t + kk, last)))
                for kk in range(n_blk)]
    operands = [src] * n_blk
    aliases = {}
    if buf is not None:
        in_specs.append(pl.BlockSpec(memory_space=pl.ANY))
        operands.append(buf)
        aliases = {n_blk: 0}
    return pl.pallas_call(
        functools.partial(_regroup_kernel, shift=shift, width=tcr, n_blk=n_blk),
        grid=(depth, d // tb, ncols // tcr),
        in_specs=in_specs,
        out_specs=pl.BlockSpec((None, tb, tcr), lambda l, i, t: (l, i, dst_col // tcr + t)),
        out_shape=jax.ShapeDtypeStruct((depth, d, n_total), BF16),
        input_output_aliases=aliases,
        compiler_params=_cparams(3),
        name="w_in_regroup",
    )(*operands)


def _pack_rows(rows, d):
    out = jnp.concatenate([r.reshape(1, d) for r in rows], axis=0)
    return jnp.pad(out, ((0, SUBLANES - out.shape[0]), (0, 0)))


def _lane_row(vals, lanes=LANES):
    return jnp.pad(vals.astype(F32), ((0, 0), (0, lanes - vals.shape[1])))[:, None, :]


def _pick_tile(total, candidates):
    for c in candidates:
        if total % c == 0:
            return c
    raise ValueError(f"no tile for {total} among {candidates}")


def kernel(x, c, ctx, c_ctx, w_ada, b_ada, norm_g, ffn_w_gate, ffn_w_up, ffn_w_down, w_in, gdn_conv, gdn_a_log,
           gdn_dt_bias, gdn_norm, mlstm_i_bias, mlstm_f_bias, mlstm_norm, hy_short_w, hy_short_b, hf_w1, hf_b1,
           hf_w2, hf_b2, hf_w3, hf_freq, hf_decay, hy_bias, w_a_out, w_b_out, w_c_out, w_out, final_norm):
    batch, seq, d = x.shape
    assert batch == 1 and c.shape[0] == 1 and ctx.shape[0] == 1
    ctx_len = ctx.shape[1]
    depth = w_ada.shape[0]
    m = ctx_len + seq
    assert ctx_len % CHUNK == 0 and seq % CHUNK == 0

    gdn_heads = gdn_a_log.shape[-1]
    gdn_vw = w_a_out.shape[1]
    gdn_qkw = (gdn_conv.shape[1] - gdn_vw) // 2
    gdn_dk, gdn_dv = gdn_qkw // gdn_heads, gdn_vw // gdn_heads
    ml_heads = mlstm_i_bias.shape[-1]
    ml_vw = w_b_out.shape[1]
    hy_w = w_c_out.shape[1]
    n_in = w_in.shape[-1]
    ml_qkw = (n_in - (2 * gdn_qkw + gdn_vw) - 4 * gdn_heads - 4 * ml_heads - ml_vw - gdn_vw - ml_vw
              - 3 * hy_w - 3 * d) // 2
    ml_dqk, ml_dv = ml_qkw // ml_heads, ml_vw // ml_heads
    assert gdn_dk == LANES and gdn_dv == LANES and ml_dqk == LANES and ml_dv % LANES == 0

    o_gqkv = 0
    o_gbeta = o_gqkv + 2 * gdn_qkw + gdn_vw
    o_ga = o_gbeta + 2 * gdn_heads
    o_mqkv = o_ga + 2 * gdn_heads
    o_mi = o_mqkv + 2 * ml_qkw + ml_vw
    o_mf = o_mi + 2 * ml_heads
    o_z = o_mf + 2 * ml_heads
    o_o = o_z + gdn_vw
    o_hy = o_o + ml_vw
    o_gates = o_hy + 3 * hy_w
    assert o_gates + 3 * d == n_in

    c_gqkv = 0
    c_mqkv = c_gqkv + 2 * gdn_qkw + gdn_vw
    c_z = c_mqkv + 2 * ml_qkw + ml_vw
    c_o = c_z + gdn_vw
    c_hy = c_o + ml_vw
    c_gates = c_hy + 3 * hy_w
    c_gsmall = c_gates + 3 * d
    c_msmall = c_gsmall + 4 * LANES
    n_p = c_msmall + 4 * LANES

    def small(off, heads):
        return [jnp.pad(w_in[:, :, off + dd * heads:off + (dd + 1) * heads], ((0, 0), (0, 0), (0, LANES - heads)))
                for dd in range(2)]

    gb, ga = small(o_gbeta, gdn_heads), small(o_ga, gdn_heads)
    mi, mf = small(o_mi, ml_heads), small(o_mf, ml_heads)
    w_small = jnp.concatenate([gb[0], ga[0], gb[1], ga[1], mi[0], mf[0], mi[1], mf[1]], axis=-1)
    moves = [(w_in, o_gqkv, c_gqkv, c_mqkv - c_gqkv), (w_in, o_mqkv, c_mqkv, c_z - c_mqkv),
             (w_in, o_z, c_z, c_gsmall - c_z), (w_small, 0, c_gsmall, n_p - c_gsmall)]
    tcr = 512 if all(dst % 512 == 0 and width % 512 == 0 for _, _, dst, width in moves) else 256
    w_in_p = None
    for src, src_col, dst_col, width in moves:
        w_in_p = _regroup_call(src, w_in_p, src_col, dst_col, width, n_p, tcr)

    f_hidden = ffn_w_gate.shape[-1]
    tf = 512
    f_pad = -(-f_hidden // tf) * tf
    wg = jnp.pad(ffn_w_gate.astype(BF16), ((0, 0), (0, 0), (0, 0), (0, f_pad - f_hidden)))
    wu = jnp.pad(ffn_w_up.astype(BF16), ((0, 0), (0, 0), (0, 0), (0, f_pad - f_hidden)))
    wd = jnp.pad(ffn_w_down.astype(BF16), ((0, 0), (0, 0), (0, f_pad - f_hidden), (0, 0)))
    wa, wb, wc, wo = (t.astype(BF16) for t in (w_a_out, w_b_out, w_c_out, w_out))

    tm_big = _pick_tile(m, (768, 512, 384, 256, 128, 64))
    tm_mid = _pick_tile(m, (256, 128, 64))
    tm_proj = _pick_tile(m, (1056, 768, 512, 384, 256, 128, 64))
    tm_seg = _pick_tile(math.gcd(ctx_len, seq), (256, 128, 64))
    tn_in = _pick_tile(n_p, (1024, 512, 256, 128))
    nc_ctx = ctx_len // CHUNK

    cond = jnp.pad(jnp.concatenate([c_ctx[None, :], c], axis=0), ((0, SUBLANES - 2), (0, 0)))
    mods = _modulation_call(cond, w_ada, b_ada)

    dft_consts = _dft_constants(seq)
    dense_fwd, dense_inv = _dense_dft_constants(ctx_len)
    feats_lat = _filter_features(seq)
    feats_lat = feats_lat.reshape(seq // DFT_N2, DFT_N2, LANES).swapaxes(0, 1).reshape(seq, LANES)
    feats_ctx = _filter_features(ctx_len)
    tl = _pick_tile(seq, (512, 256, 128, 64))

    s = jnp.concatenate([x[0], ctx[0]], axis=0)
    for l in range(depth):
        last = l == depth - 1

        def mod(idx):
            return [mods[l, 0, idx * d:(idx + 1) * d], mods[l, 1, idx * d:(idx + 1) * d]]

        s = _ffn_call(s, _pack_rows(mod(0) + mod(1) + mod(2), d), norm_g[l, 0][None, :], wg, wu, wd, l, 0,
                      seq, tm_big, tf)
        p = _inproj_call(s, _pack_rows(mod(3) + mod(4), d), norm_g[l, 1][None, :], w_in_p[l], seq, tm_proj, tn_in)

        gqkv = _dwconv_call(p, c_gqkv, 2 * gdn_qkw + gdn_vw, jnp.transpose(gdn_conv[l]),
                            jnp.zeros((1, 2 * gdn_qkw + gdn_vw), F32), seq, tm_seg, 1024, True)
        hc = _dwconv_call(p, c_hy, 3 * hy_w, jnp.transpose(hy_short_w[l]), hy_short_b[l][None, :],
                          seq, tm_seg, 1024, False)

        og, om = _fused_scans_call(
            m // CHUNK,
            _gdn_scan_parts(gqkv, p, c_gsmall // LANES, _lane_row(gdn_a_log[l]), _lane_row(gdn_dt_bias[l]),
                            gdn_heads, gdn_dk, gdn_dv, nc_ctx),
            _mlstm_scan_parts(p, c_mqkv, c_msmall, _lane_row(mlstm_i_bias[l]), _lane_row(mlstm_f_bias[l]),
                              ml_heads, ml_dqk, ml_dv, nc_ctx))

        fargs = (jnp.pad(hf_w1[l], ((0, LANES - hf_w1.shape[1]), (0, 0))), hf_b1[l][None, :], hf_w2[l],
                 hf_b2[l][None, :], hf_w3[l], hf_freq[l], jnp.tile(hf_decay[l], 2)[None, :])
        taps = _filter_call(feats_lat, *fargs, tl, 1024)
        m1, g2, minv, pinv = dft_consts
        kh = seq // DFT_N2
        ta = _dft1_call(taps.reshape(DFT_N2, kh, taps.shape[1]), 0, taps.shape[1], m1, 1024, n_lo_major=True)
        kf = _filter_spectrum_call(ta.reshape(ta.shape[0] * DFT_N2, taps.shape[1]), g2, 1024)
        hl3 = hc.reshape(m // DFT_N2, DFT_N2, 3 * hy_w)
        tc_h = min(hy_w, 1024)
        per = hy_w // tc_h
        z1 = _long_conv(hl3, 0, False, hl3, per, hy_bias[l, 0][None, :], kf, 0, dft_consts, hy_w, tc_h, True)
        z2 = _long_conv(z1, 0, True, hl3, 2 * per, hy_bias[l, 1][None, :], kf, per, dft_consts, hy_w, tc_h, False)
        z2 = z2.reshape(seq, hy_w)
        if not last:
            taps_c = _filter_call(feats_ctx, *fargs, _pick_tile(ctx_len, (256, 128, 64)), 1024)
            kf_c = _dense_spectrum_call(taps_c, dense_fwd, 512)
            zc = _dense_hyena_call(hc[seq:], ctx_len, hy_w, kf_c, dense_fwd, dense_inv, hy_bias[l, 0][None, :],
                                   hy_bias[l, 1][None, :], 256)
        else:
            zc = jnp.zeros((ctx_len, hy_w), F32)
        hy = jnp.concatenate([z2, zc], axis=0)

        merged = _merge_call(og, om, p, c_z, c_o, hy, c_gates, jnp.tile(gdn_norm[l], gdn_heads)[None, :],
                             mlstm_norm[l][None, :], wa, wb, wc, l, (gdn_heads, gdn_dv, ml_heads, ml_dv), tm_mid)
        s = _outproj_call(merged, s, _pack_rows(mod(5), d), wo, l, seq, tm_mid)
        s = _ffn_call(s, _pack_rows(mod(6) + mod(7) + mod(8), d), norm_g[l, 2][None, :], wg, wu, wd, l, 1,
                      seq, tm_big, tf)

    out = _final_norm_call(s, final_norm[None, :], 0, seq, tm_seg)
    return out[None]
```

```python
import functools
import math

import jax
import jax.numpy as jnp
import numpy as np
from jax import lax
from jax.experimental import pallas as pl
from jax.experimental.pallas import tpu as pltpu

F32 = jnp.float32
BF16 = jnp.bfloat16

N_MOD = 9
RMS_EPS = 1e-6
L2_EPS = 1e-6
CHUNK = 64
GDN_CONV = 5
HYENA_SHORT = 3
HYENA_BANDS = 16
LANES = 128
SUBLANES = 8
DFT_N2 = 128
VMEM_LIMIT = 56 * 1024 * 1024


def _cparams(n_axes):
    return pltpu.CompilerParams(dimension_semantics=("arbitrary",) * n_axes,
                                vmem_limit_bytes=VMEM_LIMIT)


def _sigmoid(x):
    return jax.nn.sigmoid(x)


def _silu(x):
    return x * jax.nn.sigmoid(x)


def _softplus(x):
    return jnp.maximum(x, 0.0) + jnp.log(1.0 + jnp.exp(-jnp.abs(x)))


def _dot(a, b):
    return jnp.dot(a.astype(BF16), b.astype(BF16), preferred_element_type=F32)


def _dot_nt(a, b):
    return lax.dot_general(a.astype(BF16), b.astype(BF16), (((1,), (1,)), ((), ())),
                           preferred_element_type=F32)


def _dot_tn(a, b):
    return lax.dot_general(a.astype(BF16), b.astype(BF16), (((0,), (0,)), ((), ())),
                           preferred_element_type=F32)


def _dot_exact(a, b):
    return jnp.dot(a, b, preferred_element_type=F32, precision=lax.Precision.HIGHEST)


def _row_is_ctx(row0, rows, lat_len):
    r = row0 + lax.broadcasted_iota(jnp.int32, (rows, 1), 0)
    return r >= lat_len


def _modnorm(x, g, shift, scale):
    y = x * lax.rsqrt(jnp.mean(x * x, axis=-1, keepdims=True) + RMS_EPS) * g
    return y * (1.0 + scale) + shift


ROW_CHUNK = 128


def _for_row_chunks(tm, fn):
    rc = math.gcd(tm, ROW_CHUNK)

    def body(c, carry):
        fn(pl.multiple_of(c * rc, rc), rc)
        return carry

    lax.fori_loop(0, tm // rc, body, 0)


def _store_modulated(x_ref, m_ref, g_ref, h_sc, row0, tm, lat_len):
    def chunk(r0, rc):
        is_ctx = _row_is_ctx(row0 + r0, rc, lat_len)
        shift = jnp.where(is_ctx, m_ref[0:1, :], m_ref[1:2, :])
        scale = jnp.where(is_ctx, m_ref[2:3, :], m_ref[3:4, :])
        h_sc[pl.ds(r0, rc), :] = _modnorm(x_ref[pl.ds(r0, rc), :], g_ref[...], shift, scale).astype(h_sc.dtype)

    _for_row_chunks(tm, chunk)


def _mod_kernel(c_ref, w_ref, b_ref, o_ref):
    o_ref[0] = _dot(_silu(c_ref[...]), w_ref[0]) + b_ref[0]


def _modulation_call(cond, w_ada, b_ada):
    depth, d, n = w_ada.shape
    tn = _pick_tile(n, (1024, 512, 256, 128))
    return pl.pallas_call(
        _mod_kernel,
        grid=(depth, n // tn),
        in_specs=[pl.BlockSpec((SUBLANES, d), lambda l, j: (0, 0)),
                  pl.BlockSpec((1, d, tn), lambda l, j: (l, 0, j)),
                  pl.BlockSpec((1, 1, tn), lambda l, j: (l, 0, j))],
        out_specs=pl.BlockSpec((1, SUBLANES, tn), lambda l, j: (l, 0, j)),
        out_shape=jax.ShapeDtypeStruct((depth, SUBLANES, n), F32),
        compiler_params=_cparams(2),
        name="adaln_modulation",
    )(cond, w_ada, b_ada.reshape(depth, 1, n))


def _ffn_kernel(x_ref, m_ref, g_ref, wg_ref, wu_ref, wd_ref, o_ref, h_sc, acc_sc, *, tm, lat_len):
    row0 = pl.program_id(0) * tm
    f = pl.program_id(1)

    @pl.when(f == 0)
    def _():
        _store_modulated(x_ref, m_ref, g_ref, h_sc, row0, tm, lat_len)
        acc_sc[...] = jnp.zeros_like(acc_sc)

    h = h_sc[...]
    a = _silu(_dot(h, wg_ref[...])) * _dot(h, wu_ref[...])
    acc_sc[...] += _dot(a, wd_ref[...])

    @pl.when(f == pl.num_programs(1) - 1)
    def _():
        def chunk(r0, rc):
            gate = jnp.where(_row_is_ctx(row0 + r0, rc, lat_len), m_ref[4:5, :], m_ref[5:6, :])
            rows = pl.ds(r0, rc)
            o_ref[rows, :] = x_ref[rows, :] + 0.5 * gate * acc_sc[rows, :]

        _for_row_chunks(tm, chunk)


def _ffn_call(x, mods, g, wg, wu, wd, layer, which, lat_len, tm, tf):
    m, d = x.shape
    fp = wg.shape[-1]
    return pl.pallas_call(
        functools.partial(_ffn_kernel, tm=tm, lat_len=lat_len),
        grid=(m // tm, fp // tf),
        in_specs=[pl.BlockSpec((tm, d), lambda i, f: (i, 0)),
                  pl.BlockSpec((SUBLANES, d), lambda i, f: (0, 0)),
                  pl.BlockSpec((1, d), lambda i, f: (0, 0)),
                  pl.BlockSpec((None, None, d, tf), lambda i, f: (layer, which, 0, f)),
                  pl.BlockSpec((None, None, d, tf), lambda i, f: (layer, which, 0, f)),
                  pl.BlockSpec((None, None, tf, d), lambda i, f: (layer, which, f, 0))],
        out_specs=pl.BlockSpec((tm, d), lambda i, f: (i, 0)),
        out_shape=jax.ShapeDtypeStruct((m, d), F32),
        scratch_shapes=[pltpu.VMEM((tm, d), BF16), pltpu.VMEM((tm, d), F32)],
        compiler_params=_cparams(2),
        name="macaron_swiglu",
    )(x, mods, g, wg, wu, wd)


def _inproj_kernel(x_ref, m_ref, g_ref, w_ref, o_ref, h_sc, *, tm, lat_len):
    @pl.when(pl.program_id(1) == 0)
    def _():
        _store_modulated(x_ref, m_ref, g_ref, h_sc, pl.program_id(0) * tm, tm, lat_len)

    o_ref[...] = _dot(h_sc[...], w_ref[...])


def _inproj_call(x, mods, g, w, layer, lat_len, tm, tn):
    m, d = x.shape
    n = w.shape[-1]
    return pl.pallas_call(
        functools.partial(_inproj_kernel, tm=tm, lat_len=lat_len),
        grid=(m // tm, n // tn),
        in_specs=[pl.BlockSpec((tm, d), lambda i, j: (i, 0)),
                  pl.BlockSpec((SUBLANES, d), lambda i, j: (0, 0)),
                  pl.BlockSpec((1, d), lambda i, j: (0, 0)),
                  pl.BlockSpec((None, d, tn), lambda i, j: (layer, 0, j))],
        out_specs=pl.BlockSpec((tm, tn), lambda i, j: (i, j)),
        out_shape=jax.ShapeDtypeStruct((m, n), F32),
        scratch_shapes=[pltpu.VMEM((tm, d), BF16)],
        compiler_params=_cparams(2),
        name="mixer_in_proj",
    )(x, mods, g, w)


def _dwconv_kernel(prev_ref, cur_ref, next_ref, w_ref, b_ref, o_ref, ext_sc, *, tm, taps, split, m_rows, act):
    row0 = pl.program_id(0) * tm
    at_start = jnp.logical_or(row0 == 0, row0 == split)
    at_end = jnp.logical_or(row0 + tm == split, row0 + tm == m_rows)
    ext_sc[0:SUBLANES, :] = jnp.where(at_start, 0.0, prev_ref[...])
    ext_sc[SUBLANES:SUBLANES + tm, :] = cur_ref[...]
    ext_sc[SUBLANES + tm:2 * SUBLANES + tm, :] = jnp.where(at_end, 0.0, next_ref[...])
    acc = jnp.zeros(o_ref.shape, F32) + b_ref[...]
    for t in range(taps):
        acc = acc + ext_sc[pl.ds(SUBLANES + t - taps // 2, tm), :] * w_ref[t:t + 1, :]
    o_ref[...] = _silu(acc) if act else acc


def _dwconv_call(p, col0, ncols, w_t, b, split, tm, tc, act):
    m = p.shape[0]
    taps = w_t.shape[0]
    assert split % tm == 0 and m % tm == 0 and col0 % tc == 0 and taps // 2 <= SUBLANES
    cb0 = col0 // tc
    rb = tm // SUBLANES
    last = m // SUBLANES - 1
    return pl.pallas_call(
        functools.partial(_dwconv_kernel, tm=tm, taps=taps, split=split, m_rows=m, act=act),
        grid=(m // tm, ncols // tc),
        in_specs=[pl.BlockSpec((SUBLANES, tc), lambda i, j: (jnp.maximum(i * rb - 1, 0), cb0 + j)),
                  pl.BlockSpec((tm, tc), lambda i, j: (i, cb0 + j)),
                  pl.BlockSpec((SUBLANES, tc), lambda i, j: (jnp.minimum((i + 1) * rb, last), cb0 + j)),
                  pl.BlockSpec((taps, tc), lambda i, j: (0, j)),
                  pl.BlockSpec((1, tc), lambda i, j: (0, j))],
        out_specs=pl.BlockSpec((tm, tc), lambda i, j: (i, j)),
        out_shape=jax.ShapeDtypeStruct((m, ncols), F32),
        scratch_shapes=[pltpu.VMEM((tm + 2 * SUBLANES, tc), F32)],
        compiler_params=_cparams(2),
        name="depthwise_conv",
    )(p, p, p, w_t, b)


def _fwd_chunk(s, nc_ctx, nc_tot):
    return jnp.where(s < nc_ctx, nc_tot - nc_ctx + s, s - nc_ctx)


def _bwd_chunk(s, nc_ctx, nc_tot):
    del nc_ctx
    return nc_tot - 1 - s


def _causal_masks(direction):
    ii = lax.broadcasted_iota(jnp.int32, (CHUNK, CHUNK), 0)
    jj = lax.broadcasted_iota(jnp.int32, (CHUNK, CHUNK), 1)
    rel = ii - jj if direction == 0 else jj - ii
    return rel >= 0, rel > 0


def _unit_triangular_inverses(mats):
    ii = lax.broadcasted_iota(jnp.int32, (CHUNK, CHUNK), 0)
    jj = lax.broadcasted_iota(jnp.int32, (CHUNK, CHUNK), 1)
    eye = (ii == jj).astype(F32)

    def same_block(width):
        return (ii // width) == (jj // width)

    inner = same_block(8)
    d1 = [jnp.where(inner, a, 0.0) for a in mats]
    d2 = [_dot(x, x) for x in d1]
    yield
    d4 = [_dot(x, x) for x in d2]
    p = [_dot(eye - x, eye + y) for x, y in zip(d1, d2)]
    yield
    p = [_dot(x, eye + y) for x, y in zip(p, d4)]
    yield
    for width in (16, 32, 64):
        outer = same_block(width)
        ring = jnp.logical_and(outer, jnp.logical_not(inner))
        t = [_dot(x, jnp.where(ring, a, 0.0)) for x, a in zip(p, mats)]
        yield
        p = [x - _dot(y, x) for x, y in zip(p, t)]
        yield
        inner = outer
    return p


def _gdn_scan_kernel(qf_ref, kf_ref, vf_ref, btf_ref, af_ref, qb_ref, kb_ref, vb_ref, btb_ref, ab_ref,
                     alog_ref, dtb_ref, of_ref, ob_ref, s_sc, *, heads, dk, dv):
    @pl.when(pl.program_id(0) == 0)
    def _():
        s_sc[...] = jnp.zeros_like(s_sc)

    chains = []
    for direction, (q_ref, k_ref, v_ref, bt_ref, a_ref, o_ref) in enumerate(
            ((qf_ref, kf_ref, vf_ref, btf_ref, af_ref, of_ref), (qb_ref, kb_ref, vb_ref, btb_ref, ab_ref, ob_ref))):
        incl, strict = _causal_masks(direction)
        beta = _sigmoid(bt_ref[...])
        glog = -jnp.exp(alog_ref[direction]) * _softplus(a_ref[...] + dtb_ref[direction])
        gcum = _dot_exact(incl.astype(F32), glog)
        gtot = gcum[CHUNK - 1:CHUNK, :] if direction == 0 else gcum[0:1, :]
        e_in = jnp.exp(gcum)
        e_out = jnp.exp(gtot - gcum)
        g_end = jnp.exp(gtot)
        gcum_t = gcum.T
        for h in range(heads):
            q = q_ref[:, h * dk:(h + 1) * dk]
            k = k_ref[:, h * dk:(h + 1) * dk]
            q = q * lax.rsqrt(jnp.sum(q * q, axis=-1, keepdims=True) + L2_EPS) * (dk ** -0.5)
            k = k * lax.rsqrt(jnp.sum(k * k, axis=-1, keepdims=True) + L2_EPS)
            b_col = beta[:, h:h + 1]
            kb = k * b_col
            chains.append(dict(
                q=q, k=k, kb=kb, strict=strict, o_ref=o_ref, h=h, slot=direction * heads + h,
                decay=jnp.exp(jnp.where(incl, gcum[:, h:h + 1] - gcum_t[h:h + 1, :], -jnp.inf)),
                rhs=jnp.concatenate([v_ref[:, h * dv:(h + 1) * dv] * b_col, kb * e_in[:, h:h + 1]], axis=1),
                q_in=q * e_in[:, h:h + 1], k_out=k * e_out[:, h:h + 1], g_end=g_end[:, h:h + 1]))

    yield
    a_mats = [jnp.where(c["strict"], _dot_nt(c["kb"], c["k"]) * c["decay"], 0.0) for c in chains]
    qk = [_dot_nt(c["q"], c["k"]) * c["decay"] for c in chains]
    yield
    t_inv = yield from _unit_triangular_inverses(a_mats)
    sol = [_dot(t, c["rhs"]) for t, c in zip(t_inv, chains)]
    yield
    states = [s_sc[c["slot"]] for c in chains]
    v_new = [x[:, :dv] - _dot(x[:, dv:], st) for x, st in zip(sol, states)]
    o_inter = [_dot(c["q_in"], st) for c, st in zip(chains, states)]
    yield
    o_intra = [_dot(x, y) for x, y in zip(qk, v_new)]
    s_upd = [_dot_tn(c["k_out"], y) for c, y in zip(chains, v_new)]
    yield
    for c, st, x, y, z in zip(chains, states, o_inter, o_intra, s_upd):
        c["o_ref"][:, c["h"] * dv:(c["h"] + 1) * dv] = x + y
        s_sc[c["slot"]] = st * c["g_end"] + z


def _gdn_scan_parts(qkv, p, beta_cb, alog, dtb, heads, dk, dv, nc_ctx):
    m = qkv.shape[0]
    nc = m // CHUNK
    qw, vw = heads * dk, heads * dv
    assert qw == vw
    bidx = functools.partial(_bwd_chunk, nc_ctx=nc_ctx, nc_tot=nc)

    def specs(row):
        return [pl.BlockSpec((CHUNK, qw), lambda s: (row(s), 0)),
                pl.BlockSpec((CHUNK, qw), lambda s: (row(s), 1)),
                pl.BlockSpec((CHUNK, vw), lambda s: (row(s), 2))]

    def gate_specs(row, direction):
        return [pl.BlockSpec((CHUNK, LANES), lambda s: (row(s), beta_cb + 2 * direction)),
                pl.BlockSpec((CHUNK, LANES), lambda s: (row(s), beta_cb + 2 * direction + 1))]

    fwd = functools.partial(_fwd_chunk, nc_ctx=nc_ctx, nc_tot=nc)
    return dict(
        kernel=functools.partial(_gdn_scan_kernel, heads=heads, dk=dk, dv=dv),
        in_specs=(specs(fwd) + gate_specs(fwd, 0) + specs(bidx) + gate_specs(bidx, 1)
                  + [pl.BlockSpec((2, 1, LANES), lambda s: (0, 0, 0)),
                     pl.BlockSpec((2, 1, LANES), lambda s: (0, 0, 0))]),
        operands=(qkv, qkv, qkv, p, p, qkv, qkv, qkv, p, p, alog, dtb),
        out_specs=[pl.BlockSpec((CHUNK, vw), lambda s: (fwd(s), 0)),
                   pl.BlockSpec((CHUNK, vw), lambda s: (bidx(s), 0))],
        out_shape=[jax.ShapeDtypeStruct((m, vw), F32)] * 2,
        scratch_shapes=[pltpu.VMEM((2 * heads, dk, dv), F32)])


def _mlstm_scan_kernel(qf_ref, kf_ref, vf_ref, if_ref, ff_ref, qb_ref, kb_ref, vb_ref, ib_ref, fb_ref,
                       ibias_ref, fbias_ref, of_ref, ob_ref, c_sc, m_sc, *, heads, dqk, dv):
    @pl.when(pl.program_id(0) == 0)
    def _():
        c_sc[...] = jnp.zeros_like(c_sc)
        m_sc[...] = jnp.zeros_like(m_sc)

    ones_col = (lax.broadcasted_iota(jnp.int32, (CHUNK, LANES), 1) == 0).astype(F32)
    chains = []
    for direction, (q_ref, k_ref, v_ref, i_ref, f_ref, o_ref) in enumerate(
            ((qf_ref, kf_ref, vf_ref, if_ref, ff_ref, of_ref), (qb_ref, kb_ref, vb_ref, ib_ref, fb_ref, ob_ref))):
        incl, _ = _causal_masks(direction)
        log_i = i_ref[...] + ibias_ref[direction]
        log_f = -_softplus(-(f_ref[...] + fbias_ref[direction]))
        bcum = _dot_exact(incl.astype(F32), log_f)
        btot = bcum[CHUNK - 1:CHUNK, :] if direction == 0 else bcum[0:1, :]
        log_end = btot - bcum + log_i
        m_st = m_sc[direction]
        m_new = jnp.maximum(btot + m_st, jnp.max(log_end, axis=0, keepdims=True))
        m_sc[direction] = m_new
        carry = jnp.exp(btot + m_st - m_new)
        k_scale = jnp.exp(log_end - m_new)
        b_inter = bcum + m_st
        bcum_t = bcum.T
        log_i_t = log_i.T
        for h in range(heads):
            log_d = jnp.where(incl, bcum[:, h:h + 1] - bcum_t[h:h + 1, :] + log_i_t[h:h + 1, :], -jnp.inf)
            m_t = jnp.maximum(b_inter[:, h:h + 1], jnp.max(log_d, axis=-1, keepdims=True))
            chains.append(dict(
                q=q_ref[:, h * dqk:(h + 1) * dqk] * (dqk ** -0.5), k=k_ref[:, h * dqk:(h + 1) * dqk],
                v_ext=jnp.concatenate([v_ref[:, h * dv:(h + 1) * dv], ones_col], axis=1),
                p_intra=jnp.exp(log_d - m_t), w_inter=jnp.exp(b_inter[:, h:h + 1] - m_t), floor=jnp.exp(-m_t),
                k_scale=k_scale[:, h:h + 1], carry=carry[:, h:h + 1], o_ref=o_ref, h=h, slot=direction * heads + h))
        yield

    s = [_dot_nt(c["q"], c["k"]) * c["p_intra"] for c in chains]
    yield
    states = [c_sc[c["slot"]] for c in chains]
    inter = [_dot(c["q"], st) for c, st in zip(chains, states)]
    yield
    intra = [_dot(x, c["v_ext"]) for x, c in zip(s, chains)]
    yield
    upd = [_dot_tn(c["k"] * c["k_scale"], c["v_ext"]) for c in chains]
    yield
    for idx, (c, st, x, y, z) in enumerate(zip(chains, states, inter, intra, upd)):
        out = c["w_inter"] * x + y
        den = jnp.maximum(jnp.abs(out[:, dv:dv + 1]), c["floor"])
        c["o_ref"][:, c["h"] * dv:(c["h"] + 1) * dv] = out[:, :dv] / den
        c_sc[c["slot"]] = c["carry"] * st + z
        if idx % 2 == 1:
            yield


def _mlstm_scan_parts(p, q_cb, gate_cb, ib, fb, heads, dqk, dv, nc_ctx):
    m = p.shape[0]
    nc = m // CHUNK
    qw, vw = heads * dqk, heads * dv
    assert q_cb % qw == 0 and (q_cb + 2 * qw) % vw == 0
    bidx = functools.partial(_bwd_chunk, nc_ctx=nc_ctx, nc_tot=nc)
    fwd = functools.partial(_fwd_chunk, nc_ctx=nc_ctx, nc_tot=nc)
    qb, kb, vb = q_cb // qw, q_cb // qw + 1, (q_cb + 2 * qw) // vw
    gb = gate_cb // LANES

    def specs(row, direction):
        return [pl.BlockSpec((CHUNK, qw), lambda s: (row(s), qb)),
                pl.BlockSpec((CHUNK, qw), lambda s: (row(s), kb)),
                pl.BlockSpec((CHUNK, vw), lambda s: (row(s), vb)),
                pl.BlockSpec((CHUNK, LANES), lambda s: (row(s), gb + 2 * direction)),
                pl.BlockSpec((CHUNK, LANES), lambda s: (row(s), gb + 2 * direction + 1))]

    return dict(
        kernel=functools.partial(_mlstm_scan_kernel, heads=heads, dqk=dqk, dv=dv),
        in_specs=(specs(fwd, 0) + specs(bidx, 1)
                  + [pl.BlockSpec((2, 1, LANES), lambda s: (0, 0, 0)),
                     pl.BlockSpec((2, 1, LANES), lambda s: (0, 0, 0))]),
        operands=(p, p, p, p, p, p, p, p, p, p, ib, fb),
        out_specs=[pl.BlockSpec((CHUNK, vw), lambda s: (fwd(s), 0)),
                   pl.BlockSpec((CHUNK, vw), lambda s: (bidx(s), 0))],
        out_shape=[jax.ShapeDtypeStruct((m, vw), F32)] * 2,
        scratch_shapes=[pltpu.VMEM((2 * heads, dqk, dv + LANES), F32), pltpu.VMEM((2, 1, LANES), F32)])


def _fused_scans_kernel(*refs, parts):
    n_in = sum(part[1] for part in parts)
    n_out = sum(part[2] for part in parts)
    i0, o0, s0 = 0, n_in, n_in + n_out
    running = []
    for fn, ni, no, ns in parts:
        running.append(fn(*refs[i0:i0 + ni], *refs[o0:o0 + no], *refs[s0:s0 + ns]))
        i0, o0, s0 = i0 + ni, o0 + no, s0 + ns
    done = object()
    while running:
        running = [g for g in running if next(g, done) is not done]


def _fused_scans_call(nc, *scans):
    parts = tuple((sc["kernel"], len(sc["in_specs"]), len(sc["out_specs"]), len(sc["scratch_shapes"]))
                  for sc in scans)
    outs = pl.pallas_call(
        functools.partial(_fused_scans_kernel, parts=parts),
        grid=(nc,),
        in_specs=sum((list(sc["in_specs"]) for sc in scans), []),
        out_specs=sum((list(sc["out_specs"]) for sc in scans), []),
        out_shape=sum((list(sc["out_shape"]) for sc in scans), []),
        scratch_shapes=sum((list(sc["scratch_shapes"]) for sc in scans), []),
        compiler_params=_cparams(1),
        name="gdn_mlstm_scans",
    )(*sum((list(sc["operands"]) for sc in scans), []))
    split, res = 0, []
    for sc in scans:
        res.append(tuple(outs[split:split + len(sc["out_specs"])]))
        split += len(sc["out_specs"])
    return res


def _filter_kernel(feat_ref, w1_ref, b1_ref, w2_ref, b2_ref, w3_ref, freq_ref, dec_ref, o_ref, h_sc):
    @pl.when(pl.program_id(1) == 0)
    def _():
        h = jnp.sin(freq_ref[0:1, :] * (_dot(feat_ref[...], w1_ref[...]) + b1_ref[...]))
        h_sc[...] = jnp.sin(freq_ref[1:2, :] * (_dot(h, w2_ref[...]) + b2_ref[...]))

    o_ref[...] = _dot(h_sc[...], w3_ref[...]) * jnp.exp(-feat_ref[:, 0:1] * jnp.abs(dec_ref[...]))


def _filter_call(feats, w1, b1, w2, b2, w3, freq, dec, tl, tn):
    length, fp = feats.shape
    hid = w2.shape[0]
    n = w3.shape[1]
    return pl.pallas_call(
        _filter_kernel,
        grid=(length // tl, n // tn),
        in_specs=[pl.BlockSpec((tl, fp), lambda i, j: (i, 0)),
                  pl.BlockSpec((fp, hid), lambda i, j: (0, 0)),
                  pl.BlockSpec((1, hid), lambda i, j: (0, 0)),
                  pl.BlockSpec((hid, hid), lambda i, j: (0, 0)),
                  pl.BlockSpec((1, hid), lambda i, j: (0, 0)),
                  pl.BlockSpec((hid, tn), lambda i, j: (0, j)),
                  pl.BlockSpec((2, hid), lambda i, j: (0, 0)),
                  pl.BlockSpec((1, tn), lambda i, j: (0, j))],
        out_specs=pl.BlockSpec((tl, tn), lambda i, j: (i, j)),
        out_shape=jax.ShapeDtypeStruct((length, n), F32),
        scratch_shapes=[pltpu.VMEM((tl, hid), F32)],
        compiler_params=_cparams(2),
        name="hyena_filter",
    )(feats, w1, b1, w2, b2, w3, freq, dec)


def _dft_constants(length):
    n2 = DFT_N2
    n = 2 * length
    n1 = n // n2
    kh = n1 // 2
    na = kh + 1

    def cis(num, den):
        ang = (2.0 * math.pi / den) * (num % den).astype(F32)
        return jnp.cos(ang), jnp.sin(ang)

    ar = lambda size: jnp.arange(size, dtype=jnp.int32)
    c, s = cis(ar(na)[None, :, None] * (n2 * ar(kh)[None, None, :] + ar(n2)[:, None, None]), n)
    m1 = jnp.stack([c, -s], axis=2).reshape(n2, 2 * na, kh)
    c, s = cis(ar(n2)[:, None] * ar(n2)[None, :], n2)
    g2 = jnp.block([[c, s], [-s, c]])
    c, s = cis(ar(n2)[None, :, None] * (ar(na)[:, None, None] + n1 * ar(n2)[None, None, :]), n)
    minv = jnp.concatenate([jnp.concatenate([c, -s], axis=2), jnp.concatenate([s, c], axis=2)], axis=1)
    c, s = cis(ar(kh)[:, None] * ar(na)[None, :], n1)
    weight = jnp.where(jnp.logical_or(ar(na) == 0, ar(na) == kh), 1.0, 2.0)[None, :] / n
    pinv = jnp.stack([c * weight, -s * weight], axis=2).reshape(kh, 2 * na)
    return m1.astype(BF16), g2.astype(BF16), minv.astype(BF16), pinv.astype(BF16)


def _dft1_kernel(x_ref, m_ref, o_ref, *, n_lo_major):
    for j in range(SUBLANES):
        o_ref[:, j, :] = _dot(m_ref[j], x_ref[j] if n_lo_major else x_ref[:, j, :])


def _dft1_call(x3, col_blk0, width, m1, tc, n_lo_major=False):
    n2, rows, kh = m1.shape
    if n_lo_major:
        x_spec = pl.BlockSpec((SUBLANES, kh, tc), lambda c, g: (g, 0, col_blk0 + c))
    else:
        x_spec = pl.BlockSpec((kh, SUBLANES, tc), lambda c, g: (0, g, col_blk0 + c))
    return pl.pallas_call(
        functools.partial(_dft1_kernel, n_lo_major=n_lo_major),
        grid=(width // tc, n2 // SUBLANES),
        in_specs=[x_spec,
                  pl.BlockSpec((SUBLANES, rows, kh), lambda c, g: (g, 0, 0))],
        out_specs=pl.BlockSpec((rows, SUBLANES, tc), lambda c, g: (0, g, c)),
        out_shape=jax.ShapeDtypeStruct((rows, n2, width), F32),
        compiler_params=_cparams(2),
        name="dft_stage1",
    )(x3, m1)


def _filter_spectrum_kernel(af_ref, ab_ref, g_ref, o_ref):
    sf = _dot(g_ref[...], af_ref[...])
    sb = _dot(g_ref[...], ab_ref[...])
    sign = jnp.where(lax.broadcasted_iota(jnp.int32, (2 * DFT_N2, 1), 0) < DFT_N2, 1.0, -1.0)
    o_ref[...] = sf + sign * sb


def _filter_spectrum_call(a, g2, tc):
    rows, width = a.shape
    half = width // 2
    blk = 2 * DFT_N2
    return pl.pallas_call(
        _filter_spectrum_kernel,
        grid=(rows // blk, half // tc),
        in_specs=[pl.BlockSpec((blk, tc), lambda a_, j: (a_, j)),
                  pl.BlockSpec((blk, tc), lambda a_, j: (a_, half // tc + j)),
                  pl.BlockSpec((blk, blk), lambda a_, j: (0, 0))],
        out_specs=pl.BlockSpec((blk, tc), lambda a_, j: (a_, j)),
        out_shape=jax.ShapeDtypeStruct((rows, half), F32),
        compiler_params=_cparams(2),
        name="hyena_filter_spectrum",
    )(a, a, g2)


def _complex_mul(x, kf, n):
    xr, xi = x[:n], x[n:]
    kr, ki = kf[:n], kf[n:]
    return jnp.concatenate([xr * kr - xi * ki, xr * ki + xi * kr], axis=0)


def _spectral_mid_kernel(a_ref, g_ref, kf_ref, minv_ref, o_ref):
    x = _dot(g_ref[...], a_ref[...])
    o_ref[...] = _dot(minv_ref[0], _complex_mul(x, kf_ref[...], DFT_N2))


def _spectral_mid_call(a, g2, kf, kf_cb, minv, tc):
    rows, width = a.shape
    blk = 2 * DFT_N2
    return pl.pallas_call(
        _spectral_mid_kernel,
        grid=(rows // blk, width // tc),
        in_specs=[pl.BlockSpec((blk, tc), lambda a_, j: (a_, j)),
                  pl.BlockSpec((blk, blk), lambda a_, j: (0, 0)),
                  pl.BlockSpec((blk, tc), lambda a_, j: (a_, kf_cb + j)),
                  pl.BlockSpec((1, blk, blk), lambda a_, j: (a_, 0, 0))],
        out_specs=pl.BlockSpec((blk, tc), lambda a_, j: (a_, j)),
        out_shape=jax.ShapeDtypeStruct((rows, width), F32),
        compiler_params=_cparams(2),
        name="hyena_spectral_mid",
    )(a, g2, kf, minv)


def _idft2_kernel(z_ref, p_ref, v_ref, gate_ref, bias_ref, o_ref, *, v_major, out_major):
    for j in range(SUBLANES):
        y = _dot(p_ref[...], z_ref[:, j, :])
        v = v_ref[j] if v_major else v_ref[:, j, :]
        res = gate_ref[:, j, :] * (y + v * bias_ref[...])
        if out_major:
            o_ref[j] = res
        else:
            o_ref[:, j, :] = res


def _idft2_call(z3, pinv, v3, v_cb, v_major, gate3, gate_cb, bias, width, tc, out_major):
    rows, n2, _ = z3.shape
    kh = pinv.shape[0]

    def time_spec(major, cb):
        if major:
            return pl.BlockSpec((SUBLANES, kh, tc), lambda c, g: (g, 0, cb + c))
        return pl.BlockSpec((kh, SUBLANES, tc), lambda c, g: (0, g, cb + c))

    return pl.pallas_call(
        functools.partial(_idft2_kernel, v_major=v_major, out_major=out_major),
        grid=(width // tc, n2 // SUBLANES),
        in_specs=[pl.BlockSpec((rows, SUBLANES, tc), lambda c, g: (0, g, c)),
                  pl.BlockSpec((kh, rows), lambda c, g: (0, 0)),
                  time_spec(v_major, v_cb),
                  time_spec(False, gate_cb),
                  pl.BlockSpec((1, tc), lambda c, g: (0, c))],
        out_specs=time_spec(out_major, 0),
        out_shape=jax.ShapeDtypeStruct((n2, kh, width) if out_major else (kh, n2, width), F32),
        compiler_params=_cparams(2),
        name="idft_stage2_gate",
    )(z3, pinv, v3, gate3, bias)


def _long_conv(v3, v_cb, v_major, gate3, gate_cb, bias, kf, kf_cb, consts, width, tc, out_major):
    m1, g2, minv, pinv = consts
    a = _dft1_call(v3, v_cb, width, m1, tc, n_lo_major=v_major)
    rows = a.shape[0]
    z = _spectral_mid_call(a.reshape(rows * DFT_N2, width), g2, kf, kf_cb, minv, tc)
    return _idft2_call(z.reshape(rows, DFT_N2, width), pinv, v3, v_cb, v_major, gate3, gate_cb, bias, width, tc,
                       out_major)


def _dense_dft_constants(length):
    n = 2 * length
    k = jnp.arange(n, dtype=jnp.int32)[:, None]
    t = jnp.arange(length, dtype=jnp.int32)[None, :]
    ang = (2.0 * math.pi / n) * ((k * t) % n).astype(F32)
    c, s = jnp.cos(ang), jnp.sin(ang)
    fwd = jnp.concatenate([c, -s], axis=0)
    inv = jnp.concatenate([c.T, -s.T], axis=1) / n
    return fwd.astype(BF16), inv.astype(BF16)


def _dense_spectrum_kernel(tf_ref, tb_ref, f_ref, o_ref, *, n):
    sign = jnp.where(lax.broadcasted_iota(jnp.int32, (2 * n, 1), 0) < n, 1.0, -1.0)
    o_ref[...] = _dot(f_ref[...], tf_ref[...]) + sign * _dot(f_ref[...], tb_ref[...])


def _dense_spectrum_call(taps, fwd, tc):
    length, width = taps.shape
    half = width // 2
    rows = fwd.shape[0]
    return pl.pallas_call(
        functools.partial(_dense_spectrum_kernel, n=rows // 2),
        grid=(half // tc,),
        in_specs=[pl.BlockSpec((length, tc), lambda j: (0, j)),
                  pl.BlockSpec((length, tc), lambda j: (0, half // tc + j)),
                  pl.BlockSpec((rows, length), lambda j: (0, 0))],
        out_specs=pl.BlockSpec((rows, tc), lambda j: (0, j)),
        out_shape=jax.ShapeDtypeStruct((rows, half), F32),
        compiler_params=_cparams(1),
        name="hyena_filter_spectrum_dense",
    )(taps, taps, fwd)


def _dense_hyena_kernel(v_ref, x1_ref, x2_ref, kf0_ref, kf1_ref, f_ref, inv_ref, b0_ref, b1_ref, o_ref, *, n):
    def conv(u, kf_ref, b_ref):
        y = _dot(inv_ref[...], _complex_mul(_dot(f_ref[...], u), kf_ref[...], n))
        return y + u * b_ref[...]

    z = x1_ref[...] * conv(v_ref[...], kf0_ref, b0_ref)
    o_ref[...] = x2_ref[...] * conv(z, kf1_ref, b1_ref)


def _dense_hyena_call(hc, length, width, kf, fwd, inv, bias0, bias1, tc):
    rows = fwd.shape[0]
    per = width // tc
    return pl.pallas_call(
        functools.partial(_dense_hyena_kernel, n=rows // 2),
        grid=(per,),
        in_specs=[pl.BlockSpec((length, tc), lambda j: (0, j)),
                  pl.BlockSpec((length, tc), lambda j: (0, per + j)),
                  pl.BlockSpec((length, tc), lambda j: (0, 2 * per + j)),
                  pl.BlockSpec((rows, tc), lambda j: (0, j)),
                  pl.BlockSpec((rows, tc), lambda j: (0, per + j)),
                  pl.BlockSpec((rows, length), lambda j: (0, 0)),
                  pl.BlockSpec((length, rows), lambda j: (0, 0)),
                  pl.BlockSpec((1, tc), lambda j: (0, j)),
                  pl.BlockSpec((1, tc), lambda j: (0, j))],
        out_specs=pl.BlockSpec((length, tc), lambda j: (0, j)),
        out_shape=jax.ShapeDtypeStruct((length, width), F32),
        compiler_params=_cparams(1),
        name="hyena_dense",
    )(hc, hc, hc, kf, kf, fwd, inv, bias0, bias1)


def _filter_features(length):
    t = jnp.linspace(0.0, 1.0, length, dtype=F32)[:, None]
    w = (2.0 * math.pi / length) * jnp.arange(length, dtype=F32)[:, None]
    bands = jnp.linspace(1e-4, HYENA_BANDS - 1, HYENA_BANDS, dtype=F32)[None, :]
    feats = jnp.concatenate([t, jnp.cos(bands * w), -jnp.sin(bands * w)], axis=-1)
    return jnp.pad(feats, ((0, 0), (0, LANES - feats.shape[1])))


def _head_rms(x, heads, width):
    outs = []
    for h in range(heads):
        xh = x[:, h * width:(h + 1) * width]
        outs.append(xh * lax.rsqrt(jnp.mean(xh * xh, axis=-1, keepdims=True) + RMS_EPS))
    return jnp.concatenate(outs, axis=1)


def _merge_kernel(ogf_ref, ogb_ref, omf_ref, omb_ref, z_ref, o_ref, hy_ref, g0_ref, g1_ref, g2_ref, gn_ref, mn_ref,
                  wa_ref, wb_ref, wc_ref, out_ref, *, gdn_heads, gdn_dv, ml_heads, ml_dv):
    a = _head_rms(ogf_ref[...] + ogb_ref[...], gdn_heads, gdn_dv) * gn_ref[...] * _silu(z_ref[...])
    b = _head_rms(omf_ref[...] + omb_ref[...], ml_heads, ml_dv) * mn_ref[...] * _sigmoid(o_ref[...])
    out_ref[...] = (_sigmoid(g0_ref[...]) * _dot(a, wa_ref[...]) + _sigmoid(g1_ref[...]) * _dot(b, wb_ref[...])
                    + _sigmoid(g2_ref[...]) * _dot(hy_ref[...], wc_ref[...]))


def _merge_call(og, om, p, z_cb, o_cb, hy, gate_cb, gn, mn, wa, wb, wc, layer, heads, tm):
    m, d = p.shape[0], wa.shape[-1]
    gw, mw, hw = og[0].shape[-1], om[0].shape[-1], hy.shape[-1]
    gdn_heads, gdn_dv, ml_heads, ml_dv = heads
    zb, ob, gb = z_cb // gw, o_cb // mw, gate_cb // d
    return pl.pallas_call(
        functools.partial(_merge_kernel, gdn_heads=gdn_heads, gdn_dv=gdn_dv, ml_heads=ml_heads, ml_dv=ml_dv),
        grid=(m // tm,),
        in_specs=[pl.BlockSpec((tm, gw), lambda i: (i, 0)),
                  pl.BlockSpec((tm, gw), lambda i: (i, 0)),
                  pl.BlockSpec((tm, mw), lambda i: (i, 0)),
                  pl.BlockSpec((tm, mw), lambda i: (i, 0)),
                  pl.BlockSpec((tm, gw), lambda i: (i, zb)),
                  pl.BlockSpec((tm, mw), lambda i: (i, ob)),
                  pl.BlockSpec((tm, hw), lambda i: (i, 0)),
                  pl.BlockSpec((tm, d), lambda i: (i, gb)),
                  pl.BlockSpec((tm, d), lambda i: (i, gb + 1)),
                  pl.BlockSpec((tm, d), lambda i: (i, gb + 2)),
                  pl.BlockSpec((1, gw), lambda i: (0, 0)),
                  pl.BlockSpec((1, mw), lambda i: (0, 0)),
                  pl.BlockSpec((None, gw, d), lambda i: (layer, 0, 0), pipeline_mode=pl.Buffered(1)),
                  pl.BlockSpec((None, mw, d), lambda i: (layer, 0, 0), pipeline_mode=pl.Buffered(1)),
                  pl.BlockSpec((None, hw, d), lambda i: (layer, 0, 0), pipeline_mode=pl.Buffered(1))],
        out_specs=pl.BlockSpec((tm, d), lambda i: (i, 0)),
        out_shape=jax.ShapeDtypeStruct((m, d), F32),
        compiler_params=_cparams(1),
        name="branch_merge",
    )(og[0], og[1], om[0], om[1], p, p, hy, p, p, p, gn, mn, wa, wb, wc)


def _outproj_kernel(y_ref, x_ref, m_ref, w_ref, o_ref, *, tm, lat_len):
    gate = jnp.where(_row_is_ctx(pl.program_id(0) * tm, tm, lat_len), m_ref[0:1, :], m_ref[1:2, :])
    o_ref[...] = x_ref[...] + gate * _dot(y_ref[...], w_ref[...])


def _outproj_call(y, x, mods, w, layer, lat_len, tm):
    m, d = x.shape
    return pl.pallas_call(
        functools.partial(_outproj_kernel, tm=tm, lat_len=lat_len),
        grid=(m // tm,),
        in_specs=[pl.BlockSpec((tm, d), lambda i: (i, 0)),
                  pl.BlockSpec((tm, d), lambda i: (i, 0)),
                  pl.BlockSpec((SUBLANES, d), lambda i: (0, 0)),
                  pl.BlockSpec((None, d, d), lambda i: (layer, 0, 0))],
        out_specs=pl.BlockSpec((tm, d), lambda i: (i, 0)),
        out_shape=jax.ShapeDtypeStruct((m, d), F32),
        compiler_params=_cparams(1),
        name="mixer_out_proj",
    )(y, x, mods, w)


def _final_norm_kernel(x_ref, g_ref, o_ref):
    x = x_ref[...]
    o_ref[...] = x * lax.rsqrt(jnp.mean(x * x, axis=-1, keepdims=True) + RMS_EPS) * g_ref[...]


def _final_norm_call(x, g, row0, rows, tm):
    d = x.shape[1]
    return pl.pallas_call(
        _final_norm_kernel,
        grid=(rows // tm,),
        in_specs=[pl.BlockSpec((tm, d), lambda i: (row0 // tm + i, 0)),
                  pl.BlockSpec((1, d), lambda i: (0, 0))],
        out_specs=pl.BlockSpec((tm, d), lambda i: (i, 0)),
        out_shape=jax.ShapeDtypeStruct((rows, d), F32),
        compiler_params=_cparams(1),
        name="final_rms_norm",
    )(x, g)


def _regroup_kernel(*refs, shift, width, n_blk, src_col, src_rows, src_cols, masked):
    o_ref = refs[-1]
    val = jnp.concatenate([r[...] for r in refs[:n_blk]], axis=1)[:, shift:shift + width]
    if masked:
        row = pl.program_id(1) * val.shape[0] + lax.broadcasted_iota(jnp.int32, val.shape, 0)
        col = src_col + pl.program_id(2) * width + lax.broadcasted_iota(jnp.int32, val.shape, 1)
        val = jnp.where(jnp.logical_and(row < src_rows, col < src_cols), val, 0.0)
    o_ref[...] = val.astype(o_ref.dtype)


def _regroup_call(src, buf, src_col, dst_col, ncols, out_rows, n_total, tcr):
    depth, src_rows, src_cols = src.shape
    q, shift = divmod(src_col, LANES)
    per = tcr // LANES
    n_blk = per + 1
    last = pl.cdiv(src_cols, LANES) - 1
    tb = _pick_tile(out_rows, (1024, 512, 256, 128))
    in_specs = [pl.BlockSpec((None, tb, LANES), lambda l, i, t, kk=kk: (l, i, jnp.minimum(q + per * t + kk, last)))
                for kk in range(n_blk)]
    operands = [src] * n_blk
    aliases = {}
    if buf is not None:
        in_specs.append(pl.BlockSpec(memory_space=pl.ANY))
        operands.append(buf)
        aliases = {n_blk: 0}
    return pl.pallas_call(
        functools.partial(_regroup_kernel, shift=shift, width=tcr, n_blk=n_blk, src_col=src_col, src_rows=src_rows,
                          src_cols=src_cols, masked=out_rows > src_rows or src_col + ncols > src_cols),
        grid=(depth, out_rows // tb, ncols // tcr),
        in_specs=in_specs,
        out_specs=pl.BlockSpec((None, tb, tcr), lambda l, i, t: (l, i, dst_col // tcr + t)),
        out_shape=jax.ShapeDtypeStruct((depth, out_rows, n_total), BF16),
        input_output_aliases=aliases,
        compiler_params=_cparams(3),
        name="weight_regroup",
    )(*operands)


def _gate_groups_kernel(*refs, lanes, heads):
    o_ref = refs[-1]
    for g, (lane, h) in enumerate(zip(lanes, heads)):
        piece = refs[g][:, lane:lane + h]
        tile = jnp.concatenate([piece, jnp.zeros((piece.shape[0], LANES - h), piece.dtype)], axis=1)
        o_ref[:, g * LANES:(g + 1) * LANES] = tile.astype(o_ref.dtype)


def _gate_groups_call(src, buf, cols, heads, dst_col):
    depth, d, _ = src.shape
    n_total = buf.shape[-1]
    width = len(cols) * LANES
    assert dst_col % width == 0 and all(c // LANES == (c + h - 1) // LANES for c, h in zip(cols, heads))
    tb = _pick_tile(d, (1024, 512, 256, 128))
    in_specs = [pl.BlockSpec((None, tb, LANES), lambda l, i, blk=c // LANES: (l, i, blk)) for c in cols]
    in_specs.append(pl.BlockSpec(memory_space=pl.ANY))
    return pl.pallas_call(
        functools.partial(_gate_groups_kernel, lanes=tuple(c % LANES for c in cols), heads=tuple(heads)),
        grid=(depth, d // tb),
        in_specs=in_specs,
        out_specs=pl.BlockSpec((None, tb, width), lambda l, i: (l, i, dst_col // width)),
        out_shape=jax.ShapeDtypeStruct((depth, d, n_total), BF16),
        input_output_aliases={len(cols): 0},
        compiler_params=_cparams(2),
        name="gate_weight_groups",
    )(*([src] * len(cols)), buf)


def _pack_rows(rows, d):
    out = jnp.concatenate([r.reshape(1, d) for r in rows], axis=0)
    return jnp.pad(out, ((0, SUBLANES - out.shape[0]), (0, 0)))


def _lane_row(vals, lanes=LANES):
    return jnp.pad(vals.astype(F32), ((0, 0), (0, lanes - vals.shape[1])))[:, None, :]


def _pick_tile(total, candidates):
    for c in candidates:
        if total % c == 0:
            return c
    raise ValueError(f"no tile for {total} among {candidates}")


def kernel(x, c, ctx, c_ctx, w_ada, b_ada, norm_g, ffn_w_gate, ffn_w_up, ffn_w_down, w_in, gdn_conv, gdn_a_log,
           gdn_dt_bias, gdn_norm, mlstm_i_bias, mlstm_f_bias, mlstm_norm, hy_short_w, hy_short_b, hf_w1, hf_b1,
           hf_w2, hf_b2, hf_w3, hf_freq, hf_decay, hy_bias, w_a_out, w_b_out, w_c_out, w_out, final_norm):
    batch, seq, d = x.shape
    assert batch == 1 and c.shape[0] == 1 and ctx.shape[0] == 1
    ctx_len = ctx.shape[1]
    depth = w_ada.shape[0]
    m = ctx_len + seq
    assert ctx_len % CHUNK == 0 and seq % CHUNK == 0

    gdn_heads = gdn_a_log.shape[-1]
    gdn_vw = w_a_out.shape[1]
    gdn_qkw = (gdn_conv.shape[1] - gdn_vw) // 2
    gdn_dk, gdn_dv = gdn_qkw // gdn_heads, gdn_vw // gdn_heads
    ml_heads = mlstm_i_bias.shape[-1]
    ml_vw = w_b_out.shape[1]
    hy_w = w_c_out.shape[1]
    n_in = w_in.shape[-1]
    ml_qkw = (n_in - (2 * gdn_qkw + gdn_vw) - 4 * gdn_heads - 4 * ml_heads - ml_vw - gdn_vw - ml_vw
              - 3 * hy_w - 3 * d) // 2
    ml_dqk, ml_dv = ml_qkw // ml_heads, ml_vw // ml_heads
    assert gdn_dk == LANES and gdn_dv == LANES and ml_dqk == LANES and ml_dv % LANES == 0

    o_gqkv = 0
    o_gbeta = o_gqkv + 2 * gdn_qkw + gdn_vw
    o_ga = o_gbeta + 2 * gdn_heads
    o_mqkv = o_ga + 2 * gdn_heads
    o_mi = o_mqkv + 2 * ml_qkw + ml_vw
    o_mf = o_mi + 2 * ml_heads
    o_z = o_mf + 2 * ml_heads
    o_o = o_z + gdn_vw
    o_hy = o_o + ml_vw
    o_gates = o_hy + 3 * hy_w
    assert o_gates + 3 * d == n_in

    c_gqkv = 0
    c_mqkv = c_gqkv + 2 * gdn_qkw + gdn_vw
    c_z = c_mqkv + 2 * ml_qkw + ml_vw
    c_o = c_z + gdn_vw
    c_hy = c_o + ml_vw
    c_gates = c_hy + 3 * hy_w
    c_gsmall = c_gates + 3 * d
    c_msmall = c_gsmall + 4 * LANES
    n_p = c_msmall + 4 * LANES

    moves = [(o_gqkv, c_gqkv, c_mqkv - c_gqkv), (o_mqkv, c_mqkv, c_z - c_mqkv), (o_z, c_z, c_gsmall - c_z)]
    tcr = 512 if all(dst % 512 == 0 and width % 512 == 0 for _, dst, width in moves) else 256
    w_in_p = None
    for src_col, dst_col, width in moves:
        w_in_p = _regroup_call(w_in, w_in_p, src_col, dst_col, width, d, n_p, tcr)
    gate_cols, gate_heads = [], []
    for offs, heads in (((o_gbeta, o_ga), gdn_heads), ((o_mi, o_mf), ml_heads)):
        for direction in range(2):
            for off in offs:
                gate_cols.append(off + direction * heads)
                gate_heads.append(heads)
    per_call = 1
    while per_call < len(gate_cols) and c_gsmall % (2 * per_call * LANES) == 0:
        per_call *= 2
    for g0 in range(0, len(gate_cols), per_call):
        w_in_p = _gate_groups_call(w_in, w_in_p, gate_cols[g0:g0 + per_call], gate_heads[g0:g0 + per_call],
                                   c_gsmall + g0 * LANES)

    f_hidden = ffn_w_gate.shape[-1]
    tf = 512
    f_pad = -(-f_hidden // tf) * tf
    wg = _regroup_call(ffn_w_gate.reshape(2 * depth, d, f_hidden), None, 0, 0, f_pad, d, f_pad, tf)
    wu = _regroup_call(ffn_w_up.reshape(2 * depth, d, f_hidden), None, 0, 0, f_pad, d, f_pad, tf)
    wd = _regroup_call(ffn_w_down.reshape(2 * depth, f_hidden, d), None, 0, 0, d, f_pad, d, _pick_tile(d, (512, 256)))
    wg, wu = wg.reshape(depth, 2, d, f_pad), wu.reshape(depth, 2, d, f_pad)
    wd = wd.reshape(depth, 2, f_pad, d)
    wa, wb, wc, wo = (t.astype(BF16) for t in (w_a_out, w_b_out, w_c_out, w_out))

    tm_big = _pick_tile(m, (768, 512, 384, 256, 128, 64))
    tm_mid = _pick_tile(m, (256, 128, 64))
    tm_proj = _pick_tile(m, (1056, 768, 512, 384, 256, 128, 64))
    tm_seg = _pick_tile(math.gcd(ctx_len, seq), (256, 128, 64))
    tn_in = _pick_tile(n_p, (1024, 512, 256, 128))
    nc_ctx = ctx_len // CHUNK

    cond = jnp.pad(jnp.concatenate([c_ctx[None, :], c], axis=0), ((0, SUBLANES - 2), (0, 0)))
    mods = _modulation_call(cond, w_ada, b_ada)

    dft_consts = _dft_constants(seq)
    dense_fwd, dense_inv = _dense_dft_constants(ctx_len)
    feats_lat = _filter_features(seq)
    feats_lat = feats_lat.reshape(seq // DFT_N2, DFT_N2, LANES).swapaxes(0, 1).reshape(seq, LANES)
    feats_ctx = _filter_features(ctx_len)
    tl = _pick_tile(seq, (512, 256, 128, 64))

    s = jnp.concatenate([x[0], ctx[0]], axis=0)
    for l in range(depth):
        last = l == depth - 1

        def mod(idx):
            return [mods[l, 0, idx * d:(idx + 1) * d], mods[l, 1, idx * d:(idx + 1) * d]]

        s = _ffn_call(s, _pack_rows(mod(0) + mod(1) + mod(2), d), norm_g[l, 0][None, :], wg, wu, wd, l, 0,
                      seq, tm_big, tf)
        p = _inproj_call(s, _pack_rows(mod(3) + mod(4), d), norm_g[l, 1][None, :], w_in_p, l, seq, tm_proj, tn_in)

        gqkv = _dwconv_call(p, c_gqkv, 2 * gdn_qkw + gdn_vw, jnp.transpose(gdn_conv[l]),
                            jnp.zeros((1, 2 * gdn_qkw + gdn_vw), F32), seq, tm_seg, 1024, True)
        hc = _dwconv_call(p, c_hy, 3 * hy_w, jnp.transpose(hy_short_w[l]), hy_short_b[l][None, :],
                          seq, tm_seg, 1024, False)

        og, om = _fused_scans_call(
            m // CHUNK,
            _gdn_scan_parts(gqkv, p, c_gsmall // LANES, _lane_row(gdn_a_log[l]), _lane_row(gdn_dt_bias[l]),
                            gdn_heads, gdn_dk, gdn_dv, nc_ctx),
            _mlstm_scan_parts(p, c_mqkv, c_msmall, _lane_row(mlstm_i_bias[l]), _lane_row(mlstm_f_bias[l]),
                              ml_heads, ml_dqk, ml_dv, nc_ctx))

        fargs = (jnp.pad(hf_w1[l], ((0, LANES - hf_w1.shape[1]), (0, 0))), hf_b1[l][None, :], hf_w2[l],
                 hf_b2[l][None, :], hf_w3[l], hf_freq[l], jnp.tile(hf_decay[l], 2)[None, :])
        taps = _filter_call(feats_lat, *fargs, tl, 1024)
        m1, g2, minv, pinv = dft_consts
        kh = seq // DFT_N2
        ta = _dft1_call(taps.reshape(DFT_N2, kh, taps.shape[1]), 0, taps.shape[1], m1, 1024, n_lo_major=True)
        kf = _filter_spectrum_call(ta.reshape(ta.shape[0] * DFT_N2, taps.shape[1]), g2, 1024)
        hl3 = hc.reshape(m // DFT_N2, DFT_N2, 3 * hy_w)
        tc_h = min(hy_w, 1024)
        per = hy_w // tc_h
        z1 = _long_conv(hl3, 0, False, hl3, per, hy_bias[l, 0][None, :], kf, 0, dft_consts, hy_w, tc_h, True)
        z2 = _long_conv(z1, 0, True, hl3, 2 * per, hy_bias[l, 1][None, :], kf, per, dft_consts, hy_w, tc_h, False)
        z2 = z2.reshape(seq, hy_w)
        if not last:
            taps_c = _filter_call(feats_ctx, *fargs, _pick_tile(ctx_len, (256, 128, 64)), 1024)
            kf_c = _dense_spectrum_call(taps_c, dense_fwd, 512)
            zc = _dense_hyena_call(hc[seq:], ctx_len, hy_w, kf_c, dense_fwd, dense_inv, hy_bias[l, 0][None, :],
                                   hy_bias[l, 1][None, :], 256)
        else:
            zc = jnp.zeros((ctx_len, hy_w), F32)
        hy = jnp.concatenate([z2, zc], axis=0)

        merged = _merge_call(og, om, p, c_z, c_o, hy, c_gates, jnp.tile(gdn_norm[l], gdn_heads)[None, :],
                             mlstm_norm[l][None, :], wa, wb, wc, l, (gdn_heads, gdn_dv, ml_heads, ml_dv), tm_mid)
        s = _outproj_call(merged, s, _pack_rows(mod(5), d), wo, l, seq, tm_mid)
        s = _ffn_call(s, _pack_rows(mod(6) + mod(7) + mod(8), d), norm_g[l, 2][None, :], wg, wu, wd, l, 1,
                      seq, tm_big, tf)

    out = _final_norm_call(s, final_norm[None, :], 0, seq, tm_seg)
    return out[None]
```

```python
import functools
import math

import jax
import jax.numpy as jnp
import numpy as np
from jax import lax
from jax.experimental import pallas as pl
from jax.experimental.pallas import tpu as pltpu

F32 = jnp.float32
BF16 = jnp.bfloat16

N_MOD = 9
RMS_EPS = 1e-6
L2_EPS = 1e-6
CHUNK = 64
GDN_CONV = 5
HYENA_SHORT = 3
HYENA_BANDS = 16
LANES = 128
SUBLANES = 8
DFT_N2 = 128
VMEM_LIMIT = 56 * 1024 * 1024


def _cparams(n_axes):
    return pltpu.CompilerParams(dimension_semantics=("arbitrary",) * n_axes,
                                vmem_limit_bytes=VMEM_LIMIT)


def _sigmoid(x):
    return jax.nn.sigmoid(x)


def _silu(x):
    return x * jax.nn.sigmoid(x)


def _softplus(x):
    return jnp.maximum(x, 0.0) + jnp.log(1.0 + jnp.exp(-jnp.abs(x)))


def _dot(a, b):
    return jnp.dot(a.astype(BF16), b.astype(BF16), preferred_element_type=F32)


def _dot_nt(a, b):
    return lax.dot_general(a.astype(BF16), b.astype(BF16), (((1,), (1,)), ((), ())),
                           preferred_element_type=F32)


def _dot_tn(a, b):
    return lax.dot_general(a.astype(BF16), b.astype(BF16), (((0,), (0,)), ((), ())),
                           preferred_element_type=F32)


def _dot_exact(a, b):
    return jnp.dot(a, b, preferred_element_type=F32, precision=lax.Precision.HIGHEST)


def _row_is_ctx(row0, rows, lat_len):
    r = row0 + lax.broadcasted_iota(jnp.int32, (rows, 1), 0)
    return r >= lat_len


def _modnorm(x, g, shift, scale):
    y = x * lax.rsqrt(jnp.mean(x * x, axis=-1, keepdims=True) + RMS_EPS) * g
    return y * (1.0 + scale) + shift


ROW_CHUNK = 128


def _for_row_chunks(tm, fn):
    rc = math.gcd(tm, ROW_CHUNK)

    def body(c, carry):
        fn(pl.multiple_of(c * rc, rc), rc)
        return carry

    lax.fori_loop(0, tm // rc, body, 0)


def _store_modulated(x_ref, m_ref, g_ref, h_sc, row0, tm, lat_len):
    def chunk(r0, rc):
        is_ctx = _row_is_ctx(row0 + r0, rc, lat_len)
        shift = jnp.where(is_ctx, m_ref[0:1, :], m_ref[1:2, :])
        scale = jnp.where(is_ctx, m_ref[2:3, :], m_ref[3:4, :])
        h_sc[pl.ds(r0, rc), :] = _modnorm(x_ref[pl.ds(r0, rc), :], g_ref[...], shift, scale).astype(h_sc.dtype)

    _for_row_chunks(tm, chunk)


def _mod_kernel(c_ref, w_ref, b_ref, o_ref):
    o_ref[0] = _dot(_silu(c_ref[...]), w_ref[0]) + b_ref[0]


def _modulation_call(cond, w_ada, b_ada):
    depth, d, n = w_ada.shape
    tn = _pick_tile(n, (1024, 512, 256, 128))
    return pl.pallas_call(
        _mod_kernel,
        grid=(depth, n // tn),
        in_specs=[pl.BlockSpec((SUBLANES, d), lambda l, j: (0, 0)),
                  pl.BlockSpec((1, d, tn), lambda l, j: (l, 0, j)),
                  pl.BlockSpec((1, 1, tn), lambda l, j: (l, 0, j))],
        out_specs=pl.BlockSpec((1, SUBLANES, tn), lambda l, j: (l, 0, j)),
        out_shape=jax.ShapeDtypeStruct((depth, SUBLANES, n), F32),
        compiler_params=_cparams(2),
        name="adaln_modulation",
    )(cond, w_ada, b_ada.reshape(depth, 1, n))


def _ffn_kernel(x_ref, m_ref, g_ref, wg_ref, wu_ref, wd_ref, o_ref, h_sc, acc_sc, *, tm, lat_len):
    row0 = pl.program_id(0) * tm
    f = pl.program_id(1)

    @pl.when(f == 0)
    def _():
        _store_modulated(x_ref, m_ref, g_ref, h_sc, row0, tm, lat_len)
        acc_sc[...] = jnp.zeros_like(acc_sc)

    h = h_sc[...]
    a = _silu(_dot(h, wg_ref[...])) * _dot(h, wu_ref[...])
    acc_sc[...] += _dot(a, wd_ref[...])

    @pl.when(f == pl.num_programs(1) - 1)
    def _():
        def chunk(r0, rc):
            gate = jnp.where(_row_is_ctx(row0 + r0, rc, lat_len), m_ref[4:5, :], m_ref[5:6, :])
            rows = pl.ds(r0, rc)
            o_ref[rows, :] = x_ref[rows, :] + 0.5 * gate * acc_sc[rows, :]

        _for_row_chunks(tm, chunk)


def _ffn_call(x, mods, g, wg, wu, wd, layer, which, lat_len, tm, tf):
    m, d = x.shape
    fp = wg.shape[-1]
    return pl.pallas_call(
        functools.partial(_ffn_kernel, tm=tm, lat_len=lat_len),
        grid=(m // tm, fp // tf),
        in_specs=[pl.BlockSpec((tm, d), lambda i, f: (i, 0)),
                  pl.BlockSpec((SUBLANES, d), lambda i, f: (0, 0)),
                  pl.BlockSpec((1, d), lambda i, f: (0, 0)),
                  pl.BlockSpec((None, None, d, tf), lambda i, f: (layer, which, 0, f)),
                  pl.BlockSpec((None, None, d, tf), lambda i, f: (layer, which, 0, f)),
                  pl.BlockSpec((None, None, tf, d), lambda i, f: (layer, which, f, 0))],
        out_specs=pl.BlockSpec((tm, d), lambda i, f: (i, 0)),
        out_shape=jax.ShapeDtypeStruct((m, d), F32),
        scratch_shapes=[pltpu.VMEM((tm, d), BF16), pltpu.VMEM((tm, d), F32)],
        compiler_params=_cparams(2),
        name="macaron_swiglu",
    )(x, mods, g, wg, wu, wd)


def _inproj_kernel(x_ref, m_ref, g_ref, w_ref, o_ref, h_sc, *, tm, lat_len):
    @pl.when(pl.program_id(1) == 0)
    def _():
        _store_modulated(x_ref, m_ref, g_ref, h_sc, pl.program_id(0) * tm, tm, lat_len)

    o_ref[...] = _dot_nt(h_sc[...], w_ref[...])


def _inproj_call(x, mods, g, w, layer, lat_len, tm, tn):
    m, d = x.shape
    n = w.shape[1]
    return pl.pallas_call(
        functools.partial(_inproj_kernel, tm=tm, lat_len=lat_len),
        grid=(m // tm, n // tn),
        in_specs=[pl.BlockSpec((tm, d), lambda i, j: (i, 0)),
                  pl.BlockSpec((SUBLANES, d), lambda i, j: (0, 0)),
                  pl.BlockSpec((1, d), lambda i, j: (0, 0)),
                  pl.BlockSpec((None, tn, d), lambda i, j: (layer, j, 0))],
        out_specs=pl.BlockSpec((tm, tn), lambda i, j: (i, j)),
        out_shape=jax.ShapeDtypeStruct((m, n), F32),
        scratch_shapes=[pltpu.VMEM((tm, d), BF16)],
        compiler_params=_cparams(2),
        name="mixer_in_proj",
    )(x, mods, g, w)


def _dwconv_kernel(prev_ref, cur_ref, next_ref, w_ref, b_ref, o_ref, ext_sc, *, tm, taps, split, m_rows, act):
    row0 = pl.program_id(0) * tm
    at_start = jnp.logical_or(row0 == 0, row0 == split)
    at_end = jnp.logical_or(row0 + tm == split, row0 + tm == m_rows)
    ext_sc[0:SUBLANES, :] = jnp.where(at_start, 0.0, prev_ref[...])
    ext_sc[SUBLANES:SUBLANES + tm, :] = cur_ref[...]
    ext_sc[SUBLANES + tm:2 * SUBLANES + tm, :] = jnp.where(at_end, 0.0, next_ref[...])
    acc = jnp.zeros(o_ref.shape, F32) + b_ref[...]
    for t in range(taps):
        acc = acc + ext_sc[pl.ds(SUBLANES + t - taps // 2, tm), :] * w_ref[t:t + 1, :]
    o_ref[...] = _silu(acc) if act else acc


def _dwconv_call(p, col0, ncols, w_t, b, split, tm, tc, act):
    m = p.shape[0]
    taps = w_t.shape[0]
    assert split % tm == 0 and m % tm == 0 and col0 % tc == 0 and taps // 2 <= SUBLANES
    cb0 = col0 // tc
    rb = tm // SUBLANES
    last = m // SUBLANES - 1
    return pl.pallas_call(
        functools.partial(_dwconv_kernel, tm=tm, taps=taps, split=split, m_rows=m, act=act),
        grid=(m // tm, ncols // tc),
        in_specs=[pl.BlockSpec((SUBLANES, tc), lambda i, j: (jnp.maximum(i * rb - 1, 0), cb0 + j)),
                  pl.BlockSpec((tm, tc), lambda i, j: (i, cb0 + j)),
                  pl.BlockSpec((SUBLANES, tc), lambda i, j: (jnp.minimum((i + 1) * rb, last), cb0 + j)),
                  pl.BlockSpec((taps, tc), lambda i, j: (0, j)),
                  pl.BlockSpec((1, tc), lambda i, j: (0, j))],
        out_specs=pl.BlockSpec((tm, tc), lambda i, j: (i, j)),
        out_shape=jax.ShapeDtypeStruct((m, ncols), F32),
        scratch_shapes=[pltpu.VMEM((tm + 2 * SUBLANES, tc), F32)],
        compiler_params=_cparams(2),
        name="depthwise_conv",
    )(p, p, p, w_t, b)


def _fwd_chunk(s, nc_ctx, nc_tot):
    return jnp.where(s < nc_ctx, nc_tot - nc_ctx + s, s - nc_ctx)


def _bwd_chunk(s, nc_ctx, nc_tot):
    del nc_ctx
    return nc_tot - 1 - s


def _causal_masks(direction):
    ii = lax.broadcasted_iota(jnp.int32, (CHUNK, CHUNK), 0)
    jj = lax.broadcasted_iota(jnp.int32, (CHUNK, CHUNK), 1)
    rel = ii - jj if direction == 0 else jj - ii
    return rel >= 0, rel > 0


def _unit_triangular_inverses(mats):
    ii = lax.broadcasted_iota(jnp.int32, (CHUNK, CHUNK), 0)
    jj = lax.broadcasted_iota(jnp.int32, (CHUNK, CHUNK), 1)
    eye = (ii == jj).astype(F32)

    def same_block(width):
        return (ii // width) == (jj // width)

    inner = same_block(8)
    d1 = [jnp.where(inner, a, 0.0) for a in mats]
    d2 = [_dot(x, x) for x in d1]
    yield
    d4 = [_dot(x, x) for x in d2]
    p = [_dot(eye - x, eye + y) for x, y in zip(d1, d2)]
    yield
    p = [_dot(x, eye + y) for x, y in zip(p, d4)]
    yield
    for width in (16, 32, 64):
        outer = same_block(width)
        ring = jnp.logical_and(outer, jnp.logical_not(inner))
        t = [_dot(x, jnp.where(ring, a, 0.0)) for x, a in zip(p, mats)]
        yield
        p = [x - _dot(y, x) for x, y in zip(p, t)]
        yield
        inner = outer
    return p


def _gdn_scan_kernel(qf_ref, kf_ref, vf_ref, btf_ref, af_ref, qb_ref, kb_ref, vb_ref, btb_ref, ab_ref,
                     alog_ref, dtb_ref, of_ref, ob_ref, s_sc, *, heads, dk, dv):
    @pl.when(pl.program_id(0) == 0)
    def _():
        s_sc[...] = jnp.zeros_like(s_sc)

    chains = []
    for direction, (q_ref, k_ref, v_ref, bt_ref, a_ref, o_ref) in enumerate(
            ((qf_ref, kf_ref, vf_ref, btf_ref, af_ref, of_ref), (qb_ref, kb_ref, vb_ref, btb_ref, ab_ref, ob_ref))):
        incl, strict = _causal_masks(direction)
        beta = _sigmoid(bt_ref[...])
        glog = -jnp.exp(alog_ref[direction]) * _softplus(a_ref[...] + dtb_ref[direction])
        gcum = _dot_exact(incl.astype(F32), glog)
        gtot = gcum[CHUNK - 1:CHUNK, :] if direction == 0 else gcum[0:1, :]
        e_in = jnp.exp(gcum)
        e_out = jnp.exp(gtot - gcum)
        g_end = jnp.exp(gtot)
        gcum_t = gcum.T
        for h in range(heads):
            q = q_ref[:, h * dk:(h + 1) * dk]
            k = k_ref[:, h * dk:(h + 1) * dk]
            q = q * lax.rsqrt(jnp.sum(q * q, axis=-1, keepdims=True) + L2_EPS) * (dk ** -0.5)
            k = k * lax.rsqrt(jnp.sum(k * k, axis=-1, keepdims=True) + L2_EPS)
            b_col = beta[:, h:h + 1]
            kb = k * b_col
            chains.append(dict(
                q=q, k=k, kb=kb, strict=strict, o_ref=o_ref, h=h, slot=direction * heads + h,
                decay=jnp.exp(jnp.where(incl, gcum[:, h:h + 1] - gcum_t[h:h + 1, :], -jnp.inf)),
                rhs=jnp.concatenate([v_ref[:, h * dv:(h + 1) * dv] * b_col, kb * e_in[:, h:h + 1]], axis=1),
                q_in=q * e_in[:, h:h + 1], k_out=k * e_out[:, h:h + 1], g_end=g_end[:, h:h + 1]))

    yield
    a_mats = [jnp.where(c["strict"], _dot_nt(c["kb"], c["k"]) * c["decay"], 0.0) for c in chains]
    qk = [_dot_nt(c["q"], c["k"]) * c["decay"] for c in chains]
    yield
    t_inv = yield from _unit_triangular_inverses(a_mats)
    sol = [_dot(t, c["rhs"]) for t, c in zip(t_inv, chains)]
    yield
    states = [s_sc[c["slot"]] for c in chains]
    v_new = [x[:, :dv] - _dot(x[:, dv:], st) for x, st in zip(sol, states)]
    o_inter = [_dot(c["q_in"], st) for c, st in zip(chains, states)]
    yield
    o_intra = [_dot(x, y) for x, y in zip(qk, v_new)]
    s_upd = [_dot_tn(c["k_out"], y) for c, y in zip(chains, v_new)]
    yield
    for c, st, x, y, z in zip(chains, states, o_inter, o_intra, s_upd):
        c["o_ref"][:, c["h"] * dv:(c["h"] + 1) * dv] = x + y
        s_sc[c["slot"]] = st * c["g_end"] + z


def _gdn_scan_parts(qkv, p, beta_cb, alog, dtb, heads, dk, dv, nc_ctx):
    m = qkv.shape[0]
    nc = m // CHUNK
    qw, vw = heads * dk, heads * dv
    assert qw == vw
    bidx = functools.partial(_bwd_chunk, nc_ctx=nc_ctx, nc_tot=nc)

    def specs(row):
        return [pl.BlockSpec((CHUNK, qw), lambda s: (row(s), 0)),
                pl.BlockSpec((CHUNK, qw), lambda s: (row(s), 1)),
                pl.BlockSpec((CHUNK, vw), lambda s: (row(s), 2))]

    def gate_specs(row, direction):
        return [pl.BlockSpec((CHUNK, LANES), lambda s: (row(s), beta_cb + 2 * direction)),
                pl.BlockSpec((CHUNK, LANES), lambda s: (row(s), beta_cb + 2 * direction + 1))]

    fwd = functools.partial(_fwd_chunk, nc_ctx=nc_ctx, nc_tot=nc)
    return dict(
        kernel=functools.partial(_gdn_scan_kernel, heads=heads, dk=dk, dv=dv),
        in_specs=(specs(fwd) + gate_specs(fwd, 0) + specs(bidx) + gate_specs(bidx, 1)
                  + [pl.BlockSpec((2, 1, LANES), lambda s: (0, 0, 0)),
                     pl.BlockSpec((2, 1, LANES), lambda s: (0, 0, 0))]),
        operands=(qkv, qkv, qkv, p, p, qkv, qkv, qkv, p, p, alog, dtb),
        out_specs=[pl.BlockSpec((CHUNK, vw), lambda s: (fwd(s), 0)),
                   pl.BlockSpec((CHUNK, vw), lambda s: (bidx(s), 0))],
        out_shape=[jax.ShapeDtypeStruct((m, vw), F32)] * 2,
        scratch_shapes=[pltpu.VMEM((2 * heads, dk, dv), F32)])


def _mlstm_scan_kernel(qf_ref, kf_ref, vf_ref, if_ref, ff_ref, qb_ref, kb_ref, vb_ref, ib_ref, fb_ref,
                       ibias_ref, fbias_ref, of_ref, ob_ref, c_sc, m_sc, *, heads, dqk, dv):
    @pl.when(pl.program_id(0) == 0)
    def _():
        c_sc[...] = jnp.zeros_like(c_sc)
        m_sc[...] = jnp.zeros_like(m_sc)

    ones_col = (lax.broadcasted_iota(jnp.int32, (CHUNK, LANES), 1) == 0).astype(F32)
    chains = []
    for direction, (q_ref, k_ref, v_ref, i_ref, f_ref, o_ref) in enumerate(
            ((qf_ref, kf_ref, vf_ref, if_ref, ff_ref, of_ref), (qb_ref, kb_ref, vb_ref, ib_ref, fb_ref, ob_ref))):
        incl, _ = _causal_masks(direction)
        log_i = i_ref[...] + ibias_ref[direction]
        log_f = -_softplus(-(f_ref[...] + fbias_ref[direction]))
        bcum = _dot_exact(incl.astype(F32), log_f)
        btot = bcum[CHUNK - 1:CHUNK, :] if direction == 0 else bcum[0:1, :]
        log_end = btot - bcum + log_i
        m_st = m_sc[direction]
        m_new = jnp.maximum(btot + m_st, jnp.max(log_end, axis=0, keepdims=True))
        m_sc[direction] = m_new
        carry = jnp.exp(btot + m_st - m_new)
        k_scale = jnp.exp(log_end - m_new)
        b_inter = bcum + m_st
        bcum_t = bcum.T
        log_i_t = log_i.T
        for h in range(heads):
            log_d = jnp.where(incl, bcum[:, h:h + 1] - bcum_t[h:h + 1, :] + log_i_t[h:h + 1, :], -jnp.inf)
            m_t = jnp.maximum(b_inter[:, h:h + 1], jnp.max(log_d, axis=-1, keepdims=True))
            chains.append(dict(
                q=q_ref[:, h * dqk:(h + 1) * dqk] * (dqk ** -0.5), k=k_ref[:, h * dqk:(h + 1) * dqk],
                v_ext=jnp.concatenate([v_ref[:, h * dv:(h + 1) * dv], ones_col], axis=1),
                p_intra=jnp.exp(log_d - m_t), w_inter=jnp.exp(b_inter[:, h:h + 1] - m_t), floor=jnp.exp(-m_t),
                k_scale=k_scale[:, h:h + 1], carry=carry[:, h:h + 1], o_ref=o_ref, h=h, slot=direction * heads + h))
        yield

    s = [_dot_nt(c["q"], c["k"]) * c["p_intra"] for c in chains]
    yield
    states = [c_sc[c["slot"]] for c in chains]
    inter = [_dot(c["q"], st) for c, st in zip(chains, states)]
    yield
    intra = [_dot(x, c["v_ext"]) for x, c in zip(s, chains)]
    yield
    upd = [_dot_tn(c["k"] * c["k_scale"], c["v_ext"]) for c in chains]
    yield
    for idx, (c, st, x, y, z) in enumerate(zip(chains, states, inter, intra, upd)):
        out = c["w_inter"] * x + y
        den = jnp.maximum(jnp.abs(out[:, dv:dv + 1]), c["floor"])
        c["o_ref"][:, c["h"] * dv:(c["h"] + 1) * dv] = out[:, :dv] / den
        c_sc[c["slot"]] = c["carry"] * st + z
        if idx % 2 == 1:
            yield


def _mlstm_scan_parts(p, q_cb, gate_cb, ib, fb, heads, dqk, dv, nc_ctx):
    m = p.shape[0]
    nc = m // CHUNK
    qw, vw = heads * dqk, heads * dv
    assert q_cb % qw == 0 and (q_cb + 2 * qw) % vw == 0
    bidx = functools.partial(_bwd_chunk, nc_ctx=nc_ctx, nc_tot=nc)
    fwd = functools.partial(_fwd_chunk, nc_ctx=nc_ctx, nc_tot=nc)
    qb, kb, vb = q_cb // qw, q_cb // qw + 1, (q_cb + 2 * qw) // vw
    gb = gate_cb // LANES

    def specs(row, direction):
        return [pl.BlockSpec((CHUNK, qw), lambda s: (row(s), qb)),
                pl.BlockSpec((CHUNK, qw), lambda s: (row(s), kb)),
                pl.BlockSpec((CHUNK, vw), lambda s: (row(s), vb)),
                pl.BlockSpec((CHUNK, LANES), lambda s: (row(s), gb + 2 * direction)),
                pl.BlockSpec((CHUNK, LANES), lambda s: (row(s), gb + 2 * direction + 1))]

    return dict(
        kernel=functools.partial(_mlstm_scan_kernel, heads=heads, dqk=dqk, dv=dv),
        in_specs=(specs(fwd, 0) + specs(bidx, 1)
                  + [pl.BlockSpec((2, 1, LANES), lambda s: (0, 0, 0)),
                     pl.BlockSpec((2, 1, LANES), lambda s: (0, 0, 0))]),
        operands=(p, p, p, p, p, p, p, p, p, p, ib, fb),
        out_specs=[pl.BlockSpec((CHUNK, vw), lambda s: (fwd(s), 0)),
                   pl.BlockSpec((CHUNK, vw), lambda s: (bidx(s), 0))],
        out_shape=[jax.ShapeDtypeStruct((m, vw), F32)] * 2,
        scratch_shapes=[pltpu.VMEM((2 * heads, dqk, dv + LANES), F32), pltpu.VMEM((2, 1, LANES), F32)])


def _fused_scans_kernel(*refs, parts):
    n_in = sum(part[1] for part in parts)
    n_out = sum(part[2] for part in parts)
    i0, o0, s0 = 0, n_in, n_in + n_out
    running = []
    for fn, ni, no, ns in parts:
        running.append(fn(*refs[i0:i0 + ni], *refs[o0:o0 + no], *refs[s0:s0 + ns]))
        i0, o0, s0 = i0 + ni, o0 + no, s0 + ns
    done = object()
    while running:
        running = [g for g in running if next(g, done) is not done]


def _fused_scans_call(nc, *scans):
    parts = tuple((sc["kernel"], len(sc["in_specs"]), len(sc["out_specs"]), len(sc["scratch_shapes"]))
                  for sc in scans)
    outs = pl.pallas_call(
        functools.partial(_fused_scans_kernel, parts=parts),
        grid=(nc,),
        in_specs=sum((list(sc["in_specs"]) for sc in scans), []),
        out_specs=sum((list(sc["out_specs"]) for sc in scans), []),
        out_shape=sum((list(sc["out_shape"]) for sc in scans), []),
        scratch_shapes=sum((list(sc["scratch_shapes"]) for sc in scans), []),
        compiler_params=_cparams(1),
        name="gdn_mlstm_scans",
    )(*sum((list(sc["operands"]) for sc in scans), []))
    split, res = 0, []
    for sc in scans:
        res.append(tuple(outs[split:split + len(sc["out_specs"])]))
        split += len(sc["out_specs"])
    return res


def _filter_kernel(feat_ref, w1_ref, b1_ref, w2_ref, b2_ref, w3_ref, freq_ref, dec_ref, o_ref, h_sc):
    @pl.when(pl.program_id(1) == 0)
    def _():
        h = jnp.sin(freq_ref[0:1, :] * (_dot(feat_ref[...], w1_ref[...]) + b1_ref[...]))
        h_sc[...] = jnp.sin(freq_ref[1:2, :] * (_dot(h, w2_ref[...]) + b2_ref[...]))

    o_ref[...] = _dot(h_sc[...], w3_ref[...]) * jnp.exp(-feat_ref[:, 0:1] * jnp.abs(dec_ref[...]))


def _filter_call(feats, w1, b1, w2, b2, w3, freq, dec, tl, tn):
    length, fp = feats.shape
    hid = w2.shape[0]
    n = w3.shape[1]
    return pl.pallas_call(
        _filter_kernel,
        grid=(length // tl, n // tn),
        in_specs=[pl.BlockSpec((tl, fp), lambda i, j: (i, 0)),
                  pl.BlockSpec((fp, hid), lambda i, j: (0, 0)),
                  pl.BlockSpec((1, hid), lambda i, j: (0, 0)),
                  pl.BlockSpec((hid, hid), lambda i, j: (0, 0)),
                  pl.BlockSpec((1, hid), lambda i, j: (0, 0)),
                  pl.BlockSpec((hid, tn), lambda i, j: (0, j)),
                  pl.BlockSpec((2, hid), lambda i, j: (0, 0)),
                  pl.BlockSpec((1, tn), lambda i, j: (0, j))],
        out_specs=pl.BlockSpec((tl, tn), lambda i, j: (i, j)),
        out_shape=jax.ShapeDtypeStruct((length, n), F32),
        scratch_shapes=[pltpu.VMEM((tl, hid), F32)],
        compiler_params=_cparams(2),
        name="hyena_filter",
    )(feats, w1, b1, w2, b2, w3, freq, dec)


def _dft_constants(length):
    n2 = DFT_N2
    n = 2 * length
    n1 = n // n2
    kh = n1 // 2
    na = kh + 1

    def cis(num, den):
        ang = (2.0 * math.pi / den) * (num % den).astype(F32)
        return jnp.cos(ang), jnp.sin(ang)

    ar = lambda size: jnp.arange(size, dtype=jnp.int32)
    c, s = cis(ar(na)[None, :, None] * (n2 * ar(kh)[None, None, :] + ar(n2)[:, None, None]), n)
    m1 = jnp.stack([c, -s], axis=2).reshape(n2, 2 * na, kh)
    c, s = cis(ar(n2)[:, None] * ar(n2)[None, :], n2)
    g2 = jnp.block([[c, s], [-s, c]])
    c, s = cis(ar(n2)[None, :, None] * (ar(na)[:, None, None] + n1 * ar(n2)[None, None, :]), n)
    minv = jnp.concatenate([jnp.concatenate([c, -s], axis=2), jnp.concatenate([s, c], axis=2)], axis=1)
    c, s = cis(ar(kh)[:, None] * ar(na)[None, :], n1)
    weight = jnp.where(jnp.logical_or(ar(na) == 0, ar(na) == kh), 1.0, 2.0)[None, :] / n
    pinv = jnp.stack([c * weight, -s * weight], axis=2).reshape(kh, 2 * na)
    return m1.astype(BF16), g2.astype(BF16), minv.astype(BF16), pinv.astype(BF16)


def _dft1_kernel(x_ref, m_ref, o_ref, *, n_lo_major):
    for j in range(SUBLANES):
        o_ref[:, j, :] = _dot(m_ref[j], x_ref[j] if n_lo_major else x_ref[:, j, :])


def _dft1_call(x3, col_blk0, width, m1, tc, n_lo_major=False):
    n2, rows, kh = m1.shape
    if n_lo_major:
        x_spec = pl.BlockSpec((SUBLANES, kh, tc), lambda c, g: (g, 0, col_blk0 + c))
    else:
        x_spec = pl.BlockSpec((kh, SUBLANES, tc), lambda c, g: (0, g, col_blk0 + c))
    return pl.pallas_call(
        functools.partial(_dft1_kernel, n_lo_major=n_lo_major),
        grid=(width // tc, n2 // SUBLANES),
        in_specs=[x_spec,
                  pl.BlockSpec((SUBLANES, rows, kh), lambda c, g: (g, 0, 0))],
        out_specs=pl.BlockSpec((rows, SUBLANES, tc), lambda c, g: (0, g, c)),
        out_shape=jax.ShapeDtypeStruct((rows, n2, width), F32),
        compiler_params=_cparams(2),
        name="dft_stage1",
    )(x3, m1)


def _filter_spectrum_kernel(af_ref, ab_ref, g_ref, o_ref):
    sf = _dot(g_ref[...], af_ref[...])
    sb = _dot(g_ref[...], ab_ref[...])
    sign = jnp.where(lax.broadcasted_iota(jnp.int32, (2 * DFT_N2, 1), 0) < DFT_N2, 1.0, -1.0)
    o_ref[...] = sf + sign * sb


def _filter_spectrum_call(a, g2, tc):
    rows, width = a.shape
    half = width // 2
    blk = 2 * DFT_N2
    return pl.pallas_call(
        _filter_spectrum_kernel,
        grid=(rows // blk, half // tc),
        in_specs=[pl.BlockSpec((blk, tc), lambda a_, j: (a_, j)),
                  pl.BlockSpec((blk, tc), lambda a_, j: (a_, half // tc + j)),
                  pl.BlockSpec((blk, blk), lambda a_, j: (0, 0))],
        out_specs=pl.BlockSpec((blk, tc), lambda a_, j: (a_, j)),
        out_shape=jax.ShapeDtypeStruct((rows, half), F32),
        compiler_params=_cparams(2),
        name="hyena_filter_spectrum",
    )(a, a, g2)


def _complex_mul(x, kf, n):
    xr, xi = x[:n], x[n:]
    kr, ki = kf[:n], kf[n:]
    return jnp.concatenate([xr * kr - xi * ki, xr * ki + xi * kr], axis=0)


def _spectral_mid_kernel(a_ref, g_ref, kf_ref, minv_ref, o_ref):
    x = _dot(g_ref[...], a_ref[...])
    o_ref[...] = _dot(minv_ref[0], _complex_mul(x, kf_ref[...], DFT_N2))


def _spectral_mid_call(a, g2, kf, kf_cb, minv, tc):
    rows, width = a.shape
    blk = 2 * DFT_N2
    return pl.pallas_call(
        _spectral_mid_kernel,
        grid=(rows // blk, width // tc),
        in_specs=[pl.BlockSpec((blk, tc), lambda a_, j: (a_, j)),
                  pl.BlockSpec((blk, blk), lambda a_, j: (0, 0)),
                  pl.BlockSpec((blk, tc), lambda a_, j: (a_, kf_cb + j)),
                  pl.BlockSpec((1, blk, blk), lambda a_, j: (a_, 0, 0))],
        out_specs=pl.BlockSpec((blk, tc), lambda a_, j: (a_, j)),
        out_shape=jax.ShapeDtypeStruct((rows, width), F32),
        compiler_params=_cparams(2),
        name="hyena_spectral_mid",
    )(a, g2, kf, minv)


def _idft2_kernel(z_ref, p_ref, v_ref, gate_ref, bias_ref, o_ref, *, v_major, out_major):
    for j in range(SUBLANES):
        y = _dot(p_ref[...], z_ref[:, j, :])
        v = v_ref[j] if v_major else v_ref[:, j, :]
        res = gate_ref[:, j, :] * (y + v * bias_ref[...])
        if out_major:
            o_ref[j] = res
        else:
            o_ref[:, j, :] = res


def _idft2_call(z3, pinv, v3, v_cb, v_major, gate3, gate_cb, bias, width, tc, out_major):
    rows, n2, _ = z3.shape
    kh = pinv.shape[0]

    def time_spec(major, cb):
        if major:
            return pl.BlockSpec((SUBLANES, kh, tc), lambda c, g: (g, 0, cb + c))
        return pl.BlockSpec((kh, SUBLANES, tc), lambda c, g: (0, g, cb + c))

    return pl.pallas_call(
        functools.partial(_idft2_kernel, v_major=v_major, out_major=out_major),
        grid=(width // tc, n2 // SUBLANES),
        in_specs=[pl.BlockSpec((rows, SUBLANES, tc), lambda c, g: (0, g, c)),
                  pl.BlockSpec((kh, rows), lambda c, g: (0, 0)),
                  time_spec(v_major, v_cb),
                  time_spec(False, gate_cb),
                  pl.BlockSpec((1, tc), lambda c, g: (0, c))],
        out_specs=time_spec(out_major, 0),
        out_shape=jax.ShapeDtypeStruct((n2, kh, width) if out_major else (kh, n2, width), F32),
        compiler_params=_cparams(2),
        name="idft_stage2_gate",
    )(z3, pinv, v3, gate3, bias)


def _long_conv(v3, v_cb, v_major, gate3, gate_cb, bias, kf, kf_cb, consts, width, tc, out_major):
    m1, g2, minv, pinv = consts
    a = _dft1_call(v3, v_cb, width, m1, tc, n_lo_major=v_major)
    rows = a.shape[0]
    z = _spectral_mid_call(a.reshape(rows * DFT_N2, width), g2, kf, kf_cb, minv, tc)
    return _idft2_call(z.reshape(rows, DFT_N2, width), pinv, v3, v_cb, v_major, gate3, gate_cb, bias, width, tc,
                       out_major)


def _dense_dft_constants(length):
    n = 2 * length
    k = jnp.arange(n, dtype=jnp.int32)[:, None]
    t = jnp.arange(length, dtype=jnp.int32)[None, :]
    ang = (2.0 * math.pi / n) * ((k * t) % n).astype(F32)
    c, s = jnp.cos(ang), jnp.sin(ang)
    fwd = jnp.concatenate([c, -s], axis=0)
    inv = jnp.concatenate([c.T, -s.T], axis=1) / n
    return fwd.astype(BF16), inv.astype(BF16)


def _dense_spectrum_kernel(tf_ref, tb_ref, f_ref, o_ref, *, n):
    sign = jnp.where(lax.broadcasted_iota(jnp.int32, (2 * n, 1), 0) < n, 1.0, -1.0)
    o_ref[...] = _dot(f_ref[...], tf_ref[...]) + sign * _dot(f_ref[...], tb_ref[...])


def _dense_spectrum_call(taps, fwd, tc):
    length, width = taps.shape
    half = width // 2
    rows = fwd.shape[0]
    return pl.pallas_call(
        functools.partial(_dense_spectrum_kernel, n=rows // 2),
        grid=(half // tc,),
        in_specs=[pl.BlockSpec((length, tc), lambda j: (0, j)),
                  pl.BlockSpec((length, tc), lambda j: (0, half // tc + j)),
                  pl.BlockSpec((rows, length), lambda j: (0, 0))],
        out_specs=pl.BlockSpec((rows, tc), lambda j: (0, j)),
        out_shape=jax.ShapeDtypeStruct((rows, half), F32),
        compiler_params=_cparams(1),
        name="hyena_filter_spectrum_dense",
    )(taps, taps, fwd)


def _dense_hyena_kernel(v_ref, x1_ref, x2_ref, kf0_ref, kf1_ref, f_ref, inv_ref, b0_ref, b1_ref, o_ref, *, n):
    def conv(u, kf_ref, b_ref):
        y = _dot(inv_ref[...], _complex_mul(_dot(f_ref[...], u), kf_ref[...], n))
        return y + u * b_ref[...]

    z = x1_ref[...] * conv(v_ref[...], kf0_ref, b0_ref)
    o_ref[...] = x2_ref[...] * conv(z, kf1_ref, b1_ref)


def _dense_hyena_call(hc, length, width, kf, fwd, inv, bias0, bias1, tc):
    rows = fwd.shape[0]
    per = width // tc
    return pl.pallas_call(
        functools.partial(_dense_hyena_kernel, n=rows // 2),
        grid=(per,),
        in_specs=[pl.BlockSpec((length, tc), lambda j: (0, j)),
                  pl.BlockSpec((length, tc), lambda j: (0, per + j)),
                  pl.BlockSpec((length, tc), lambda j: (0, 2 * per + j)),
                  pl.BlockSpec((rows, tc), lambda j: (0, j)),
                  pl.BlockSpec((rows, tc), lambda j: (0, per + j)),
                  pl.BlockSpec((rows, length), lambda j: (0, 0)),
                  pl.BlockSpec((length, rows), lambda j: (0, 0)),
                  pl.BlockSpec((1, tc), lambda j: (0, j)),
                  pl.BlockSpec((1, tc), lambda j: (0, j))],
        out_specs=pl.BlockSpec((length, tc), lambda j: (0, j)),
        out_shape=jax.ShapeDtypeStruct((length, width), F32),
        compiler_params=_cparams(1),
        name="hyena_dense",
    )(hc, hc, hc, kf, kf, fwd, inv, bias0, bias1)


def _filter_features(length):
    t = jnp.linspace(0.0, 1.0, length, dtype=F32)[:, None]
    w = (2.0 * math.pi / length) * jnp.arange(length, dtype=F32)[:, None]
    bands = jnp.linspace(1e-4, HYENA_BANDS - 1, HYENA_BANDS, dtype=F32)[None, :]
    feats = jnp.concatenate([t, jnp.cos(bands * w), -jnp.sin(bands * w)], axis=-1)
    return jnp.pad(feats, ((0, 0), (0, LANES - feats.shape[1])))


def _head_rms(x, heads, width):
    outs = []
    for h in range(heads):
        xh = x[:, h * width:(h + 1) * width]
        outs.append(xh * lax.rsqrt(jnp.mean(xh * xh, axis=-1, keepdims=True) + RMS_EPS))
    return jnp.concatenate(outs, axis=1)


def _merge_kernel(ogf_ref, ogb_ref, omf_ref, omb_ref, z_ref, o_ref, hy_ref, g0_ref, g1_ref, g2_ref, gn_ref, mn_ref,
                  wa_ref, wb_ref, wc_ref, out_ref, *, gdn_heads, gdn_dv, ml_heads, ml_dv):
    a = _head_rms(ogf_ref[...] + ogb_ref[...], gdn_heads, gdn_dv) * gn_ref[...] * _silu(z_ref[...])
    b = _head_rms(omf_ref[...] + omb_ref[...], ml_heads, ml_dv) * mn_ref[...] * _sigmoid(o_ref[...])
    out_ref[...] = (_sigmoid(g0_ref[...]) * _dot(a, wa_ref[...]) + _sigmoid(g1_ref[...]) * _dot(b, wb_ref[...])
                    + _sigmoid(g2_ref[...]) * _dot(hy_ref[...], wc_ref[...]))


def _merge_call(og, om, p, z_cb, o_cb, hy, gate_cb, gn, mn, wa, wb, wc, layer, heads, tm):
    m, d = p.shape[0], wa.shape[-1]
    gw, mw, hw = og[0].shape[-1], om[0].shape[-1], hy.shape[-1]
    gdn_heads, gdn_dv, ml_heads, ml_dv = heads
    zb, ob, gb = z_cb // gw, o_cb // mw, gate_cb // d
    return pl.pallas_call(
        functools.partial(_merge_kernel, gdn_heads=gdn_heads, gdn_dv=gdn_dv, ml_heads=ml_heads, ml_dv=ml_dv),
        grid=(m // tm,),
        in_specs=[pl.BlockSpec((tm, gw), lambda i: (i, 0)),
                  pl.BlockSpec((tm, gw), lambda i: (i, 0)),
                  pl.BlockSpec((tm, mw), lambda i: (i, 0)),
                  pl.BlockSpec((tm, mw), lambda i: (i, 0)),
                  pl.BlockSpec((tm, gw), lambda i: (i, zb)),
                  pl.BlockSpec((tm, mw), lambda i: (i, ob)),
                  pl.BlockSpec((tm, hw), lambda i: (i, 0)),
                  pl.BlockSpec((tm, d), lambda i: (i, gb)),
                  pl.BlockSpec((tm, d), lambda i: (i, gb + 1)),
                  pl.BlockSpec((tm, d), lambda i: (i, gb + 2)),
                  pl.BlockSpec((1, gw), lambda i: (0, 0)),
                  pl.BlockSpec((1, mw), lambda i: (0, 0)),
                  pl.BlockSpec((None, gw, d), lambda i: (layer, 0, 0), pipeline_mode=pl.Buffered(1)),
                  pl.BlockSpec((None, mw, d), lambda i: (layer, 0, 0), pipeline_mode=pl.Buffered(1)),
                  pl.BlockSpec((None, hw, d), lambda i: (layer, 0, 0), pipeline_mode=pl.Buffered(1))],
        out_specs=pl.BlockSpec((tm, d), lambda i: (i, 0)),
        out_shape=jax.ShapeDtypeStruct((m, d), F32),
        compiler_params=_cparams(1),
        name="branch_merge",
    )(og[0], og[1], om[0], om[1], p, p, hy, p, p, p, gn, mn, wa, wb, wc)


def _outproj_kernel(y_ref, x_ref, m_ref, w_ref, o_ref, *, tm, lat_len):
    gate = jnp.where(_row_is_ctx(pl.program_id(0) * tm, tm, lat_len), m_ref[0:1, :], m_ref[1:2, :])
    o_ref[...] = x_ref[...] + gate * _dot(y_ref[...], w_ref[...])


def _outproj_call(y, x, mods, w, layer, lat_len, tm):
    m, d = x.shape
    return pl.pallas_call(
        functools.partial(_outproj_kernel, tm=tm, lat_len=lat_len),
        grid=(m // tm,),
        in_specs=[pl.BlockSpec((tm, d), lambda i: (i, 0)),
                  pl.BlockSpec((tm, d), lambda i: (i, 0)),
                  pl.BlockSpec((SUBLANES, d), lambda i: (0, 0)),
                  pl.BlockSpec((None, d, d), lambda i: (layer, 0, 0))],
        out_specs=pl.BlockSpec((tm, d), lambda i: (i, 0)),
        out_shape=jax.ShapeDtypeStruct((m, d), F32),
        compiler_params=_cparams(1),
        name="mixer_out_proj",
    )(y, x, mods, w)


def _final_norm_kernel(x_ref, g_ref, o_ref):
    x = x_ref[...]
    o_ref[...] = x * lax.rsqrt(jnp.mean(x * x, axis=-1, keepdims=True) + RMS_EPS) * g_ref[...]


def _final_norm_call(x, g, row0, rows, tm):
    d = x.shape[1]
    return pl.pallas_call(
        _final_norm_kernel,
        grid=(rows // tm,),
        in_specs=[pl.BlockSpec((tm, d), lambda i: (row0 // tm + i, 0)),
                  pl.BlockSpec((1, d), lambda i: (0, 0))],
        out_specs=pl.BlockSpec((tm, d), lambda i: (i, 0)),
        out_shape=jax.ShapeDtypeStruct((rows, d), F32),
        compiler_params=_cparams(1),
        name="final_rms_norm",
    )(x, g)


def _regroup_kernel(*refs, shift, width, n_blk, src_col, src_rows, src_cols, masked):
    o_ref = refs[-1]
    val = refs[0][...]
    if n_blk > 1:
        val = jnp.concatenate([val, refs[1][...]], axis=1)[:, shift:shift + width]
    if masked:
        row = pl.program_id(1) * val.shape[0] + lax.broadcasted_iota(jnp.int32, val.shape, 0)
        col = src_col + pl.program_id(2) * width + lax.broadcasted_iota(jnp.int32, val.shape, 1)
        val = jnp.where(jnp.logical_and(row < src_rows, col < src_cols), val, 0.0)
    o_ref[...] = val.astype(o_ref.dtype)


def _regroup_call(src, buf, src_col, dst_col, ncols, out_rows, n_total, tcr):
    depth, src_rows, src_cols = src.shape
    q, shift = divmod(src_col, LANES)
    per = tcr // LANES
    assert q % per == 0
    last = pl.cdiv(src_cols, LANES) - 1
    tb = _pick_tile(out_rows, (1024, 512, 256, 128))
    in_specs = [pl.BlockSpec((None, tb, tcr), lambda l, i, t: (l, i, q // per + t))]
    if shift:
        in_specs.append(pl.BlockSpec((None, tb, LANES), lambda l, i, t: (l, i, jnp.minimum(q + per * (t + 1), last))))
    n_blk = len(in_specs)
    operands = [src] * n_blk
    aliases = {}
    if buf is not None:
        in_specs.append(pl.BlockSpec(memory_space=pl.ANY))
        operands.append(buf)
        aliases = {n_blk: 0}
    return pl.pallas_call(
        functools.partial(_regroup_kernel, shift=shift, width=tcr, n_blk=n_blk, src_col=src_col, src_rows=src_rows,
                          src_cols=src_cols, masked=out_rows > src_rows or src_col + ncols > src_cols),
        grid=(depth, out_rows // tb, ncols // tcr),
        in_specs=in_specs,
        out_specs=pl.BlockSpec((None, tb, tcr), lambda l, i, t: (l, i, dst_col // tcr + t)),
        out_shape=jax.ShapeDtypeStruct((depth, out_rows, n_total), BF16),
        input_output_aliases=aliases,
        compiler_params=_cparams(3),
        name="weight_regroup",
    )(*operands)


ROW_HALO = 64


def _regroup_rows_kernel(*refs, shift):
    o_ref = refs[-1]
    val = refs[0][...]
    if shift:
        val = jnp.concatenate([val, refs[1][...]], axis=0)[shift:shift + val.shape[0]]
    o_ref[...] = val.astype(o_ref.dtype)


def _regroup_rows_call(src, buf, src_row, dst_row, nrows, n_total, tr):
    depth, src_rows, d = src.shape
    qb, shift = divmod(src_row, tr)
    assert dst_row % tr == 0 and nrows % tr == 0 and shift % SUBLANES == 0 and shift <= ROW_HALO
    assert tr % ROW_HALO == 0 and src_row + nrows <= src_rows
    last = pl.cdiv(src_rows, ROW_HALO) - 1
    in_specs = [pl.BlockSpec((None, tr, d), lambda l, t: (l, qb + t, 0))]
    if shift:
        in_specs.append(pl.BlockSpec((None, ROW_HALO, d),
                                     lambda l, t: (l, jnp.minimum((qb + t + 1) * (tr // ROW_HALO), last), 0)))
    operands = [src] * len(in_specs)
    aliases = {}
    if buf is not None:
        aliases = {len(in_specs): 0}
        in_specs.append(pl.BlockSpec(memory_space=pl.ANY))
        operands.append(buf)
    return pl.pallas_call(
        functools.partial(_regroup_rows_kernel, shift=shift),
        grid=(depth, nrows // tr),
        in_specs=in_specs,
        out_specs=pl.BlockSpec((None, tr, d), lambda l, t: (l, dst_row // tr + t, 0)),
        out_shape=jax.ShapeDtypeStruct((depth, n_total, d), BF16),
        input_output_aliases=aliases,
        compiler_params=_cparams(2),
        name="weight_row_regroup",
    )(*operands)


def _gate_rows_kernel(*refs, offs, heads):
    o_ref = refs[-1]
    row = lax.broadcasted_iota(jnp.int32, refs[0].shape, 0)
    tiles = []
    for g, (off, h) in enumerate(zip(offs, heads)):
        t8 = jnp.where(jnp.logical_and(row >= off, row < off + h), refs[g][...], 0.0)
        if off:
            t8 = pltpu.roll(t8, SUBLANES - off, axis=0)
        tiles += [t8, jnp.zeros((LANES - SUBLANES, t8.shape[1]), t8.dtype)]
    o_ref[...] = jnp.concatenate(tiles, axis=0).astype(o_ref.dtype)


def _gate_rows_call(src, buf, rows, heads, dst_row):
    depth, _, d = src.shape
    n_total = buf.shape[1]
    height = len(rows) * LANES
    assert dst_row % height == 0 and all(r // SUBLANES == (r + h - 1) // SUBLANES for r, h in zip(rows, heads))
    in_specs = [pl.BlockSpec((None, SUBLANES, d), lambda l, blk=r // SUBLANES: (l, blk, 0)) for r in rows]
    in_specs.append(pl.BlockSpec(memory_space=pl.ANY))
    return pl.pallas_call(
        functools.partial(_gate_rows_kernel, offs=tuple(r % SUBLANES for r in rows), heads=tuple(heads)),
        grid=(depth,),
        in_specs=in_specs,
        out_specs=pl.BlockSpec((None, height, d), lambda l: (l, dst_row // height, 0)),
        out_shape=jax.ShapeDtypeStruct((depth, n_total, d), BF16),
        input_output_aliases={len(rows): 0},
        compiler_params=_cparams(1),
        name="gate_weight_groups",
    )(*([src] * len(rows)), buf)


def _pack_rows(rows, d):
    out = jnp.concatenate([r.reshape(1, d) for r in rows], axis=0)
    return jnp.pad(out, ((0, SUBLANES - out.shape[0]), (0, 0)))


def _lane_row(vals, lanes=LANES):
    return jnp.pad(vals.astype(F32), ((0, 0), (0, lanes - vals.shape[1])))[:, None, :]


def _pick_tile(total, candidates):
    for c in candidates:
        if total % c == 0:
            return c
    raise ValueError(f"no tile for {total} among {candidates}")


def kernel(x, c, ctx, c_ctx, w_ada, b_ada, norm_g, ffn_w_gate, ffn_w_up, ffn_w_down, w_in, gdn_conv, gdn_a_log,
           gdn_dt_bias, gdn_norm, mlstm_i_bias, mlstm_f_bias, mlstm_norm, hy_short_w, hy_short_b, hf_w1, hf_b1,
           hf_w2, hf_b2, hf_w3, hf_freq, hf_decay, hy_bias, w_a_out, w_b_out, w_c_out, w_out, final_norm):
    batch, seq, d = x.shape
    assert batch == 1 and c.shape[0] == 1 and ctx.shape[0] == 1
    ctx_len = ctx.shape[1]
    depth = w_ada.shape[0]
    m = ctx_len + seq
    assert ctx_len % CHUNK == 0 and seq % CHUNK == 0

    gdn_heads = gdn_a_log.shape[-1]
    gdn_vw = w_a_out.shape[1]
    gdn_qkw = (gdn_conv.shape[1] - gdn_vw) // 2
    gdn_dk, gdn_dv = gdn_qkw // gdn_heads, gdn_vw // gdn_heads
    ml_heads = mlstm_i_bias.shape[-1]
    ml_vw = w_b_out.shape[1]
    hy_w = w_c_out.shape[1]
    n_in = w_in.shape[-1]
    ml_qkw = (n_in - (2 * gdn_qkw + gdn_vw) - 4 * gdn_heads - 4 * ml_heads - ml_vw - gdn_vw - ml_vw
              - 3 * hy_w - 3 * d) // 2
    ml_dqk, ml_dv = ml_qkw // ml_heads, ml_vw // ml_heads
    assert gdn_dk == LANES and gdn_dv == LANES and ml_dqk == LANES and ml_dv % LANES == 0

    o_gqkv = 0
    o_gbeta = o_gqkv + 2 * gdn_qkw + gdn_vw
    o_ga = o_gbeta + 2 * gdn_heads
    o_mqkv = o_ga + 2 * gdn_heads
    o_mi = o_mqkv + 2 * ml_qkw + ml_vw
    o_mf = o_mi + 2 * ml_heads
    o_z = o_mf + 2 * ml_heads
    o_o = o_z + gdn_vw
    o_hy = o_o + ml_vw
    o_gates = o_hy + 3 * hy_w
    assert o_gates + 3 * d == n_in

    c_gqkv = 0
    c_mqkv = c_gqkv + 2 * gdn_qkw + gdn_vw
    c_z = c_mqkv + 2 * ml_qkw + ml_vw
    c_o = c_z + gdn_vw
    c_hy = c_o + ml_vw
    c_gates = c_hy + 3 * hy_w
    c_gsmall = c_gates + 3 * d
    c_msmall = c_gsmall + 4 * LANES
    n_p = c_msmall + 4 * LANES

    w_in_t = jnp.swapaxes(w_in, 1, 2)
    moves = [(o_gqkv, c_gqkv, c_mqkv - c_gqkv), (o_mqkv, c_mqkv, c_z - c_mqkv), (o_z, c_z, c_gsmall - c_z)]
    tr = 512 if all(dst % 512 == 0 and width % 512 == 0 for _, dst, width in moves) else 256
    w_in_p = None
    for src_row, dst_row, height in moves:
        w_in_p = _regroup_rows_call(w_in_t, w_in_p, src_row, dst_row, height, n_p, tr)
    gate_cols, gate_heads = [], []
    for offs, heads in (((o_gbeta, o_ga), gdn_heads), ((o_mi, o_mf), ml_heads)):
        for direction in range(2):
            for off in offs:
                gate_cols.append(off + direction * heads)
                gate_heads.append(heads)
    per_call = 1
    while per_call < len(gate_cols) and c_gsmall % (2 * per_call * LANES) == 0:
        per_call *= 2
    for g0 in range(0, len(gate_cols), per_call):
        w_in_p = _gate_rows_call(w_in_t, w_in_p, gate_cols[g0:g0 + per_call], gate_heads[g0:g0 + per_call],
                                 c_gsmall + g0 * LANES)

    f_hidden = ffn_w_gate.shape[-1]
    tf = 512
    f_pad = -(-f_hidden // tf) * tf
    wg = _regroup_call(ffn_w_gate.reshape(2 * depth, d, f_hidden), None, 0, 0, f_pad, d, f_pad, tf)
    wu = _regroup_call(ffn_w_up.reshape(2 * depth, d, f_hidden), None, 0, 0, f_pad, d, f_pad, tf)
    wd = _regroup_call(ffn_w_down.reshape(2 * depth, f_hidden, d), None, 0, 0, d, f_pad, d, _pick_tile(d, (512, 256)))
    wg, wu = wg.reshape(depth, 2, d, f_pad), wu.reshape(depth, 2, d, f_pad)
    wd = wd.reshape(depth, 2, f_pad, d)
    wa, wb, wc, wo = (t.astype(BF16) for t in (w_a_out, w_b_out, w_c_out, w_out))

    tm_big = _pick_tile(m, (768, 512, 384, 256, 128, 64))
    tm_mid = _pick_tile(m, (256, 128, 64))
    tm_proj = _pick_tile(m, (1056, 768, 512, 384, 256, 128, 64))
    tm_seg = _pick_tile(math.gcd(ctx_len, seq), (256, 128, 64))
    tn_in = _pick_tile(n_p, (1024, 512, 256, 128))
    nc_ctx = ctx_len // CHUNK

    cond = jnp.pad(jnp.concatenate([c_ctx[None, :], c], axis=0), ((0, SUBLANES - 2), (0, 0)))
    mods = _modulation_call(cond, w_ada, b_ada)

    dft_consts = _dft_constants(seq)
    dense_fwd, dense_inv = _dense_dft_constants(ctx_len)
    feats_lat = _filter_features(seq)
    feats_lat = feats_lat.reshape(seq // DFT_N2, DFT_N2, LANES).swapaxes(0, 1).reshape(seq, LANES)
    feats_ctx = _filter_features(ctx_len)
    tl = _pick_tile(seq, (512, 256, 128, 64))

    s = jnp.concatenate([x[0], ctx[0]], axis=0)
    for l in range(depth):
        last = l == depth - 1

        def mod(idx):
            return [mods[l, 0, idx * d:(idx + 1) * d], mods[l, 1, idx * d:(idx + 1) * d]]

        s = _ffn_call(s, _pack_rows(mod(0) + mod(1) + mod(2), d), norm_g[l, 0][None, :], wg, wu, wd, l, 0,
                      seq, tm_big, tf)
        p = _inproj_call(s, _pack_rows(mod(3) + mod(4), d), norm_g[l, 1][None, :], w_in_p, l, seq, tm_proj, tn_in)

        gqkv = _dwconv_call(p, c_gqkv, 2 * gdn_qkw + gdn_vw, jnp.transpose(gdn_conv[l]),
                            jnp.zeros((1, 2 * gdn_qkw + gdn_vw), F32), seq, tm_seg, 1024, True)
        hc = _dwconv_call(p, c_hy, 3 * hy_w, jnp.transpose(hy_short_w[l]), hy_short_b[l][None, :],
                          seq, tm_seg, 1024, False)

        og, om = _fused_scans_call(
            m // CHUNK,
            _gdn_scan_parts(gqkv, p, c_gsmall // LANES, _lane_row(gdn_a_log[l]), _lane_row(gdn_dt_bias[l]),
                            gdn_heads, gdn_dk, gdn_dv, nc_ctx),
            _mlstm_scan_parts(p, c_mqkv, c_msmall, _lane_row(mlstm_i_bias[l]), _lane_row(mlstm_f_bias[l]),
                              ml_heads, ml_dqk, ml_dv, nc_ctx))

        fargs = (jnp.pad(hf_w1[l], ((0, LANES - hf_w1.shape[1]), (0, 0))), hf_b1[l][None, :], hf_w2[l],
                 hf_b2[l][None, :], hf_w3[l], hf_freq[l], jnp.tile(hf_decay[l], 2)[None, :])
        taps = _filter_call(feats_lat, *fargs, tl, 1024)
        m1, g2, minv, pinv = dft_consts
        kh = seq // DFT_N2
        ta = _dft1_call(taps.reshape(DFT_N2, kh, taps.shape[1]), 0, taps.shape[1], m1, 1024, n_lo_major=True)
        kf = _filter_spectrum_call(ta.reshape(ta.shape[0] * DFT_N2, taps.shape[1]), g2, 1024)
        hl3 = hc.reshape(m // DFT_N2, DFT_N2, 3 * hy_w)
        tc_h = min(hy_w, 1024)
        per = hy_w // tc_h
        z1 = _long_conv(hl3, 0, False, hl3, per, hy_bias[l, 0][None, :], kf, 0, dft_consts, hy_w, tc_h, True)
        z2 = _long_conv(z1, 0, True, hl3, 2 * per, hy_bias[l, 1][None, :], kf, per, dft_consts, hy_w, tc_h, False)
        z2 = z2.reshape(seq, hy_w)
        if not last:
            taps_c = _filter_call(feats_ctx, *fargs, _pick_tile(ctx_len, (256, 128, 64)), 1024)
            kf_c = _dense_spectrum_call(taps_c, dense_fwd, 512)
            zc = _dense_hyena_call(hc[seq:], ctx_len, hy_w, kf_c, dense_fwd, dense_inv, hy_bias[l, 0][None, :],
                                   hy_bias[l, 1][None, :], 256)
        else:
            zc = jnp.zeros((ctx_len, hy_w), F32)
        hy = jnp.concatenate([z2, zc], axis=0)

        merged = _merge_call(og, om, p, c_z, c_o, hy, c_gates, jnp.tile(gdn_norm[l], gdn_heads)[None, :],
                             mlstm_norm[l][None, :], wa, wb, wc, l, (gdn_heads, gdn_dv, ml_heads, ml_dv), tm_mid)
        s = _outproj_call(merged, s, _pack_rows(mod(5), d), wo, l, seq, tm_mid)
        s = _ffn_call(s, _pack_rows(mod(6) + mod(7) + mod(8), d), norm_g[l, 2][None, :], wg, wu, wd, l, 1,
                      seq, tm_big, tf)

    out = _final_norm_call(s, final_norm[None, :], 0, seq, tm_seg)
    return out[None]
```

```python
import functools
import math

import jax
import jax.numpy as jnp
import numpy as np
from jax import lax
from jax.experimental import pallas as pl
from jax.experimental.pallas import tpu as pltpu

F32 = jnp.float32
BF16 = jnp.bfloat16

N_MOD = 9
RMS_EPS = 1e-6
L2_EPS = 1e-6
CHUNK = 64
GDN_CONV = 5
HYENA_SHORT = 3
HYENA_BANDS = 16
LANES = 128
SUBLANES = 8
DFT_N2 = 128
VMEM_LIMIT = 56 * 1024 * 1024


def _cparams(n_axes):
    return pltpu.CompilerParams(dimension_semantics=("arbitrary",) * n_axes,
                                vmem_limit_bytes=VMEM_LIMIT)


def _sigmoid(x):
    return jax.nn.sigmoid(x)


def _silu(x):
    return x * jax.nn.sigmoid(x)


def _softplus(x):
    return jnp.maximum(x, 0.0) + jnp.log(1.0 + jnp.exp(-jnp.abs(x)))


def _dot(a, b):
    return jnp.dot(a.astype(BF16), b.astype(BF16), preferred_element_type=F32)


def _dot_nt(a, b):
    return lax.dot_general(a.astype(BF16), b.astype(BF16), (((1,), (1,)), ((), ())),
                           preferred_element_type=F32)


def _dot_tn(a, b):
    return lax.dot_general(a.astype(BF16), b.astype(BF16), (((0,), (0,)), ((), ())),
                           preferred_element_type=F32)


def _dot_exact(a, b):
    return jnp.dot(a, b, preferred_element_type=F32, precision=lax.Precision.HIGHEST)


def _row_is_ctx(row0, rows, lat_len):
    r = row0 + lax.broadcasted_iota(jnp.int32, (rows, 1), 0)
    return r >= lat_len


def _modnorm(x, g, shift, scale):
    y = x * lax.rsqrt(jnp.mean(x * x, axis=-1, keepdims=True) + RMS_EPS) * g
    return y * (1.0 + scale) + shift


ROW_CHUNK = 128


def _for_row_chunks(tm, fn):
    rc = math.gcd(tm, ROW_CHUNK)

    def body(c, carry):
        fn(pl.multiple_of(c * rc, rc), rc)
        return carry

    lax.fori_loop(0, tm // rc, body, 0)


def _store_modulated(x_ref, m_ref, g_ref, h_sc, row0, tm, lat_len):
    def chunk(r0, rc):
        is_ctx = _row_is_ctx(row0 + r0, rc, lat_len)
        shift = jnp.where(is_ctx, m_ref[0:1, :], m_ref[1:2, :])
        scale = jnp.where(is_ctx, m_ref[2:3, :], m_ref[3:4, :])
        h_sc[pl.ds(r0, rc), :] = _modnorm(x_ref[pl.ds(r0, rc), :], g_ref[...], shift, scale).astype(h_sc.dtype)

    _for_row_chunks(tm, chunk)


def _mod_kernel(c_ref, w_ref, b_ref, o_ref):
    o_ref[0] = _dot(_silu(c_ref[...]), w_ref[0]) + b_ref[0]


def _modulation_call(cond, w_ada, b_ada):
    depth, d, n = w_ada.shape
    tn = _pick_tile(n, (1024, 512, 256, 128))
    return pl.pallas_call(
        _mod_kernel,
        grid=(depth, n // tn),
        in_specs=[pl.BlockSpec((SUBLANES, d), lambda l, j: (0, 0)),
                  pl.BlockSpec((1, d, tn), lambda l, j: (l, 0, j)),
                  pl.BlockSpec((1, 1, tn), lambda l, j: (l, 0, j))],
        out_specs=pl.BlockSpec((1, SUBLANES, tn), lambda l, j: (l, 0, j)),
        out_shape=jax.ShapeDtypeStruct((depth, SUBLANES, n), F32),
        compiler_params=_cparams(2),
        name="adaln_modulation",
    )(cond, w_ada, b_ada.reshape(depth, 1, n))


def _ffn_kernel(x_ref, m_ref, g_ref, wg_ref, wu_ref, wd_ref, o_ref, h_sc, acc_sc, *, tm, lat_len):
    row0 = pl.program_id(0) * tm
    f = pl.program_id(1)

    @pl.when(f == 0)
    def _():
        _store_modulated(x_ref, m_ref, g_ref, h_sc, row0, tm, lat_len)
        acc_sc[...] = jnp.zeros_like(acc_sc)

    h = h_sc[...]
    a = _silu(_dot(h, wg_ref[...])) * _dot(h, wu_ref[...])
    acc_sc[...] += _dot(a, wd_ref[...])

    @pl.when(f == pl.num_programs(1) - 1)
    def _():
        def chunk(r0, rc):
            gate = jnp.where(_row_is_ctx(row0 + r0, rc, lat_len), m_ref[4:5, :], m_ref[5:6, :])
            rows = pl.ds(r0, rc)
            o_ref[rows, :] = x_ref[rows, :] + 0.5 * gate * acc_sc[rows, :]

        _for_row_chunks(tm, chunk)


def _ffn_call(x, mods, g, wg, wu, wd, layer, which, lat_len, tm, tf):
    m, d = x.shape
    fp = wg.shape[-1]
    return pl.pallas_call(
        functools.partial(_ffn_kernel, tm=tm, lat_len=lat_len),
        grid=(m // tm, fp // tf),
        in_specs=[pl.BlockSpec((tm, d), lambda i, f: (i, 0)),
                  pl.BlockSpec((SUBLANES, d), lambda i, f: (0, 0)),
                  pl.BlockSpec((1, d), lambda i, f: (0, 0)),
                  pl.BlockSpec((None, None, d, tf), lambda i, f: (layer, which, 0, f)),
                  pl.BlockSpec((None, None, d, tf), lambda i, f: (layer, which, 0, f)),
                  pl.BlockSpec((None, None, tf, d), lambda i, f: (layer, which, f, 0))],
        out_specs=pl.BlockSpec((tm, d), lambda i, f: (i, 0)),
        out_shape=jax.ShapeDtypeStruct((m, d), F32),
        scratch_shapes=[pltpu.VMEM((tm, d), BF16), pltpu.VMEM((tm, d), F32)],
        compiler_params=_cparams(2),
        name="macaron_swiglu",
    )(x, mods, g, wg, wu, wd)


def _inproj_kernel(x_ref, m_ref, g_ref, w_ref, o_ref, h_sc, *, tm, lat_len):
    @pl.when(pl.program_id(1) == 0)
    def _():
        _store_modulated(x_ref, m_ref, g_ref, h_sc, pl.program_id(0) * tm, tm, lat_len)

    o_ref[...] = _dot_nt(h_sc[...], w_ref[...])


def _inproj_call(x, mods, g, w, layer, lat_len, tm, tn):
    m, d = x.shape
    n = w.shape[1]
    return pl.pallas_call(
        functools.partial(_inproj_kernel, tm=tm, lat_len=lat_len),
        grid=(m // tm, n // tn),
        in_specs=[pl.BlockSpec((tm, d), lambda i, j: (i, 0)),
                  pl.BlockSpec((SUBLANES, d), lambda i, j: (0, 0)),
                  pl.BlockSpec((1, d), lambda i, j: (0, 0)),
                  pl.BlockSpec((None, tn, d), lambda i, j: (layer, j, 0))],
        out_specs=pl.BlockSpec((tm, tn), lambda i, j: (i, j)),
        out_shape=jax.ShapeDtypeStruct((m, n), F32),
        scratch_shapes=[pltpu.VMEM((tm, d), BF16)],
        compiler_params=_cparams(2),
        name="mixer_in_proj",
    )(x, mods, g, w)


def _dwconv_kernel(prev_ref, cur_ref, next_ref, w_ref, b_ref, o_ref, ext_sc, *, tm, taps, split, m_rows, act):
    row0 = pl.program_id(0) * tm
    at_start = jnp.logical_or(row0 == 0, row0 == split)
    at_end = jnp.logical_or(row0 + tm == split, row0 + tm == m_rows)
    ext_sc[0:SUBLANES, :] = jnp.where(at_start, 0.0, prev_ref[...])
    ext_sc[SUBLANES:SUBLANES + tm, :] = cur_ref[...]
    ext_sc[SUBLANES + tm:2 * SUBLANES + tm, :] = jnp.where(at_end, 0.0, next_ref[...])
    acc = jnp.zeros(o_ref.shape, F32) + b_ref[...]
    for t in range(taps):
        acc = acc + ext_sc[pl.ds(SUBLANES + t - taps // 2, tm), :] * w_ref[t:t + 1, :]
    o_ref[...] = _silu(acc) if act else acc


def _dwconv_call(p, col0, ncols, w_t, b, split, tm, tc, act):
    m = p.shape[0]
    taps = w_t.shape[0]
    assert split % tm == 0 and m % tm == 0 and col0 % tc == 0 and taps // 2 <= SUBLANES
    cb0 = col0 // tc
    rb = tm // SUBLANES
    last = m // SUBLANES - 1
    return pl.pallas_call(
        functools.partial(_dwconv_kernel, tm=tm, taps=taps, split=split, m_rows=m, act=act),
        grid=(m // tm, ncols // tc),
        in_specs=[pl.BlockSpec((SUBLANES, tc), lambda i, j: (jnp.maximum(i * rb - 1, 0), cb0 + j)),
                  pl.BlockSpec((tm, tc), lambda i, j: (i, cb0 + j)),
                  pl.BlockSpec((SUBLANES, tc), lambda i, j: (jnp.minimum((i + 1) * rb, last), cb0 + j)),
                  pl.BlockSpec((taps, tc), lambda i, j: (0, j)),
                  pl.BlockSpec((1, tc), lambda i, j: (0, j))],
        out_specs=pl.BlockSpec((tm, tc), lambda i, j: (i, j)),
        out_shape=jax.ShapeDtypeStruct((m, ncols), F32),
        scratch_shapes=[pltpu.VMEM((tm + 2 * SUBLANES, tc), F32)],
        compiler_params=_cparams(2),
        name="depthwise_conv",
    )(p, p, p, w_t, b)


def _fwd_chunk(s, nc_ctx, nc_tot):
    return jnp.where(s < nc_ctx, nc_tot - nc_ctx + s, s - nc_ctx)


def _bwd_chunk(s, nc_ctx, nc_tot):
    del nc_ctx
    return nc_tot - 1 - s


def _causal_masks(direction):
    ii = lax.broadcasted_iota(jnp.int32, (CHUNK, CHUNK), 0)
    jj = lax.broadcasted_iota(jnp.int32, (CHUNK, CHUNK), 1)
    rel = ii - jj if direction == 0 else jj - ii
    return rel >= 0, rel > 0


def _unit_triangular_inverses(mats):
    ii = lax.broadcasted_iota(jnp.int32, (CHUNK, CHUNK), 0)
    jj = lax.broadcasted_iota(jnp.int32, (CHUNK, CHUNK), 1)
    eye = (ii == jj).astype(F32)

    def same_block(width):
        return (ii // width) == (jj // width)

    inner = same_block(8)
    d1 = [jnp.where(inner, a, 0.0) for a in mats]
    d2 = [_dot(x, x) for x in d1]
    yield
    d4 = [_dot(x, x) for x in d2]
    p = [_dot(eye - x, eye + y) for x, y in zip(d1, d2)]
    yield
    p = [_dot(x, eye + y) for x, y in zip(p, d4)]
    yield
    for width in (16, 32, 64):
        outer = same_block(width)
        ring = jnp.logical_and(outer, jnp.logical_not(inner))
        t = [_dot(x, jnp.where(ring, a, 0.0)) for x, a in zip(p, mats)]
        yield
        p = [x - _dot(y, x) for x, y in zip(p, t)]
        yield
        inner = outer
    return p


def _gdn_scan_kernel(qf_ref, kf_ref, vf_ref, btf_ref, af_ref, qb_ref, kb_ref, vb_ref, btb_ref, ab_ref,
                     alog_ref, dtb_ref, of_ref, ob_ref, s_sc, *, heads, dk, dv):
    @pl.when(pl.program_id(0) == 0)
    def _():
        s_sc[...] = jnp.zeros_like(s_sc)

    chains = []
    for direction, (q_ref, k_ref, v_ref, bt_ref, a_ref, o_ref) in enumerate(
            ((qf_ref, kf_ref, vf_ref, btf_ref, af_ref, of_ref), (qb_ref, kb_ref, vb_ref, btb_ref, ab_ref, ob_ref))):
        incl, strict = _causal_masks(direction)
        beta = _sigmoid(bt_ref[...])
        glog = -jnp.exp(alog_ref[direction]) * _softplus(a_ref[...] + dtb_ref[direction])
        gcum = _dot_exact(incl.astype(F32), glog)
        gtot = gcum[CHUNK - 1:CHUNK, :] if direction == 0 else gcum[0:1, :]
        e_in = jnp.exp(gcum)
        e_out = jnp.exp(gtot - gcum)
        g_end = jnp.exp(gtot)
        gcum_t = gcum.T
        for h in range(heads):
            q = q_ref[:, h * dk:(h + 1) * dk]
            k = k_ref[:, h * dk:(h + 1) * dk]
            q = q * lax.rsqrt(jnp.sum(q * q, axis=-1, keepdims=True) + L2_EPS) * (dk ** -0.5)
            k = k * lax.rsqrt(jnp.sum(k * k, axis=-1, keepdims=True) + L2_EPS)
            b_col = beta[:, h:h + 1]
            kb = k * b_col
            chains.append(dict(
                q=q, k=k, kb=kb, strict=strict, o_ref=o_ref, h=h, slot=direction * heads + h,
                decay=jnp.exp(jnp.where(incl, gcum[:, h:h + 1] - gcum_t[h:h + 1, :], -jnp.inf)),
                rhs=jnp.concatenate([v_ref[:, h * dv:(h + 1) * dv] * b_col, kb * e_in[:, h:h + 1]], axis=1),
                q_in=q * e_in[:, h:h + 1], k_out=k * e_out[:, h:h + 1], g_end=g_end[:, h:h + 1]))

    yield
    a_mats = [jnp.where(c["strict"], _dot_nt(c["kb"], c["k"]) * c["decay"], 0.0) for c in chains]
    qk = [_dot_nt(c["q"], c["k"]) * c["decay"] for c in chains]
    yield
    t_inv = yield from _unit_triangular_inverses(a_mats)
    sol = [_dot(t, c["rhs"]) for t, c in zip(t_inv, chains)]
    yield
    states = [s_sc[c["slot"]] for c in chains]
    v_new = [x[:, :dv] - _dot(x[:, dv:], st) for x, st in zip(sol, states)]
    o_inter = [_dot(c["q_in"], st) for c, st in zip(chains, states)]
    yield
    o_intra = [_dot(x, y) for x, y in zip(qk, v_new)]
    s_upd = [_dot_tn(c["k_out"], y) for c, y in zip(chains, v_new)]
    yield
    for c, st, x, y, z in zip(chains, states, o_inter, o_intra, s_upd):
        c["o_ref"][:, c["h"] * dv:(c["h"] + 1) * dv] = x + y
        s_sc[c["slot"]] = st * c["g_end"] + z


def _gdn_scan_parts(qkv, p, beta_cb, alog, dtb, heads, dk, dv, nc_ctx):
    m = qkv.shape[0]
    nc = m // CHUNK
    qw, vw = heads * dk, heads * dv
    assert qw == vw
    bidx = functools.partial(_bwd_chunk, nc_ctx=nc_ctx, nc_tot=nc)

    def specs(row):
        return [pl.BlockSpec((CHUNK, qw), lambda s: (row(s), 0)),
                pl.BlockSpec((CHUNK, qw), lambda s: (row(s), 1)),
                pl.BlockSpec((CHUNK, vw), lambda s: (row(s), 2))]

    def gate_specs(row, direction):
        return [pl.BlockSpec((CHUNK, LANES), lambda s: (row(s), beta_cb + 2 * direction)),
                pl.BlockSpec((CHUNK, LANES), lambda s: (row(s), beta_cb + 2 * direction + 1))]

    fwd = functools.partial(_fwd_chunk, nc_ctx=nc_ctx, nc_tot=nc)
    return dict(
        kernel=functools.partial(_gdn_scan_kernel, heads=heads, dk=dk, dv=dv),
        in_specs=(specs(fwd) + gate_specs(fwd, 0) + specs(bidx) + gate_specs(bidx, 1)
                  + [pl.BlockSpec((2, 1, LANES), lambda s: (0, 0, 0)),
                     pl.BlockSpec((2, 1, LANES), lambda s: (0, 0, 0))]),
        operands=(qkv, qkv, qkv, p, p, qkv, qkv, qkv, p, p, alog, dtb),
        out_specs=[pl.BlockSpec((CHUNK, vw), lambda s: (fwd(s), 0)),
                   pl.BlockSpec((CHUNK, vw), lambda s: (bidx(s), 0))],
        out_shape=[jax.ShapeDtypeStruct((m, vw), F32)] * 2,
        scratch_shapes=[pltpu.VMEM((2 * heads, dk, dv), F32)])


def _mlstm_scan_kernel(qf_ref, kf_ref, vf_ref, if_ref, ff_ref, qb_ref, kb_ref, vb_ref, ib_ref, fb_ref,
                       ibias_ref, fbias_ref, of_ref, ob_ref, c_sc, m_sc, *, heads, dqk, dv):
    @pl.when(pl.program_id(0) == 0)
    def _():
        c_sc[...] = jnp.zeros_like(c_sc)
        m_sc[...] = jnp.zeros_like(m_sc)

    ones_col = (lax.broadcasted_iota(jnp.int32, (CHUNK, LANES), 1) == 0).astype(F32)
    chains = []
    for direction, (q_ref, k_ref, v_ref, i_ref, f_ref, o_ref) in enumerate(
            ((qf_ref, kf_ref, vf_ref, if_ref, ff_ref, of_ref), (qb_ref, kb_ref, vb_ref, ib_ref, fb_ref, ob_ref))):
        incl, _ = _causal_masks(direction)
        log_i = i_ref[...] + ibias_ref[direction]
        log_f = -_softplus(-(f_ref[...] + fbias_ref[direction]))
        bcum = _dot_exact(incl.astype(F32), log_f)
        btot = bcum[CHUNK - 1:CHUNK, :] if direction == 0 else bcum[0:1, :]
        log_end = btot - bcum + log_i
        m_st = m_sc[direction]
        m_new = jnp.maximum(btot + m_st, jnp.max(log_end, axis=0, keepdims=True))
        m_sc[direction] = m_new
        carry = jnp.exp(btot + m_st - m_new)
        k_scale = jnp.exp(log_end - m_new)
        b_inter = bcum + m_st
        bcum_t = bcum.T
        log_i_t = log_i.T
        for h in range(heads):
            log_d = jnp.where(incl, bcum[:, h:h + 1] - bcum_t[h:h + 1, :] + log_i_t[h:h + 1, :], -jnp.inf)
            m_t = jnp.maximum(b_inter[:, h:h + 1], jnp.max(log_d, axis=-1, keepdims=True))
            chains.append(dict(
                q=q_ref[:, h * dqk:(h + 1) * dqk] * (dqk ** -0.5), k=k_ref[:, h * dqk:(h + 1) * dqk],
                v_ext=jnp.concatenate([v_ref[:, h * dv:(h + 1) * dv], ones_col], axis=1),
                p_intra=jnp.exp(log_d - m_t), w_inter=jnp.exp(b_inter[:, h:h + 1] - m_t), floor=jnp.exp(-m_t),
                k_scale=k_scale[:, h:h + 1], carry=carry[:, h:h + 1], o_ref=o_ref, h=h, slot=direction * heads + h))
        yield

    s = [_dot_nt(c["q"], c["k"]) * c["p_intra"] for c in chains]
    yield
    states = [c_sc[c["slot"]] for c in chains]
    inter = [_dot(c["q"], st) for c, st in zip(chains, states)]
    yield
    intra = [_dot(x, c["v_ext"]) for x, c in zip(s, chains)]
    yield
    upd = [_dot_tn(c["k"] * c["k_scale"], c["v_ext"]) for c in chains]
    yield
    for idx, (c, st, x, y, z) in enumerate(zip(chains, states, inter, intra, upd)):
        out = c["w_inter"] * x + y
        den = jnp.maximum(jnp.abs(out[:, dv:dv + 1]), c["floor"])
        c["o_ref"][:, c["h"] * dv:(c["h"] + 1) * dv] = out[:, :dv] / den
        c_sc[c["slot"]] = c["carry"] * st + z
        if idx % 2 == 1:
            yield


def _mlstm_scan_parts(p, q_cb, gate_cb, ib, fb, heads, dqk, dv, nc_ctx):
    m = p.shape[0]
    nc = m // CHUNK
    qw, vw = heads * dqk, heads * dv
    assert q_cb % qw == 0 and (q_cb + 2 * qw) % vw == 0
    bidx = functools.partial(_bwd_chunk, nc_ctx=nc_ctx, nc_tot=nc)
    fwd = functools.partial(_fwd_chunk, nc_ctx=nc_ctx, nc_tot=nc)
    qb, kb, vb = q_cb // qw, q_cb // qw + 1, (q_cb + 2 * qw) // vw
    gb = gate_cb // LANES

    def specs(row, direction):
        return [pl.BlockSpec((CHUNK, qw), lambda s: (row(s), qb)),
                pl.BlockSpec((CHUNK, qw), lambda s: (row(s), kb)),
                pl.BlockSpec((CHUNK, vw), lambda s: (row(s), vb)),
                pl.BlockSpec((CHUNK, LANES), lambda s: (row(s), gb + 2 * direction)),
                pl.BlockSpec((CHUNK, LANES), lambda s: (row(s), gb + 2 * direction + 1))]

    return dict(
        kernel=functools.partial(_mlstm_scan_kernel, heads=heads, dqk=dqk, dv=dv),
        in_specs=(specs(fwd, 0) + specs(bidx, 1)
                  + [pl.BlockSpec((2, 1, LANES), lambda s: (0, 0, 0)),
                     pl.BlockSpec((2, 1, LANES), lambda s: (0, 0, 0))]),
        operands=(p, p, p, p, p, p, p, p, p, p, ib, fb),
        out_specs=[pl.BlockSpec((CHUNK, vw), lambda s: (fwd(s), 0)),
                   pl.BlockSpec((CHUNK, vw), lambda s: (bidx(s), 0))],
        out_shape=[jax.ShapeDtypeStruct((m, vw), F32)] * 2,
        scratch_shapes=[pltpu.VMEM((2 * heads, dqk, dv + LANES), F32), pltpu.VMEM((2, 1, LANES), F32)])


def _fused_scans_kernel(*refs, parts):
    n_in = sum(part[1] for part in parts)
    n_out = sum(part[2] for part in parts)
    i0, o0, s0 = 0, n_in, n_in + n_out
    running = []
    for fn, ni, no, ns in parts:
        running.append(fn(*refs[i0:i0 + ni], *refs[o0:o0 + no], *refs[s0:s0 + ns]))
        i0, o0, s0 = i0 + ni, o0 + no, s0 + ns
    done = object()
    while running:
        running = [g for g in running if next(g, done) is not done]


def _fused_scans_call(nc, *scans):
    parts = tuple((sc["kernel"], len(sc["in_specs"]), len(sc["out_specs"]), len(sc["scratch_shapes"]))
                  for sc in scans)
    outs = pl.pallas_call(
        functools.partial(_fused_scans_kernel, parts=parts),
        grid=(nc,),
        in_specs=sum((list(sc["in_specs"]) for sc in scans), []),
        out_specs=sum((list(sc["out_specs"]) for sc in scans), []),
        out_shape=sum((list(sc["out_shape"]) for sc in scans), []),
        scratch_shapes=sum((list(sc["scratch_shapes"]) for sc in scans), []),
        compiler_params=_cparams(1),
        name="gdn_mlstm_scans",
    )(*sum((list(sc["operands"]) for sc in scans), []))
    split, res = 0, []
    for sc in scans:
        res.append(tuple(outs[split:split + len(sc["out_specs"])]))
        split += len(sc["out_specs"])
    return res


def _filter_kernel(feat_ref, w1_ref, b1_ref, w2_ref, b2_ref, w3_ref, freq_ref, dec_ref, o_ref, h_sc):
    @pl.when(pl.program_id(1) == 0)
    def _():
        h = jnp.sin(freq_ref[0:1, :] * (_dot(feat_ref[...], w1_ref[...]) + b1_ref[...]))
        h_sc[...] = jnp.sin(freq_ref[1:2, :] * (_dot(h, w2_ref[...]) + b2_ref[...]))

    o_ref[...] = _dot(h_sc[...], w3_ref[...]) * jnp.exp(-feat_ref[:, 0:1] * jnp.abs(dec_ref[...]))


def _filter_call(feats, w1, b1, w2, b2, w3, freq, dec, tl, tn):
    length, fp = feats.shape
    hid = w2.shape[0]
    n = w3.shape[1]
    return pl.pallas_call(
        _filter_kernel,
        grid=(length // tl, n // tn),
        in_specs=[pl.BlockSpec((tl, fp), lambda i, j: (i, 0)),
                  pl.BlockSpec((fp, hid), lambda i, j: (0, 0)),
                  pl.BlockSpec((1, hid), lambda i, j: (0, 0)),
                  pl.BlockSpec((hid, hid), lambda i, j: (0, 0)),
                  pl.BlockSpec((1, hid), lambda i, j: (0, 0)),
                  pl.BlockSpec((hid, tn), lambda i, j: (0, j)),
                  pl.BlockSpec((2, hid), lambda i, j: (0, 0)),
                  pl.BlockSpec((1, tn), lambda i, j: (0, j))],
        out_specs=pl.BlockSpec((tl, tn), lambda i, j: (i, j)),
        out_shape=jax.ShapeDtypeStruct((length, n), F32),
        scratch_shapes=[pltpu.VMEM((tl, hid), F32)],
        compiler_params=_cparams(2),
        name="hyena_filter",
    )(feats, w1, b1, w2, b2, w3, freq, dec)


def _dft_constants(length):
    n2 = DFT_N2
    n = 2 * length
    n1 = n // n2
    kh = n1 // 2
    na = kh + 1

    def cis(num, den):
        ang = (2.0 * math.pi / den) * (num % den).astype(F32)
        return jnp.cos(ang), jnp.sin(ang)

    ar = lambda size: jnp.arange(size, dtype=jnp.int32)
    c, s = cis(ar(na)[None, :, None] * (n2 * ar(kh)[None, None, :] + ar(n2)[:, None, None]), n)
    m1 = jnp.stack([c, -s], axis=2).reshape(n2, 2 * na, kh)
    c, s = cis(ar(n2)[:, None] * ar(n2)[None, :], n2)
    g2 = jnp.block([[c, s], [-s, c]])
    c, s = cis(ar(n2)[None, :, None] * (ar(na)[:, None, None] + n1 * ar(n2)[None, None, :]), n)
    minv = jnp.concatenate([jnp.concatenate([c, -s], axis=2), jnp.concatenate([s, c], axis=2)], axis=1)
    c, s = cis(ar(kh)[:, None] * ar(na)[None, :], n1)
    weight = jnp.where(jnp.logical_or(ar(na) == 0, ar(na) == kh), 1.0, 2.0)[None, :] / n
    pinv = jnp.stack([c * weight, -s * weight], axis=2).reshape(kh, 2 * na)
    return m1.astype(BF16), g2.astype(BF16), minv.astype(BF16), pinv.astype(BF16)


def _dft1_kernel(x_ref, m_ref, o_ref, *, n_lo_major):
    for j in range(SUBLANES):
        o_ref[:, j, :] = _dot(m_ref[j], x_ref[j] if n_lo_major else x_ref[:, j, :])


def _dft1_call(x3, col_blk0, width, m1, tc, n_lo_major=False):
    n2, rows, kh = m1.shape
    if n_lo_major:
        x_spec = pl.BlockSpec((SUBLANES, kh, tc), lambda c, g: (g, 0, col_blk0 + c))
    else:
        x_spec = pl.BlockSpec((kh, SUBLANES, tc), lambda c, g: (0, g, col_blk0 + c))
    return pl.pallas_call(
        functools.partial(_dft1_kernel, n_lo_major=n_lo_major),
        grid=(width // tc, n2 // SUBLANES),
        in_specs=[x_spec,
                  pl.BlockSpec((SUBLANES, rows, kh), lambda c, g: (g, 0, 0))],
        out_specs=pl.BlockSpec((rows, SUBLANES, tc), lambda c, g: (0, g, c)),
        out_shape=jax.ShapeDtypeStruct((rows, n2, width), F32),
        compiler_params=_cparams(2),
        name="dft_stage1",
    )(x3, m1)


def _filter_spectrum_kernel(af_ref, ab_ref, g_ref, o_ref):
    sf = _dot(g_ref[...], af_ref[...])
    sb = _dot(g_ref[...], ab_ref[...])
    sign = jnp.where(lax.broadcasted_iota(jnp.int32, (2 * DFT_N2, 1), 0) < DFT_N2, 1.0, -1.0)
    o_ref[...] = sf + sign * sb


def _filter_spectrum_call(a, g2, tc):
    rows, width = a.shape
    half = width // 2
    blk = 2 * DFT_N2
    return pl.pallas_call(
        _filter_spectrum_kernel,
        grid=(rows // blk, half // tc),
        in_specs=[pl.BlockSpec((blk, tc), lambda a_, j: (a_, j)),
                  pl.BlockSpec((blk, tc), lambda a_, j: (a_, half // tc + j)),
                  pl.BlockSpec((blk, blk), lambda a_, j: (0, 0))],
        out_specs=pl.BlockSpec((blk, tc), lambda a_, j: (a_, j)),
        out_shape=jax.ShapeDtypeStruct((rows, half), F32),
        compiler_params=_cparams(2),
        name="hyena_filter_spectrum",
    )(a, a, g2)


def _complex_mul(x, kf, n):
    xr, xi = x[:n], x[n:]
    kr, ki = kf[:n], kf[n:]
    return jnp.concatenate([xr * kr - xi * ki, xr * ki + xi * kr], axis=0)


def _spectral_mid_kernel(a_ref, g_ref, kf_ref, minv_ref, o_ref):
    x = _dot(g_ref[...], a_ref[...])
    o_ref[...] = _dot(minv_ref[0], _complex_mul(x, kf_ref[...], DFT_N2))


def _spectral_mid_call(a, g2, kf, kf_cb, minv, tc):
    rows, width = a.shape
    blk = 2 * DFT_N2
    return pl.pallas_call(
        _spectral_mid_kernel,
        grid=(rows // blk, width // tc),
        in_specs=[pl.BlockSpec((blk, tc), lambda a_, j: (a_, j)),
                  pl.BlockSpec((blk, blk), lambda a_, j: (0, 0)),
                  pl.BlockSpec((blk, tc), lambda a_, j: (a_, kf_cb + j)),
                  pl.BlockSpec((1, blk, blk), lambda a_, j: (a_, 0, 0))],
        out_specs=pl.BlockSpec((blk, tc), lambda a_, j: (a_, j)),
        out_shape=jax.ShapeDtypeStruct((rows, width), F32),
        compiler_params=_cparams(2),
        name="hyena_spectral_mid",
    )(a, g2, kf, minv)


def _idft2_kernel(z_ref, p_ref, v_ref, gate_ref, bias_ref, o_ref, *, v_major, out_major):
    for j in range(SUBLANES):
        y = _dot(p_ref[...], z_ref[:, j, :])
        v = v_ref[j] if v_major else v_ref[:, j, :]
        res = gate_ref[:, j, :] * (y + v * bias_ref[...])
        if out_major:
            o_ref[j] = res
        else:
            o_ref[:, j, :] = res


def _idft2_call(z3, pinv, v3, v_cb, v_major, gate3, gate_cb, bias, width, tc, out_major):
    rows, n2, _ = z3.shape
    kh = pinv.shape[0]

    def time_spec(major, cb):
        if major:
            return pl.BlockSpec((SUBLANES, kh, tc), lambda c, g: (g, 0, cb + c))
        return pl.BlockSpec((kh, SUBLANES, tc), lambda c, g: (0, g, cb + c))

    return pl.pallas_call(
        functools.partial(_idft2_kernel, v_major=v_major, out_major=out_major),
        grid=(width // tc, n2 // SUBLANES),
        in_specs=[pl.BlockSpec((rows, SUBLANES, tc), lambda c, g: (0, g, c)),
                  pl.BlockSpec((kh, rows), lambda c, g: (0, 0)),
                  time_spec(v_major, v_cb),
                  time_spec(False, gate_cb),
                  pl.BlockSpec((1, tc), lambda c, g: (0, c))],
        out_specs=time_spec(out_major, 0),
        out_shape=jax.ShapeDtypeStruct((n2, kh, width) if out_major else (kh, n2, width), F32),
        compiler_params=_cparams(2),
        name="idft_stage2_gate",
    )(z3, pinv, v3, gate3, bias)


def _long_conv(v3, v_cb, v_major, gate3, gate_cb, bias, kf, kf_cb, consts, width, tc, out_major):
    m1, g2, minv, pinv = consts
    a = _dft1_call(v3, v_cb, width, m1, tc, n_lo_major=v_major)
    rows = a.shape[0]
    z = _spectral_mid_call(a.reshape(rows * DFT_N2, width), g2, kf, kf_cb, minv, tc)
    return _idft2_call(z.reshape(rows, DFT_N2, width), pinv, v3, v_cb, v_major, gate3, gate_cb, bias, width, tc,
                       out_major)


def _dense_dft_constants(length):
    n = 2 * length
    k = jnp.arange(n, dtype=jnp.int32)[:, None]
    t = jnp.arange(length, dtype=jnp.int32)[None, :]
    ang = (2.0 * math.pi / n) * ((k * t) % n).astype(F32)
    c, s = jnp.cos(ang), jnp.sin(ang)
    fwd = jnp.concatenate([c, -s], axis=0)
    inv = jnp.concatenate([c.T, -s.T], axis=1) / n
    return fwd.astype(BF16), inv.astype(BF16)


def _dense_spectrum_kernel(tf_ref, tb_ref, f_ref, o_ref, *, n):
    sign = jnp.where(lax.broadcasted_iota(jnp.int32, (2 * n, 1), 0) < n, 1.0, -1.0)
    o_ref[...] = _dot(f_ref[...], tf_ref[...]) + sign * _dot(f_ref[...], tb_ref[...])


def _dense_spectrum_call(taps, fwd, tc):
    length, width = taps.shape
    half = width // 2
    rows = fwd.shape[0]
    return pl.pallas_call(
        functools.partial(_dense_spectrum_kernel, n=rows // 2),
        grid=(half // tc,),
        in_specs=[pl.BlockSpec((length, tc), lambda j: (0, j)),
                  pl.BlockSpec((length, tc), lambda j: (0, half // tc + j)),
                  pl.BlockSpec((rows, length), lambda j: (0, 0))],
        out_specs=pl.BlockSpec((rows, tc), lambda j: (0, j)),
        out_shape=jax.ShapeDtypeStruct((rows, half), F32),
        compiler_params=_cparams(1),
        name="hyena_filter_spectrum_dense",
    )(taps, taps, fwd)


def _dense_hyena_kernel(v_ref, x1_ref, x2_ref, kf0_ref, kf1_ref, f_ref, inv_ref, b0_ref, b1_ref, o_ref, *, n):
    def conv(u, kf_ref, b_ref):
        y = _dot(inv_ref[...], _complex_mul(_dot(f_ref[...], u), kf_ref[...], n))
        return y + u * b_ref[...]

    z = x1_ref[...] * conv(v_ref[...], kf0_ref, b0_ref)
    o_ref[...] = x2_ref[...] * conv(z, kf1_ref, b1_ref)


def _dense_hyena_call(hc, length, width, kf, fwd, inv, bias0, bias1, tc):
    rows = fwd.shape[0]
    per = width // tc
    return pl.pallas_call(
        functools.partial(_dense_hyena_kernel, n=rows // 2),
        grid=(per,),
        in_specs=[pl.BlockSpec((length, tc), lambda j: (0, j)),
                  pl.BlockSpec((length, tc), lambda j: (0, per + j)),
                  pl.BlockSpec((length, tc), lambda j: (0, 2 * per + j)),
                  pl.BlockSpec((rows, tc), lambda j: (0, j)),
                  pl.BlockSpec((rows, tc), lambda j: (0, per + j)),
                  pl.BlockSpec((rows, length), lambda j: (0, 0)),
                  pl.BlockSpec((length, rows), lambda j: (0, 0)),
                  pl.BlockSpec((1, tc), lambda j: (0, j)),
                  pl.BlockSpec((1, tc), lambda j: (0, j))],
        out_specs=pl.BlockSpec((length, tc), lambda j: (0, j)),
        out_shape=jax.ShapeDtypeStruct((length, width), F32),
        compiler_params=_cparams(1),
        name="hyena_dense",
    )(hc, hc, hc, kf, kf, fwd, inv, bias0, bias1)


def _filter_features(length):
    t = jnp.linspace(0.0, 1.0, length, dtype=F32)[:, None]
    w = (2.0 * math.pi / length) * jnp.arange(length, dtype=F32)[:, None]
    bands = jnp.linspace(1e-4, HYENA_BANDS - 1, HYENA_BANDS, dtype=F32)[None, :]
    feats = jnp.concatenate([t, jnp.cos(bands * w), -jnp.sin(bands * w)], axis=-1)
    return jnp.pad(feats, ((0, 0), (0, LANES - feats.shape[1])))


def _head_rms(x, heads, width):
    outs = []
    for h in range(heads):
        xh = x[:, h * width:(h + 1) * width]
        outs.append(xh * lax.rsqrt(jnp.mean(xh * xh, axis=-1, keepdims=True) + RMS_EPS))
    return jnp.concatenate(outs, axis=1)


def _merge_kernel(ogf_ref, ogb_ref, omf_ref, omb_ref, z_ref, o_ref, hy_ref, g0_ref, g1_ref, g2_ref, gn_ref, mn_ref,
                  wa_ref, wb_ref, wc_ref, out_ref, *, gdn_heads, gdn_dv, ml_heads, ml_dv):
    a = _head_rms(ogf_ref[...] + ogb_ref[...], gdn_heads, gdn_dv) * gn_ref[...] * _silu(z_ref[...])
    b = _head_rms(omf_ref[...] + omb_ref[...], ml_heads, ml_dv) * mn_ref[...] * _sigmoid(o_ref[...])
    out_ref[...] = (_sigmoid(g0_ref[...]) * _dot(a, wa_ref[...]) + _sigmoid(g1_ref[...]) * _dot(b, wb_ref[...])
                    + _sigmoid(g2_ref[...]) * _dot(hy_ref[...], wc_ref[...]))


def _merge_call(og, om, p, z_cb, o_cb, hy, gate_cb, gn, mn, wa, wb, wc, layer, heads, tm):
    m, d = p.shape[0], wa.shape[-1]
    gw, mw, hw = og[0].shape[-1], om[0].shape[-1], hy.shape[-1]
    gdn_heads, gdn_dv, ml_heads, ml_dv = heads
    zb, ob, gb = z_cb // gw, o_cb // mw, gate_cb // d
    return pl.pallas_call(
        functools.partial(_merge_kernel, gdn_heads=gdn_heads, gdn_dv=gdn_dv, ml_heads=ml_heads, ml_dv=ml_dv),
        grid=(m // tm,),
        in_specs=[pl.BlockSpec((tm, gw), lambda i: (i, 0)),
                  pl.BlockSpec((tm, gw), lambda i: (i, 0)),
                  pl.BlockSpec((tm, mw), lambda i: (i, 0)),
                  pl.BlockSpec((tm, mw), lambda i: (i, 0)),
                  pl.BlockSpec((tm, gw), lambda i: (i, zb)),
                  pl.BlockSpec((tm, mw), lambda i: (i, ob)),
                  pl.BlockSpec((tm, hw), lambda i: (i, 0)),
                  pl.BlockSpec((tm, d), lambda i: (i, gb)),
                  pl.BlockSpec((tm, d), lambda i: (i, gb + 1)),
                  pl.BlockSpec((tm, d), lambda i: (i, gb + 2)),
                  pl.BlockSpec((1, gw), lambda i: (0, 0)),
                  pl.BlockSpec((1, mw), lambda i: (0, 0)),
                  pl.BlockSpec((None, gw, d), lambda i: (layer, 0, 0), pipeline_mode=pl.Buffered(1)),
                  pl.BlockSpec((None, mw, d), lambda i: (layer, 0, 0), pipeline_mode=pl.Buffered(1)),
                  pl.BlockSpec((None, hw, d), lambda i: (layer, 0, 0), pipeline_mode=pl.Buffered(1))],
        out_specs=pl.BlockSpec((tm, d), lambda i: (i, 0)),
        out_shape=jax.ShapeDtypeStruct((m, d), F32),
        compiler_params=_cparams(1),
        name="branch_merge",
    )(og[0], og[1], om[0], om[1], p, p, hy, p, p, p, gn, mn, wa, wb, wc)


def _outproj_kernel(y_ref, x_ref, m_ref, w_ref, o_ref, *, tm, lat_len):
    gate = jnp.where(_row_is_ctx(pl.program_id(0) * tm, tm, lat_len), m_ref[0:1, :], m_ref[1:2, :])
    o_ref[...] = x_ref[...] + gate * _dot(y_ref[...], w_ref[...])


def _outproj_call(y, x, mods, w, layer, lat_len, tm):
    m, d = x.shape
    return pl.pallas_call(
        functools.partial(_outproj_kernel, tm=tm, lat_len=lat_len),
        grid=(m // tm,),
        in_specs=[pl.BlockSpec((tm, d), lambda i: (i, 0)),
                  pl.BlockSpec((tm, d), lambda i: (i, 0)),
                  pl.BlockSpec((SUBLANES, d), lambda i: (0, 0)),
                  pl.BlockSpec((None, d, d), lambda i: (layer, 0, 0))],
        out_specs=pl.BlockSpec((tm, d), lambda i: (i, 0)),
        out_shape=jax.ShapeDtypeStruct((m, d), F32),
        compiler_params=_cparams(1),
        name="mixer_out_proj",
    )(y, x, mods, w)


def _final_norm_kernel(x_ref, g_ref, o_ref):
    x = x_ref[...]
    o_ref[...] = x * lax.rsqrt(jnp.mean(x * x, axis=-1, keepdims=True) + RMS_EPS) * g_ref[...]


def _final_norm_call(x, g, row0, rows, tm):
    d = x.shape[1]
    return pl.pallas_call(
        _final_norm_kernel,
        grid=(rows // tm,),
        in_specs=[pl.BlockSpec((tm, d), lambda i: (row0 // tm + i, 0)),
                  pl.BlockSpec((1, d), lambda i: (0, 0))],
        out_specs=pl.BlockSpec((tm, d), lambda i: (i, 0)),
        out_shape=jax.ShapeDtypeStruct((rows, d), F32),
        compiler_params=_cparams(1),
        name="final_rms_norm",
    )(x, g)


def _regroup_kernel(*refs, shift, width, n_blk, src_col, src_rows, src_cols, masked):
    o_ref = refs[-1]
    val = refs[0][...]
    if n_blk > 1:
        val = jnp.concatenate([val, refs[1][...]], axis=1)[:, shift:shift + width]
    if masked:
        row = pl.program_id(1) * val.shape[0] + lax.broadcasted_iota(jnp.int32, val.shape, 0)
        col = src_col + pl.program_id(2) * width + lax.broadcasted_iota(jnp.int32, val.shape, 1)
        val = jnp.where(jnp.logical_and(row < src_rows, col < src_cols), val, 0.0)
    o_ref[...] = val.astype(o_ref.dtype)


def _regroup_call(src, buf, src_col, dst_col, ncols, out_rows, n_total, tcr, tb=None):
    depth, src_rows, src_cols = src.shape
    q, shift = divmod(src_col, LANES)
    per = tcr // LANES
    assert q % per == 0
    last = pl.cdiv(src_cols, LANES) - 1
    tb = tb or _pick_tile(out_rows, (1024, 512, 256, 128))
    assert out_rows % tb == 0
    last_rb = pl.cdiv(src_rows, tb) - 1
    in_specs = [pl.BlockSpec((None, tb, tcr), lambda l, i, t: (l, jnp.minimum(i, last_rb), q // per + t))]
    if shift:
        in_specs.append(pl.BlockSpec((None, tb, LANES),
                                     lambda l, i, t: (l, jnp.minimum(i, last_rb), jnp.minimum(q + per * (t + 1), last))))
    n_blk = len(in_specs)
    operands = [src] * n_blk
    aliases = {}
    if buf is not None:
        in_specs.append(pl.BlockSpec(memory_space=pl.ANY))
        operands.append(buf)
        aliases = {n_blk: 0}
    return pl.pallas_call(
        functools.partial(_regroup_kernel, shift=shift, width=tcr, n_blk=n_blk, src_col=src_col, src_rows=src_rows,
                          src_cols=src_cols, masked=out_rows > src_rows or src_col + ncols > src_cols),
        grid=(depth, out_rows // tb, ncols // tcr),
        in_specs=in_specs,
        out_specs=pl.BlockSpec((None, tb, tcr), lambda l, i, t: (l, i, dst_col // tcr + t)),
        out_shape=jax.ShapeDtypeStruct((depth, out_rows, n_total), BF16),
        input_output_aliases=aliases,
        compiler_params=_cparams(3),
        name="weight_regroup",
    )(*operands)


ROW_HALO = 64


def _regroup_rows_kernel(*refs, shift):
    o_ref = refs[-1]
    val = refs[0][...]
    if shift:
        val = jnp.concatenate([val, refs[1][...]], axis=0)[shift:shift + val.shape[0]]
    o_ref[...] = val.astype(o_ref.dtype)


def _regroup_rows_call(src, buf, src_row, dst_row, nrows, n_total, tr):
    depth, src_rows, d = src.shape
    qb, shift = divmod(src_row, tr)
    assert dst_row % tr == 0 and nrows % tr == 0 and shift % SUBLANES == 0 and shift <= ROW_HALO
    assert tr % ROW_HALO == 0 and src_row + nrows <= src_rows
    last = pl.cdiv(src_rows, ROW_HALO) - 1
    in_specs = [pl.BlockSpec((None, tr, d), lambda l, t: (l, qb + t, 0))]
    if shift:
        in_specs.append(pl.BlockSpec((None, ROW_HALO, d),
                                     lambda l, t: (l, jnp.minimum((qb + t + 1) * (tr // ROW_HALO), last), 0)))
    operands = [src] * len(in_specs)
    aliases = {}
    if buf is not None:
        aliases = {len(in_specs): 0}
        in_specs.append(pl.BlockSpec(memory_space=pl.ANY))
        operands.append(buf)
    return pl.pallas_call(
        functools.partial(_regroup_rows_kernel, shift=shift),
        grid=(depth, nrows // tr),
        in_specs=in_specs,
        out_specs=pl.BlockSpec((None, tr, d), lambda l, t: (l, dst_row // tr + t, 0)),
        out_shape=jax.ShapeDtypeStruct((depth, n_total, d), BF16),
        input_output_aliases=aliases,
        compiler_params=_cparams(2),
        name="weight_row_regroup",
    )(*operands)


def _gate_rows_kernel(*refs, offs, heads):
    o_ref = refs[-1]
    row = lax.broadcasted_iota(jnp.int32, refs[0].shape, 0)
    tiles = []
    for g, (off, h) in enumerate(zip(offs, heads)):
        t8 = jnp.where(jnp.logical_and(row >= off, row < off + h), refs[g][...], 0.0)
        if off:
            t8 = pltpu.roll(t8, SUBLANES - off, axis=0)
        tiles += [t8, jnp.zeros((LANES - SUBLANES, t8.shape[1]), t8.dtype)]
    o_ref[...] = jnp.concatenate(tiles, axis=0).astype(o_ref.dtype)


def _gate_rows_call(src, buf, rows, heads, dst_row):
    depth, _, d = src.shape
    n_total = buf.shape[1]
    height = len(rows) * LANES
    assert dst_row % height == 0 and all(r // SUBLANES == (r + h - 1) // SUBLANES for r, h in zip(rows, heads))
    in_specs = [pl.BlockSpec((None, SUBLANES, d), lambda l, blk=r // SUBLANES: (l, blk, 0)) for r in rows]
    in_specs.append(pl.BlockSpec(memory_space=pl.ANY))
    return pl.pallas_call(
        functools.partial(_gate_rows_kernel, offs=tuple(r % SUBLANES for r in rows), heads=tuple(heads)),
        grid=(depth,),
        in_specs=in_specs,
        out_specs=pl.BlockSpec((None, height, d), lambda l: (l, dst_row // height, 0)),
        out_shape=jax.ShapeDtypeStruct((depth, n_total, d), BF16),
        input_output_aliases={len(rows): 0},
        compiler_params=_cparams(1),
        name="gate_weight_groups",
    )(*([src] * len(rows)), buf)


def _pack_rows(rows, d):
    out = jnp.concatenate([r.reshape(1, d) for r in rows], axis=0)
    return jnp.pad(out, ((0, SUBLANES - out.shape[0]), (0, 0)))


def _lane_row(vals, lanes=LANES):
    return jnp.pad(vals.astype(F32), ((0, 0), (0, lanes - vals.shape[1])))[:, None, :]


def _pick_tile(total, candidates):
    for c in candidates:
        if total % c == 0:
            return c
    raise ValueError(f"no tile for {total} among {candidates}")


def kernel(x, c, ctx, c_ctx, w_ada, b_ada, norm_g, ffn_w_gate, ffn_w_up, ffn_w_down, w_in, gdn_conv, gdn_a_log,
           gdn_dt_bias, gdn_norm, mlstm_i_bias, mlstm_f_bias, mlstm_norm, hy_short_w, hy_short_b, hf_w1, hf_b1,
           hf_w2, hf_b2, hf_w3, hf_freq, hf_decay, hy_bias, w_a_out, w_b_out, w_c_out, w_out, final_norm):
    batch, seq, d = x.shape
    assert batch == 1 and c.shape[0] == 1 and ctx.shape[0] == 1
    ctx_len = ctx.shape[1]
    depth = w_ada.shape[0]
    m = ctx_len + seq
    assert ctx_len % CHUNK == 0 and seq % CHUNK == 0

    gdn_heads = gdn_a_log.shape[-1]
    gdn_vw = w_a_out.shape[1]
    gdn_qkw = (gdn_conv.shape[1] - gdn_vw) // 2
    gdn_dk, gdn_dv = gdn_qkw // gdn_heads, gdn_vw // gdn_heads
    ml_heads = mlstm_i_bias.shape[-1]
    ml_vw = w_b_out.shape[1]
    hy_w = w_c_out.shape[1]
    n_in = w_in.shape[-1]
    ml_qkw = (n_in - (2 * gdn_qkw + gdn_vw) - 4 * gdn_heads - 4 * ml_heads - ml_vw - gdn_vw - ml_vw
              - 3 * hy_w - 3 * d) // 2
    ml_dqk, ml_dv = ml_qkw // ml_heads, ml_vw // ml_heads
    assert gdn_dk == LANES and gdn_dv == LANES and ml_dqk == LANES and ml_dv % LANES == 0

    o_gqkv = 0
    o_gbeta = o_gqkv + 2 * gdn_qkw + gdn_vw
    o_ga = o_gbeta + 2 * gdn_heads
    o_mqkv = o_ga + 2 * gdn_heads
    o_mi = o_mqkv + 2 * ml_qkw + ml_vw
    o_mf = o_mi + 2 * ml_heads
    o_z = o_mf + 2 * ml_heads
    o_o = o_z + gdn_vw
    o_hy = o_o + ml_vw
    o_gates = o_hy + 3 * hy_w
    assert o_gates + 3 * d == n_in

    c_gqkv = 0
    c_mqkv = c_gqkv + 2 * gdn_qkw + gdn_vw
    c_z = c_mqkv + 2 * ml_qkw + ml_vw
    c_o = c_z + gdn_vw
    c_hy = c_o + ml_vw
    c_gates = c_hy + 3 * hy_w
    c_gsmall = c_gates + 3 * d
    c_msmall = c_gsmall + 4 * LANES
    n_p = c_msmall + 4 * LANES

    w_in_t = jnp.swapaxes(w_in, 1, 2)
    moves = [(o_gqkv, c_gqkv, c_mqkv - c_gqkv), (o_mqkv, c_mqkv, c_z - c_mqkv), (o_z, c_z, c_gsmall - c_z)]
    tr = 512 if all(dst % 512 == 0 and width % 512 == 0 for _, dst, width in moves) else 256
    w_in_p = None
    for src_row, dst_row, height in moves:
        w_in_p = _regroup_rows_call(w_in_t, w_in_p, src_row, dst_row, height, n_p, tr)
    gate_cols, gate_heads = [], []
    for offs, heads in (((o_gbeta, o_ga), gdn_heads), ((o_mi, o_mf), ml_heads)):
        for direction in range(2):
            for off in offs:
                gate_cols.append(off + direction * heads)
                gate_heads.append(heads)
    per_call = 1
    while per_call < len(gate_cols) and c_gsmall % (2 * per_call * LANES) == 0:
        per_call *= 2
    for g0 in range(0, len(gate_cols), per_call):
        w_in_p = _gate_rows_call(w_in_t, w_in_p, gate_cols[g0:g0 + per_call], gate_heads[g0:g0 + per_call],
                                 c_gsmall + g0 * LANES)

    f_hidden = ffn_w_gate.shape[-1]
    tf = 512
    f_pad = -(-f_hidden // tf) * tf
    wide = f_pad // 2 if f_pad % (2 * LANES) == 0 else tf
    wg = _regroup_call(ffn_w_gate.reshape(2 * depth, d, f_hidden), None, 0, 0, f_pad, d, f_pad, wide)
    wu = _regroup_call(ffn_w_up.reshape(2 * depth, d, f_hidden), None, 0, 0, f_pad, d, f_pad, wide)
    wd = _regroup_call(ffn_w_down.reshape(2 * depth, f_hidden, d), None, 0, 0, d, f_pad, d, d,
                       tb=f_pad // 4 if f_pad % (4 * 2 * SUBLANES) == 0 else None)
    wg, wu = wg.reshape(depth, 2, d, f_pad), wu.reshape(depth, 2, d, f_pad)
    wd = wd.reshape(depth, 2, f_pad, d)
    wa, wb, wc, wo = (t.astype(BF16) for t in (w_a_out, w_b_out, w_c_out, w_out))

    tm_big = _pick_tile(m, (768, 512, 384, 256, 128, 64))
    tm_mid = _pick_tile(m, (256, 128, 64))
    tm_proj = _pick_tile(m, (1056, 768, 512, 384, 256, 128, 64))
    tm_seg = _pick_tile(math.gcd(ctx_len, seq), (256, 128, 64))
    tn_in = _pick_tile(n_p, (1024, 512, 256, 128))
    nc_ctx = ctx_len // CHUNK

    cond = jnp.pad(jnp.concatenate([c_ctx[None, :], c], axis=0), ((0, SUBLANES - 2), (0, 0)))
    mods = _modulation_call(cond, w_ada, b_ada)

    dft_consts = _dft_constants(seq)
    dense_fwd, dense_inv = _dense_dft_constants(ctx_len)
    feats_lat = _filter_features(seq)
    feats_lat = feats_lat.reshape(seq // DFT_N2, DFT_N2, LANES).swapaxes(0, 1).reshape(seq, LANES)
    feats_ctx = _filter_features(ctx_len)
    tl = _pick_tile(seq, (512, 256, 128, 64))

    s = jnp.concatenate([x[0], ctx[0]], axis=0)
    for l in range(depth):
        last = l == depth - 1

        def mod(idx):
            return [mods[l, 0, idx * d:(idx + 1) * d], mods[l, 1, idx * d:(idx + 1) * d]]

        s = _ffn_call(s, _pack_rows(mod(0) + mod(1) + mod(2), d), norm_g[l, 0][None, :], wg, wu, wd, l, 0,
                      seq, tm_big, tf)
        p = _inproj_call(s, _pack_rows(mod(3) + mod(4), d), norm_g[l, 1][None, :], w_in_p, l, seq, tm_proj, tn_in)

        gqkv = _dwconv_call(p, c_gqkv, 2 * gdn_qkw + gdn_vw, jnp.transpose(gdn_conv[l]),
                            jnp.zeros((1, 2 * gdn_qkw + gdn_vw), F32), seq, tm_seg, 1024, True)
        hc = _dwconv_call(p, c_hy, 3 * hy_w, jnp.transpose(hy_short_w[l]), hy_short_b[l][None, :],
                          seq, tm_seg, 1024, False)

        og, om = _fused_scans_call(
            m // CHUNK,
            _gdn_scan_parts(gqkv, p, c_gsmall // LANES, _lane_row(gdn_a_log[l]), _lane_row(gdn_dt_bias[l]),
                            gdn_heads, gdn_dk, gdn_dv, nc_ctx),
            _mlstm_scan_parts(p, c_mqkv, c_msmall, _lane_row(mlstm_i_bias[l]), _lane_row(mlstm_f_bias[l]),
                              ml_heads, ml_dqk, ml_dv, nc_ctx))

        fargs = (jnp.pad(hf_w1[l], ((0, LANES - hf_w1.shape[1]), (0, 0))), hf_b1[l][None, :], hf_w2[l],
                 hf_b2[l][None, :], hf_w3[l], hf_freq[l], jnp.tile(hf_decay[l], 2)[None, :])
        taps = _filter_call(feats_lat, *fargs, tl, 1024)
        m1, g2, minv, pinv = dft_consts
        kh = seq // DFT_N2
        ta = _dft1_call(taps.reshape(DFT_N2, kh, taps.shape[1]), 0, taps.shape[1], m1, 1024, n_lo_major=True)
        kf = _filter_spectrum_call(ta.reshape(ta.shape[0] * DFT_N2, taps.shape[1]), g2, 1024)
        hl3 = hc.reshape(m // DFT_N2, DFT_N2, 3 * hy_w)
        tc_h = min(hy_w, 1024)
        per = hy_w // tc_h
        z1 = _long_conv(hl3, 0, False, hl3, per, hy_bias[l, 0][None, :], kf, 0, dft_consts, hy_w, tc_h, True)
        z2 = _long_conv(z1, 0, True, hl3, 2 * per, hy_bias[l, 1][None, :], kf, per, dft_consts, hy_w, tc_h, False)
        z2 = z2.reshape(seq, hy_w)
        if not last:
            taps_c = _filter_call(feats_ctx, *fargs, _pick_tile(ctx_len, (256, 128, 64)), 1024)
            kf_c = _dense_spectrum_call(taps_c, dense_fwd, 512)
            zc = _dense_hyena_call(hc[seq:], ctx_len, hy_w, kf_c, dense_fwd, dense_inv, hy_bias[l, 0][None, :],
                                   hy_bias[l, 1][None, :], 256)
        else:
            zc = jnp.zeros((ctx_len, hy_w), F32)
        hy = jnp.concatenate([z2, zc], axis=0)

        merged = _merge_call(og, om, p, c_z, c_o, hy, c_gates, jnp.tile(gdn_norm[l], gdn_heads)[None, :],
                             mlstm_norm[l][None, :], wa, wb, wc, l, (gdn_heads, gdn_dv, ml_heads, ml_dv), tm_mid)
        s = _outproj_call(merged, s, _pack_rows(mod(5), d), wo, l, seq, tm_mid)
        s = _ffn_call(s, _pack_rows(mod(6) + mod(7) + mod(8), d), norm_g[l, 2][None, :], wg, wu, wd, l, 1,
                      seq, tm_big, tf)

    out = _final_norm_call(s, final_norm[None, :], 0, seq, tm_seg)
    return out[None]
```

```python
import functools
import math

import jax
import jax.numpy as jnp
import numpy as np
from jax import lax
from jax.experimental import pallas as pl
from jax.experimental.pallas import tpu as pltpu

F32 = jnp.float32
BF16 = jnp.bfloat16

N_MOD = 9
RMS_EPS = 1e-6
L2_EPS = 1e-6
CHUNK = 64
GDN_CONV = 5
HYENA_SHORT = 3
HYENA_BANDS = 16
LANES = 128
SUBLANES = 8
DFT_N2 = 128
VMEM_LIMIT = 56 * 1024 * 1024


def _cparams(n_axes):
    return pltpu.CompilerParams(dimension_semantics=("arbitrary",) * n_axes,
                                vmem_limit_bytes=VMEM_LIMIT)


def _sigmoid(x):
    return jax.nn.sigmoid(x)


def _silu(x):
    return x * jax.nn.sigmoid(x)


def _softplus(x):
    return jnp.maximum(x, 0.0) + jnp.log(1.0 + jnp.exp(-jnp.abs(x)))


def _dot(a, b):
    return jnp.dot(a.astype(BF16), b.astype(BF16), preferred_element_type=F32)


def _dot_nt(a, b):
    return lax.dot_general(a.astype(BF16), b.astype(BF16), (((1,), (1,)), ((), ())),
                           preferred_element_type=F32)


def _dot_tn(a, b):
    return lax.dot_general(a.astype(BF16), b.astype(BF16), (((0,), (0,)), ((), ())),
                           preferred_element_type=F32)


def _dot_exact(a, b):
    return jnp.dot(a, b, preferred_element_type=F32, precision=lax.Precision.HIGHEST)


def _row_is_ctx(row0, rows, lat_len):
    r = row0 + lax.broadcasted_iota(jnp.int32, (rows, 1), 0)
    return r >= lat_len


def _modnorm(x, g, shift, scale):
    y = x * lax.rsqrt(jnp.mean(x * x, axis=-1, keepdims=True) + RMS_EPS) * g
    return y * (1.0 + scale) + shift


ROW_CHUNK = 128


def _for_row_chunks(tm, fn):
    rc = math.gcd(tm, ROW_CHUNK)

    def body(c, carry):
        fn(pl.multiple_of(c * rc, rc), rc)
        return carry

    lax.fori_loop(0, tm // rc, body, 0)


def _store_modulated(x_ref, m_ref, g_ref, h_sc, row0, tm, lat_len):
    def chunk(r0, rc):
        is_ctx = _row_is_ctx(row0 + r0, rc, lat_len)
        shift = jnp.where(is_ctx, m_ref[0:1, :], m_ref[1:2, :])
        scale = jnp.where(is_ctx, m_ref[2:3, :], m_ref[3:4, :])
        h_sc[pl.ds(r0, rc), :] = _modnorm(x_ref[pl.ds(r0, rc), :], g_ref[...], shift, scale).astype(h_sc.dtype)

    _for_row_chunks(tm, chunk)


def _mod_kernel(c_ref, w_ref, b_ref, o_ref):
    o_ref[0] = _dot(_silu(c_ref[...]), w_ref[0]) + b_ref[0]


def _modulation_call(cond, w_ada, b_ada):
    depth, d, n = w_ada.shape
    tn = _pick_tile(n, (1024, 512, 256, 128))
    return pl.pallas_call(
        _mod_kernel,
        grid=(depth, n // tn),
        in_specs=[pl.BlockSpec((SUBLANES, d), lambda l, j: (0, 0)),
                  pl.BlockSpec((1, d, tn), lambda l, j: (l, 0, j)),
                  pl.BlockSpec((1, 1, tn), lambda l, j: (l, 0, j))],
        out_specs=pl.BlockSpec((1, SUBLANES, tn), lambda l, j: (l, 0, j)),
        out_shape=jax.ShapeDtypeStruct((depth, SUBLANES, n), F32),
        compiler_params=_cparams(2),
        name="adaln_modulation",
    )(cond, w_ada, b_ada.reshape(depth, 1, n))


def _ffn_kernel(x_ref, m_ref, g_ref, wg_ref, wu_ref, wd_ref, o_ref, h_sc, acc_sc, *, tm, lat_len):
    row0 = pl.program_id(0) * tm
    f = pl.program_id(1)

    @pl.when(f == 0)
    def _():
        _store_modulated(x_ref, m_ref, g_ref, h_sc, row0, tm, lat_len)
        acc_sc[...] = jnp.zeros_like(acc_sc)

    h = h_sc[...]
    a = _silu(_dot(h, wg_ref[...])) * _dot(h, wu_ref[...])
    acc_sc[...] += _dot(a, wd_ref[...])

    @pl.when(f == pl.num_programs(1) - 1)
    def _():
        def chunk(r0, rc):
            gate = jnp.where(_row_is_ctx(row0 + r0, rc, lat_len), m_ref[4:5, :], m_ref[5:6, :])
            rows = pl.ds(r0, rc)
            o_ref[rows, :] = x_ref[rows, :] + 0.5 * gate * acc_sc[rows, :]

        _for_row_chunks(tm, chunk)


def _ffn_call(x, mods, g, wg, wu, wd, layer, which, lat_len, tm, tf):
    m, d = x.shape
    fp = wg.shape[-1]
    return pl.pallas_call(
        functools.partial(_ffn_kernel, tm=tm, lat_len=lat_len),
        grid=(m // tm, fp // tf),
        in_specs=[pl.BlockSpec((tm, d), lambda i, f: (i, 0)),
                  pl.BlockSpec((SUBLANES, d), lambda i, f: (0, 0)),
                  pl.BlockSpec((1, d), lambda i, f: (0, 0)),
                  pl.BlockSpec((None, None, d, tf), lambda i, f: (layer, which, 0, f)),
                  pl.BlockSpec((None, None, d, tf), lambda i, f: (layer, which, 0, f)),
                  pl.BlockSpec((None, None, tf, d), lambda i, f: (layer, which, f, 0))],
        out_specs=pl.BlockSpec((tm, d), lambda i, f: (i, 0)),
        out_shape=jax.ShapeDtypeStruct((m, d), F32),
        scratch_shapes=[pltpu.VMEM((tm, d), BF16), pltpu.VMEM((tm, d), F32)],
        compiler_params=_cparams(2),
        name="macaron_swiglu",
    )(x, mods, g, wg, wu, wd)


def _inproj_kernel(x_ref, m_ref, g_ref, w_ref, o_ref, h_sc, *, tm, lat_len):
    @pl.when(pl.program_id(1) == 0)
    def _():
        _store_modulated(x_ref, m_ref, g_ref, h_sc, pl.program_id(0) * tm, tm, lat_len)

    o_ref[...] = _dot_nt(h_sc[...], w_ref[...])


def _inproj_call(x, mods, g, w, layer, lat_len, tm, tn):
    m, d = x.shape
    n = w.shape[1]
    return pl.pallas_call(
        functools.partial(_inproj_kernel, tm=tm, lat_len=lat_len),
        grid=(m // tm, n // tn),
        in_specs=[pl.BlockSpec((tm, d), lambda i, j: (i, 0)),
                  pl.BlockSpec((SUBLANES, d), lambda i, j: (0, 0)),
                  pl.BlockSpec((1, d), lambda i, j: (0, 0)),
                  pl.BlockSpec((None, tn, d), lambda i, j: (layer, j, 0))],
        out_specs=pl.BlockSpec((tm, tn), lambda i, j: (i, j)),
        out_shape=jax.ShapeDtypeStruct((m, n), F32),
        scratch_shapes=[pltpu.VMEM((tm, d), BF16)],
        compiler_params=_cparams(2),
        name="mixer_in_proj",
    )(x, mods, g, w)


def _dwconv_kernel(prev_ref, cur_ref, next_ref, w_ref, b_ref, o_ref, ext_sc, *, tm, taps, split, m_rows, act):
    row0 = pl.program_id(0) * tm
    at_start = jnp.logical_or(row0 == 0, row0 == split)
    at_end = jnp.logical_or(row0 + tm == split, row0 + tm == m_rows)
    ext_sc[0:SUBLANES, :] = jnp.where(at_start, 0.0, prev_ref[...])
    ext_sc[SUBLANES:SUBLANES + tm, :] = cur_ref[...]
    ext_sc[SUBLANES + tm:2 * SUBLANES + tm, :] = jnp.where(at_end, 0.0, next_ref[...])
    acc = jnp.zeros(o_ref.shape, F32) + b_ref[...]
    for t in range(taps):
        acc = acc + ext_sc[pl.ds(SUBLANES + t - taps // 2, tm), :] * w_ref[t:t + 1, :]
    o_ref[...] = _silu(acc) if act else acc


def _dwconv_call(p, col0, ncols, w_t, b, split, tm, tc, act):
    m = p.shape[0]
    taps = w_t.shape[0]
    assert split % tm == 0 and m % tm == 0 and col0 % tc == 0 and taps // 2 <= SUBLANES
    cb0 = col0 // tc
    rb = tm // SUBLANES
    last = m // SUBLANES - 1
    return pl.pallas_call(
        functools.partial(_dwconv_kernel, tm=tm, taps=taps, split=split, m_rows=m, act=act),
        grid=(m // tm, ncols // tc),
        in_specs=[pl.BlockSpec((SUBLANES, tc), lambda i, j: (jnp.maximum(i * rb - 1, 0), cb0 + j)),
                  pl.BlockSpec((tm, tc), lambda i, j: (i, cb0 + j)),
                  pl.BlockSpec((SUBLANES, tc), lambda i, j: (jnp.minimum((i + 1) * rb, last), cb0 + j)),
                  pl.BlockSpec((taps, tc), lambda i, j: (0, j)),
                  pl.BlockSpec((1, tc), lambda i, j: (0, j))],
        out_specs=pl.BlockSpec((tm, tc), lambda i, j: (i, j)),
        out_shape=jax.ShapeDtypeStruct((m, ncols), F32),
        scratch_shapes=[pltpu.VMEM((tm + 2 * SUBLANES, tc), F32)],
        compiler_params=_cparams(2),
        name="depthwise_conv",
    )(p, p, p, w_t, b)


def _fwd_chunk(s, nc_ctx, nc_tot):
    return jnp.where(s < nc_ctx, nc_tot - nc_ctx + s, s - nc_ctx)


def _bwd_chunk(s, nc_ctx, nc_tot):
    del nc_ctx
    return nc_tot - 1 - s


def _causal_masks(direction):
    ii = lax.broadcasted_iota(jnp.int32, (CHUNK, CHUNK), 0)
    jj = lax.broadcasted_iota(jnp.int32, (CHUNK, CHUNK), 1)
    rel = ii - jj if direction == 0 else jj - ii
    return rel >= 0, rel > 0


def _unit_triangular_inverses(mats):
    ii = lax.broadcasted_iota(jnp.int32, (CHUNK, CHUNK), 0)
    jj = lax.broadcasted_iota(jnp.int32, (CHUNK, CHUNK), 1)
    eye = (ii == jj).astype(F32)

    def same_block(width):
        return (ii // width) == (jj // width)

    inner = same_block(8)
    d1 = [jnp.where(inner, a, 0.0) for a in mats]
    d2 = [_dot(x, x) for x in d1]
    yield
    d4 = [_dot(x, x) for x in d2]
    p = [_dot(eye - x, eye + y) for x, y in zip(d1, d2)]
    yield
    p = [_dot(x, eye + y) for x, y in zip(p, d4)]
    yield
    for width in (16, 32, 64):
        outer = same_block(width)
        ring = jnp.logical_and(outer, jnp.logical_not(inner))
        t = [_dot(x, jnp.where(ring, a, 0.0)) for x, a in zip(p, mats)]
        yield
        p = [x - _dot(y, x) for x, y in zip(p, t)]
        yield
        inner = outer
    return p


def _gdn_scan_kernel(qf_ref, kf_ref, vf_ref, btf_ref, af_ref, qb_ref, kb_ref, vb_ref, btb_ref, ab_ref,
                     alog_ref, dtb_ref, of_ref, ob_ref, s_sc, *, heads, dk, dv):
    @pl.when(pl.program_id(0) == 0)
    def _():
        s_sc[...] = jnp.zeros_like(s_sc)

    chains = []
    for direction, (q_ref, k_ref, v_ref, bt_ref, a_ref, o_ref) in enumerate(
            ((qf_ref, kf_ref, vf_ref, btf_ref, af_ref, of_ref), (qb_ref, kb_ref, vb_ref, btb_ref, ab_ref, ob_ref))):
        incl, strict = _causal_masks(direction)
        beta = _sigmoid(bt_ref[...])
        glog = -jnp.exp(alog_ref[direction]) * _softplus(a_ref[...] + dtb_ref[direction])
        gcum = _dot_exact(incl.astype(F32), glog)
        gtot = gcum[CHUNK - 1:CHUNK, :] if direction == 0 else gcum[0:1, :]
        e_in = jnp.exp(gcum)
        e_out = jnp.exp(gtot - gcum)
        g_end = jnp.exp(gtot)
        gcum_t = gcum.T
        for h in range(heads):
            q = q_ref[:, h * dk:(h + 1) * dk]
            k = k_ref[:, h * dk:(h + 1) * dk]
            q = q * lax.rsqrt(jnp.sum(q * q, axis=-1, keepdims=True) + L2_EPS) * (dk ** -0.5)
            k = k * lax.rsqrt(jnp.sum(k * k, axis=-1, keepdims=True) + L2_EPS)
            b_col = beta[:, h:h + 1]
            kb = k * b_col
            chains.append(dict(
                q=q, k=k, kb=kb, strict=strict, o_ref=o_ref, h=h, slot=direction * heads + h,
                decay=jnp.exp(jnp.where(incl, gcum[:, h:h + 1] - gcum_t[h:h + 1, :], -jnp.inf)),
                rhs=jnp.concatenate([v_ref[:, h * dv:(h + 1) * dv] * b_col, kb * e_in[:, h:h + 1]], axis=1),
                q_in=q * e_in[:, h:h + 1], k_out=k * e_out[:, h:h + 1], g_end=g_end[:, h:h + 1]))

    yield
    a_mats = [jnp.where(c["strict"], _dot_nt(c["kb"], c["k"]) * c["decay"], 0.0) for c in chains]
    qk = [_dot_nt(c["q"], c["k"]) * c["decay"] for c in chains]
    yield
    t_inv = yield from _unit_triangular_inverses(a_mats)
    sol = [_dot(t, c["rhs"]) for t, c in zip(t_inv, chains)]
    yield
    states = [s_sc[c["slot"]] for c in chains]
    v_new = [x[:, :dv] - _dot(x[:, dv:], st) for x, st in zip(sol, states)]
    o_inter = [_dot(c["q_in"], st) for c, st in zip(chains, states)]
    yield
    o_intra = [_dot(x, y) for x, y in zip(qk, v_new)]
    s_upd = [_dot_tn(c["k_out"], y) for c, y in zip(chains, v_new)]
    yield
    for c, st, x, y, z in zip(chains, states, o_inter, o_intra, s_upd):
        c["o_ref"][:, c["h"] * dv:(c["h"] + 1) * dv] = x + y
        s_sc[c["slot"]] = st * c["g_end"] + z


def _gdn_scan_parts(qkv, p, beta_cb, alog, dtb, heads, dk, dv, nc_ctx):
    m = qkv.shape[0]
    nc = m // CHUNK
    qw, vw = heads * dk, heads * dv
    assert qw == vw
    bidx = functools.partial(_bwd_chunk, nc_ctx=nc_ctx, nc_tot=nc)

    def specs(row):
        return [pl.BlockSpec((CHUNK, qw), lambda s: (row(s), 0)),
                pl.BlockSpec((CHUNK, qw), lambda s: (row(s), 1)),
                pl.BlockSpec((CHUNK, vw), lambda s: (row(s), 2))]

    def gate_specs(row, direction):
        return [pl.BlockSpec((CHUNK, LANES), lambda s: (row(s), beta_cb + 2 * direction)),
                pl.BlockSpec((CHUNK, LANES), lambda s: (row(s), beta_cb + 2 * direction + 1))]

    fwd = functools.partial(_fwd_chunk, nc_ctx=nc_ctx, nc_tot=nc)
    return dict(
        kernel=functools.partial(_gdn_scan_kernel, heads=heads, dk=dk, dv=dv),
        in_specs=(specs(fwd) + gate_specs(fwd, 0) + specs(bidx) + gate_specs(bidx, 1)
                  + [pl.BlockSpec((2, 1, LANES), lambda s: (0, 0, 0)),
                     pl.BlockSpec((2, 1, LANES), lambda s: (0, 0, 0))]),
        operands=(qkv, qkv, qkv, p, p, qkv, qkv, qkv, p, p, alog, dtb),
        out_specs=[pl.BlockSpec((CHUNK, vw), lambda s: (fwd(s), 0)),
                   pl.BlockSpec((CHUNK, vw), lambda s: (bidx(s), 0))],
        out_shape=[jax.ShapeDtypeStruct((m, vw), F32)] * 2,
        scratch_shapes=[pltpu.VMEM((2 * heads, dk, dv), F32)])


def _mlstm_scan_kernel(qf_ref, kf_ref, vf_ref, if_ref, ff_ref, qb_ref, kb_ref, vb_ref, ib_ref, fb_ref,
                       ibias_ref, fbias_ref, of_ref, ob_ref, c_sc, m_sc, *, heads, dqk, dv):
    @pl.when(pl.program_id(0) == 0)
    def _():
        c_sc[...] = jnp.zeros_like(c_sc)
        m_sc[...] = jnp.zeros_like(m_sc)

    ones_col = (lax.broadcasted_iota(jnp.int32, (CHUNK, LANES), 1) == 0).astype(F32)
    chains = []
    for direction, (q_ref, k_ref, v_ref, i_ref, f_ref, o_ref) in enumerate(
            ((qf_ref, kf_ref, vf_ref, if_ref, ff_ref, of_ref), (qb_ref, kb_ref, vb_ref, ib_ref, fb_ref, ob_ref))):
        incl, _ = _causal_masks(direction)
        log_i = i_ref[...] + ibias_ref[direction]
        log_f = -_softplus(-(f_ref[...] + fbias_ref[direction]))
        bcum = _dot_exact(incl.astype(F32), log_f)
        btot = bcum[CHUNK - 1:CHUNK, :] if direction == 0 else bcum[0:1, :]
        log_end = btot - bcum + log_i
        m_st = m_sc[direction]
        m_new = jnp.maximum(btot + m_st, jnp.max(log_end, axis=0, keepdims=True))
        m_sc[direction] = m_new
        carry = jnp.exp(btot + m_st - m_new)
        k_scale = jnp.exp(log_end - m_new)
        b_inter = bcum + m_st
        bcum_t = bcum.T
        log_i_t = log_i.T
        for h in range(heads):
            log_d = jnp.where(incl, bcum[:, h:h + 1] - bcum_t[h:h + 1, :] + log_i_t[h:h + 1, :], -jnp.inf)
            m_t = jnp.maximum(b_inter[:, h:h + 1], jnp.max(log_d, axis=-1, keepdims=True))
            chains.append(dict(
                q=q_ref[:, h * dqk:(h + 1) * dqk] * (dqk ** -0.5), k=k_ref[:, h * dqk:(h + 1) * dqk],
                v_ext=jnp.concatenate([v_ref[:, h * dv:(h + 1) * dv], ones_col], axis=1),
                p_intra=jnp.exp(log_d - m_t), w_inter=jnp.exp(b_inter[:, h:h + 1] - m_t), floor=jnp.exp(-m_t),
                k_scale=k_scale[:, h:h + 1], carry=carry[:, h:h + 1], o_ref=o_ref, h=h, slot=direction * heads + h))
        yield

    s = [_dot_nt(c["q"], c["k"]) * c["p_intra"] for c in chains]
    yield
    states = [c_sc[c["slot"]] for c in chains]
    inter = [_dot(c["q"], st) for c, st in zip(chains, states)]
    yield
    intra = [_dot(x, c["v_ext"]) for x, c in zip(s, chains)]
    yield
    upd = [_dot_tn(c["k"] * c["k_scale"], c["v_ext"]) for c in chains]
    yield
    for idx, (c, st, x, y, z) in enumerate(zip(chains, states, inter, intra, upd)):
        out = c["w_inter"] * x + y
        den = jnp.maximum(jnp.abs(out[:, dv:dv + 1]), c["floor"])
        c["o_ref"][:, c["h"] * dv:(c["h"] + 1) * dv] = out[:, :dv] / den
        c_sc[c["slot"]] = c["carry"] * st + z
        if idx % 2 == 1:
            yield


def _mlstm_scan_parts(p, q_cb, gate_cb, ib, fb, heads, dqk, dv, nc_ctx):
    m = p.shape[0]
    nc = m // CHUNK
    qw, vw = heads * dqk, heads * dv
    assert q_cb % qw == 0 and (q_cb + 2 * qw) % vw == 0
    bidx = functools.partial(_bwd_chunk, nc_ctx=nc_ctx, nc_tot=nc)
    fwd = functools.partial(_fwd_chunk, nc_ctx=nc_ctx, nc_tot=nc)
    qb, kb, vb = q_cb // qw, q_cb // qw + 1, (q_cb + 2 * qw) // vw
    gb = gate_cb // LANES

    def specs(row, direction):
        return [pl.BlockSpec((CHUNK, qw), lambda s: (row(s), qb)),
                pl.BlockSpec((CHUNK, qw), lambda s: (row(s), kb)),
                pl.BlockSpec((CHUNK, vw), lambda s: (row(s), vb)),
                pl.BlockSpec((CHUNK, LANES), lambda s: (row(s), gb + 2 * direction)),
                pl.BlockSpec((CHUNK, LANES), lambda s: (row(s), gb + 2 * direction + 1))]

    return dict(
        kernel=functools.partial(_mlstm_scan_kernel, heads=heads, dqk=dqk, dv=dv),
        in_specs=(specs(fwd, 0) + specs(bidx, 1)
                  + [pl.BlockSpec((2, 1, LANES), lambda s: (0, 0, 0)),
                     pl.BlockSpec((2, 1, LANES), lambda s: (0, 0, 0))]),
        operands=(p, p, p, p, p, p, p, p, p, p, ib, fb),
        out_specs=[pl.BlockSpec((CHUNK, vw), lambda s: (fwd(s), 0)),
                   pl.BlockSpec((CHUNK, vw), lambda s: (bidx(s), 0))],
        out_shape=[jax.ShapeDtypeStruct((m, vw), F32)] * 2,
        scratch_shapes=[pltpu.VMEM((2 * heads, dqk, dv + LANES), F32), pltpu.VMEM((2, 1, LANES), F32)])


def _fused_scans_kernel(*refs, parts):
    n_in = sum(part[1] for part in parts)
    n_out = sum(part[2] for part in parts)
    i0, o0, s0 = 0, n_in, n_in + n_out
    running = []
    for fn, ni, no, ns in parts:
        running.append(fn(*refs[i0:i0 + ni], *refs[o0:o0 + no], *refs[s0:s0 + ns]))
        i0, o0, s0 = i0 + ni, o0 + no, s0 + ns
    done = object()
    while running:
        running = [g for g in running if next(g, done) is not done]


def _fused_scans_call(nc, *scans):
    parts = tuple((sc["kernel"], len(sc["in_specs"]), len(sc["out_specs"]), len(sc["scratch_shapes"]))
                  for sc in scans)
    outs = pl.pallas_call(
        functools.partial(_fused_scans_kernel, parts=parts),
        grid=(nc,),
        in_specs=sum((list(sc["in_specs"]) for sc in scans), []),
        out_specs=sum((list(sc["out_specs"]) for sc in scans), []),
        out_shape=sum((list(sc["out_shape"]) for sc in scans), []),
        scratch_shapes=sum((list(sc["scratch_shapes"]) for sc in scans), []),
        compiler_params=_cparams(1),
        name="gdn_mlstm_scans",
    )(*sum((list(sc["operands"]) for sc in scans), []))
    split, res = 0, []
    for sc in scans:
        res.append(tuple(outs[split:split + len(sc["out_specs"])]))
        split += len(sc["out_specs"])
    return res


def _filter_kernel(feat_ref, w1_ref, b1_ref, w2_ref, b2_ref, w3_ref, freq_ref, dec_ref, o_ref, h_sc):
    @pl.when(pl.program_id(1) == 0)
    def _():
        h = jnp.sin(freq_ref[0:1, :] * (_dot(feat_ref[...], w1_ref[...]) + b1_ref[...]))
        h_sc[...] = jnp.sin(freq_ref[1:2, :] * (_dot(h, w2_ref[...]) + b2_ref[...]))

    o_ref[...] = _dot(h_sc[...], w3_ref[...]) * jnp.exp(-feat_ref[:, 0:1] * jnp.abs(dec_ref[...]))


def _filter_call(feats, w1, b1, w2, b2, w3, freq, dec, tl, tn):
    length, fp = feats.shape
    hid = w2.shape[0]
    n = w3.shape[1]
    return pl.pallas_call(
        _filter_kernel,
        grid=(length // tl, n // tn),
        in_specs=[pl.BlockSpec((tl, fp), lambda i, j: (i, 0)),
                  pl.BlockSpec((fp, hid), lambda i, j: (0, 0)),
                  pl.BlockSpec((1, hid), lambda i, j: (0, 0)),
                  pl.BlockSpec((hid, hid), lambda i, j: (0, 0)),
                  pl.BlockSpec((1, hid), lambda i, j: (0, 0)),
                  pl.BlockSpec((hid, tn), lambda i, j: (0, j)),
                  pl.BlockSpec((2, hid), lambda i, j: (0, 0)),
                  pl.BlockSpec((1, tn), lambda i, j: (0, j))],
        out_specs=pl.BlockSpec((tl, tn), lambda i, j: (i, j)),
        out_shape=jax.ShapeDtypeStruct((length, n), F32),
        scratch_shapes=[pltpu.VMEM((tl, hid), F32)],
        compiler_params=_cparams(2),
        name="hyena_filter",
    )(feats, w1, b1, w2, b2, w3, freq, dec)


def _dft_constants(length):
    n2 = DFT_N2
    n = 2 * length
    n1 = n // n2
    kh = n1 // 2
    na = kh + 1

    def cis(num, den):
        ang = (2.0 * math.pi / den) * (num % den).astype(F32)
        return jnp.cos(ang), jnp.sin(ang)

    ar = lambda size: jnp.arange(size, dtype=jnp.int32)
    c, s = cis(ar(na)[None, :, None] * (n2 * ar(kh)[None, None, :] + ar(n2)[:, None, None]), n)
    m1 = jnp.stack([c, -s], axis=2).reshape(n2, 2 * na, kh)
    c, s = cis(ar(n2)[:, None] * ar(n2)[None, :], n2)
    g2 = jnp.block([[c, s], [-s, c]])
    c, s = cis(ar(n2)[None, :, None] * (ar(na)[:, None, None] + n1 * ar(n2)[None, None, :]), n)
    minv = jnp.concatenate([jnp.concatenate([c, -s], axis=2), jnp.concatenate([s, c], axis=2)], axis=1)
    c, s = cis(ar(kh)[:, None] * ar(na)[None, :], n1)
    weight = jnp.where(jnp.logical_or(ar(na) == 0, ar(na) == kh), 1.0, 2.0)[None, :] / n
    pinv = jnp.stack([c * weight, -s * weight], axis=2).reshape(kh, 2 * na)
    return m1.astype(BF16), g2.astype(BF16), minv.astype(BF16), pinv.astype(BF16)


def _dft1_kernel(x_ref, m_ref, o_ref, *, n_lo_major):
    for j in range(SUBLANES):
        o_ref[:, j, :] = _dot(m_ref[j], x_ref[j] if n_lo_major else x_ref[:, j, :])


def _dft1_call(x3, col_blk0, width, m1, tc, n_lo_major=False):
    n2, rows, kh = m1.shape
    if n_lo_major:
        x_spec = pl.BlockSpec((SUBLANES, kh, tc), lambda c, g: (g, 0, col_blk0 + c))
    else:
        x_spec = pl.BlockSpec((kh, SUBLANES, tc), lambda c, g: (0, g, col_blk0 + c))
    return pl.pallas_call(
        functools.partial(_dft1_kernel, n_lo_major=n_lo_major),
        grid=(width // tc, n2 // SUBLANES),
        in_specs=[x_spec,
                  pl.BlockSpec((SUBLANES, rows, kh), lambda c, g: (g, 0, 0))],
        out_specs=pl.BlockSpec((rows, SUBLANES, tc), lambda c, g: (0, g, c)),
        out_shape=jax.ShapeDtypeStruct((rows, n2, width), F32),
        compiler_params=_cparams(2),
        name="dft_stage1",
    )(x3, m1)


def _filter_spectrum_kernel(af_ref, ab_ref, g_ref, o_ref):
    sf = _dot(g_ref[...], af_ref[...])
    sb = _dot(g_ref[...], ab_ref[...])
    sign = jnp.where(lax.broadcasted_iota(jnp.int32, (2 * DFT_N2, 1), 0) < DFT_N2, 1.0, -1.0)
    o_ref[...] = sf + sign * sb


def _filter_spectrum_call(a, g2, tc):
    rows, width = a.shape
    half = width // 2
    blk = 2 * DFT_N2
    return pl.pallas_call(
        _filter_spectrum_kernel,
        grid=(rows // blk, half // tc),
        in_specs=[pl.BlockSpec((blk, tc), lambda a_, j: (a_, j)),
                  pl.BlockSpec((blk, tc), lambda a_, j: (a_, half // tc + j)),
                  pl.BlockSpec((blk, blk), lambda a_, j: (0, 0))],
        out_specs=pl.BlockSpec((blk, tc), lambda a_, j: (a_, j)),
        out_shape=jax.ShapeDtypeStruct((rows, half), F32),
        compiler_params=_cparams(2),
        name="hyena_filter_spectrum",
    )(a, a, g2)


def _complex_mul(x, kf, n):
    xr, xi = x[:n], x[n:]
    kr, ki = kf[:n], kf[n:]
    return jnp.concatenate([xr * kr - xi * ki, xr * ki + xi * kr], axis=0)


def _spectral_mid_kernel(a_ref, g_ref, kf_ref, minv_ref, o_ref):
    x = _dot(g_ref[...], a_ref[...])
    o_ref[...] = _dot(minv_ref[0], _complex_mul(x, kf_ref[...], DFT_N2))


def _spectral_mid_call(a, g2, kf, kf_cb, minv, tc):
    rows, width = a.shape
    blk = 2 * DFT_N2
    return pl.pallas_call(
        _spectral_mid_kernel,
        grid=(rows // blk, width // tc),
        in_specs=[pl.BlockSpec((blk, tc), lambda a_, j: (a_, j)),
                  pl.BlockSpec((blk, blk), lambda a_, j: (0, 0)),
                  pl.BlockSpec((blk, tc), lambda a_, j: (a_, kf_cb + j)),
                  pl.BlockSpec((1, blk, blk), lambda a_, j: (a_, 0, 0))],
        out_specs=pl.BlockSpec((blk, tc), lambda a_, j: (a_, j)),
        out_shape=jax.ShapeDtypeStruct((rows, width), F32),
        compiler_params=_cparams(2),
        name="hyena_spectral_mid",
    )(a, g2, kf, minv)


def _idft2_kernel(z_ref, p_ref, v_ref, gate_ref, bias_ref, o_ref, *, v_major, out_major):
    for j in range(SUBLANES):
        y = _dot(p_ref[...], z_ref[:, j, :])
        v = v_ref[j] if v_major else v_ref[:, j, :]
        res = gate_ref[:, j, :] * (y + v * bias_ref[...])
        if out_major:
            o_ref[j] = res
        else:
            o_ref[:, j, :] = res


def _idft2_call(z3, pinv, v3, v_cb, v_major, gate3, gate_cb, bias, width, tc, out_major):
    rows, n2, _ = z3.shape
    kh = pinv.shape[0]

    def time_spec(major, cb):
        if major:
            return pl.BlockSpec((SUBLANES, kh, tc), lambda c, g: (g, 0, cb + c))
        return pl.BlockSpec((kh, SUBLANES, tc), lambda c, g: (0, g, cb + c))

    return pl.pallas_call(
        functools.partial(_idft2_kernel, v_major=v_major, out_major=out_major),
        grid=(width // tc, n2 // SUBLANES),
        in_specs=[pl.BlockSpec((rows, SUBLANES, tc), lambda c, g: (0, g, c)),
                  pl.BlockSpec((kh, rows), lambda c, g: (0, 0)),
                  time_spec(v_major, v_cb),
                  time_spec(False, gate_cb),
                  pl.BlockSpec((1, tc), lambda c, g: (0, c))],
        out_specs=time_spec(out_major, 0),
        out_shape=jax.ShapeDtypeStruct((n2, kh, width) if out_major else (kh, n2, width), F32),
        compiler_params=_cparams(2),
        name="idft_stage2_gate",
    )(z3, pinv, v3, gate3, bias)


def _long_conv(v3, v_cb, v_major, gate3, gate_cb, bias, kf, kf_cb, consts, width, tc, out_major):
    m1, g2, minv, pinv = consts
    a = _dft1_call(v3, v_cb, width, m1, tc, n_lo_major=v_major)
    rows = a.shape[0]
    z = _spectral_mid_call(a.reshape(rows * DFT_N2, width), g2, kf, kf_cb, minv, tc)
    return _idft2_call(z.reshape(rows, DFT_N2, width), pinv, v3, v_cb, v_major, gate3, gate_cb, bias, width, tc,
                       out_major)


def _dense_dft_constants(length):
    n = 2 * length
    k = jnp.arange(n, dtype=jnp.int32)[:, None]
    t = jnp.arange(length, dtype=jnp.int32)[None, :]
    ang = (2.0 * math.pi / n) * ((k * t) % n).astype(F32)
    c, s = jnp.cos(ang), jnp.sin(ang)
    fwd = jnp.concatenate([c, -s], axis=0)
    inv = jnp.concatenate([c.T, -s.T], axis=1) / n
    return fwd.astype(BF16), inv.astype(BF16)


def _dense_spectrum_kernel(tf_ref, tb_ref, f_ref, o_ref, *, n):
    sign = jnp.where(lax.broadcasted_iota(jnp.int32, (2 * n, 1), 0) < n, 1.0, -1.0)
    o_ref[...] = _dot(f_ref[...], tf_ref[...]) + sign * _dot(f_ref[...], tb_ref[...])


def _dense_spectrum_call(taps, fwd, tc):
    length, width = taps.shape
    half = width // 2
    rows = fwd.shape[0]
    return pl.pallas_call(
        functools.partial(_dense_spectrum_kernel, n=rows // 2),
        grid=(half // tc,),
        in_specs=[pl.BlockSpec((length, tc), lambda j: (0, j)),
                  pl.BlockSpec((length, tc), lambda j: (0, half // tc + j)),
                  pl.BlockSpec((rows, length), lambda j: (0, 0))],
        out_specs=pl.BlockSpec((rows, tc), lambda j: (0, j)),
        out_shape=jax.ShapeDtypeStruct((rows, half), F32),
        compiler_params=_cparams(1),
        name="hyena_filter_spectrum_dense",
    )(taps, taps, fwd)


def _dense_hyena_kernel(v_ref, x1_ref, x2_ref, kf0_ref, kf1_ref, f_ref, inv_ref, b0_ref, b1_ref, o_ref, *, n):
    def conv(u, kf_ref, b_ref):
        y = _dot(inv_ref[...], _complex_mul(_dot(f_ref[...], u), kf_ref[...], n))
        return y + u * b_ref[...]

    z = x1_ref[...] * conv(v_ref[...], kf0_ref, b0_ref)
    o_ref[...] = x2_ref[...] * conv(z, kf1_ref, b1_ref)


def _dense_hyena_call(hc, length, width, kf, fwd, inv, bias0, bias1, tc):
    rows = fwd.shape[0]
    per = width // tc
    return pl.pallas_call(
        functools.partial(_dense_hyena_kernel, n=rows // 2),
        grid=(per,),
        in_specs=[pl.BlockSpec((length, tc), lambda j: (0, j)),
                  pl.BlockSpec((length, tc), lambda j: (0, per + j)),
                  pl.BlockSpec((length, tc), lambda j: (0, 2 * per + j)),
                  pl.BlockSpec((rows, tc), lambda j: (0, j)),
                  pl.BlockSpec((rows, tc), lambda j: (0, per + j)),
                  pl.BlockSpec((rows, length), lambda j: (0, 0)),
                  pl.BlockSpec((length, rows), lambda j: (0, 0)),
                  pl.BlockSpec((1, tc), lambda j: (0, j)),
                  pl.BlockSpec((1, tc), lambda j: (0, j))],
        out_specs=pl.BlockSpec((length, tc), lambda j: (0, j)),
        out_shape=jax.ShapeDtypeStruct((length, width), F32),
        compiler_params=_cparams(1),
        name="hyena_dense",
    )(hc, hc, hc, kf, kf, fwd, inv, bias0, bias1)


def _filter_features(length):
    t = jnp.linspace(0.0, 1.0, length, dtype=F32)[:, None]
    w = (2.0 * math.pi / length) * jnp.arange(length, dtype=F32)[:, None]
    bands = jnp.linspace(1e-4, HYENA_BANDS - 1, HYENA_BANDS, dtype=F32)[None, :]
    feats = jnp.concatenate([t, jnp.cos(bands * w), -jnp.sin(bands * w)], axis=-1)
    return jnp.pad(feats, ((0, 0), (0, LANES - feats.shape[1])))


def _head_rms(x, heads, width):
    outs = []
    for h in range(heads):
        xh = x[:, h * width:(h + 1) * width]
        outs.append(xh * lax.rsqrt(jnp.mean(xh * xh, axis=-1, keepdims=True) + RMS_EPS))
    return jnp.concatenate(outs, axis=1)


def _merge_kernel(ogf_ref, ogb_ref, omf_ref, omb_ref, z_ref, o_ref, hyl_ref, hyc_ref, g0_ref, g1_ref, g2_ref, gn_ref,
                  mn_ref, wa_ref, wb_ref, wc_ref, out_ref, *, gdn_heads, gdn_dv, ml_heads, ml_dv, lat_tiles):
    a = _head_rms(ogf_ref[...] + ogb_ref[...], gdn_heads, gdn_dv) * gn_ref[...] * _silu(z_ref[...])
    b = _head_rms(omf_ref[...] + omb_ref[...], ml_heads, ml_dv) * mn_ref[...] * _sigmoid(o_ref[...])
    hy = jnp.where(pl.program_id(0) < lat_tiles, hyl_ref[...], hyc_ref[...])
    out_ref[...] = (_sigmoid(g0_ref[...]) * _dot(a, wa_ref[...]) + _sigmoid(g1_ref[...]) * _dot(b, wb_ref[...])
                    + _sigmoid(g2_ref[...]) * _dot(hy, wc_ref[...]))


def _merge_call(og, om, p, z_cb, o_cb, hy, gate_cb, gn, mn, wa, wb, wc, layer, heads, tm):
    m, d = p.shape[0], wa.shape[-1]
    hy_lat, hy_ctx = hy
    gw, mw, hw = og[0].shape[-1], om[0].shape[-1], hy_lat.shape[-1]
    gdn_heads, gdn_dv, ml_heads, ml_dv = heads
    zb, ob, gb = z_cb // gw, o_cb // mw, gate_cb // d
    assert hy_lat.shape[0] % tm == 0 and hy_ctx.shape[0] % tm == 0
    lat_tiles = hy_lat.shape[0] // tm
    return pl.pallas_call(
        functools.partial(_merge_kernel, gdn_heads=gdn_heads, gdn_dv=gdn_dv, ml_heads=ml_heads, ml_dv=ml_dv,
                          lat_tiles=lat_tiles),
        grid=(m // tm,),
        in_specs=[pl.BlockSpec((tm, gw), lambda i: (i, 0)),
                  pl.BlockSpec((tm, gw), lambda i: (i, 0)),
                  pl.BlockSpec((tm, mw), lambda i: (i, 0)),
                  pl.BlockSpec((tm, mw), lambda i: (i, 0)),
                  pl.BlockSpec((tm, gw), lambda i: (i, zb)),
                  pl.BlockSpec((tm, mw), lambda i: (i, ob)),
                  pl.BlockSpec((tm, hw), lambda i: (jnp.minimum(i, lat_tiles - 1), 0)),
                  pl.BlockSpec((tm, hw), lambda i: (jnp.maximum(i - lat_tiles, 0), 0)),
                  pl.BlockSpec((tm, d), lambda i: (i, gb)),
                  pl.BlockSpec((tm, d), lambda i: (i, gb + 1)),
                  pl.BlockSpec((tm, d), lambda i: (i, gb + 2)),
                  pl.BlockSpec((1, gw), lambda i: (0, 0)),
                  pl.BlockSpec((1, mw), lambda i: (0, 0)),
                  pl.BlockSpec((None, gw, d), lambda i: (layer, 0, 0), pipeline_mode=pl.Buffered(1)),
                  pl.BlockSpec((None, mw, d), lambda i: (layer, 0, 0), pipeline_mode=pl.Buffered(1)),
                  pl.BlockSpec((None, hw, d), lambda i: (layer, 0, 0), pipeline_mode=pl.Buffered(1))],
        out_specs=pl.BlockSpec((tm, d), lambda i: (i, 0)),
        out_shape=jax.ShapeDtypeStruct((m, d), F32),
        compiler_params=_cparams(1),
        name="branch_merge",
    )(og[0], og[1], om[0], om[1], p, p, hy_lat, hy_ctx, p, p, p, gn, mn, wa, wb, wc)


def _outproj_kernel(y_ref, x_ref, m_ref, w_ref, o_ref, *, tm, lat_len):
    gate = jnp.where(_row_is_ctx(pl.program_id(0) * tm, tm, lat_len), m_ref[0:1, :], m_ref[1:2, :])
    o_ref[...] = x_ref[...] + gate * _dot(y_ref[...], w_ref[...])


def _outproj_call(y, x, mods, w, layer, lat_len, tm):
    m, d = x.shape
    return pl.pallas_call(
        functools.partial(_outproj_kernel, tm=tm, lat_len=lat_len),
        grid=(m // tm,),
        in_specs=[pl.BlockSpec((tm, d), lambda i: (i, 0)),
                  pl.BlockSpec((tm, d), lambda i: (i, 0)),
                  pl.BlockSpec((SUBLANES, d), lambda i: (0, 0)),
                  pl.BlockSpec((None, d, d), lambda i: (layer, 0, 0))],
        out_specs=pl.BlockSpec((tm, d), lambda i: (i, 0)),
        out_shape=jax.ShapeDtypeStruct((m, d), F32),
        compiler_params=_cparams(1),
        name="mixer_out_proj",
    )(y, x, mods, w)


def _final_norm_kernel(x_ref, g_ref, o_ref):
    x = x_ref[...]
    o_ref[...] = x * lax.rsqrt(jnp.mean(x * x, axis=-1, keepdims=True) + RMS_EPS) * g_ref[...]


def _final_norm_call(x, g, row0, rows, tm):
    d = x.shape[1]
    return pl.pallas_call(
        _final_norm_kernel,
        grid=(rows // tm,),
        in_specs=[pl.BlockSpec((tm, d), lambda i: (row0 // tm + i, 0)),
                  pl.BlockSpec((1, d), lambda i: (0, 0))],
        out_specs=pl.BlockSpec((tm, d), lambda i: (i, 0)),
        out_shape=jax.ShapeDtypeStruct((rows, d), F32),
        compiler_params=_cparams(1),
        name="final_rms_norm",
    )(x, g)


def _regroup_kernel(*refs, shift, width, n_blk, src_col, src_rows, src_cols, masked):
    o_ref = refs[-1]
    val = refs[0][...]
    if n_blk > 1:
        val = jnp.concatenate([val, refs[1][...]], axis=1)[:, shift:shift + width]
    if masked:
        row = pl.program_id(1) * val.shape[0] + lax.broadcasted_iota(jnp.int32, val.shape, 0)
        col = src_col + pl.program_id(2) * width + lax.broadcasted_iota(jnp.int32, val.shape, 1)
        val = jnp.where(jnp.logical_and(row < src_rows, col < src_cols), val, 0.0)
    o_ref[...] = val.astype(o_ref.dtype)


def _regroup_call(src, buf, src_col, dst_col, ncols, out_rows, n_total, tcr, tb=None):
    depth, src_rows, src_cols = src.shape
    q, shift = divmod(src_col, LANES)
    per = tcr // LANES
    assert q % per == 0
    last = pl.cdiv(src_cols, LANES) - 1
    tb = tb or _pick_tile(out_rows, (1024, 512, 256, 128))
    assert out_rows % tb == 0
    last_rb = pl.cdiv(src_rows, tb) - 1
    in_specs = [pl.BlockSpec((None, tb, tcr), lambda l, i, t: (l, jnp.minimum(i, last_rb), q // per + t))]
    if shift:
        in_specs.append(pl.BlockSpec((None, tb, LANES),
                                     lambda l, i, t: (l, jnp.minimum(i, last_rb), jnp.minimum(q + per * (t + 1), last))))
    n_blk = len(in_specs)
    operands = [src] * n_blk
    aliases = {}
    if buf is not None:
        in_specs.append(pl.BlockSpec(memory_space=pl.ANY))
        operands.append(buf)
        aliases = {n_blk: 0}
    return pl.pallas_call(
        functools.partial(_regroup_kernel, shift=shift, width=tcr, n_blk=n_blk, src_col=src_col, src_rows=src_rows,
                          src_cols=src_cols, masked=out_rows > src_rows or src_col + ncols > src_cols),
        grid=(depth, out_rows // tb, ncols // tcr),
        in_specs=in_specs,
        out_specs=pl.BlockSpec((None, tb, tcr), lambda l, i, t: (l, i, dst_col // tcr + t)),
        out_shape=jax.ShapeDtypeStruct((depth, out_rows, n_total), BF16),
        input_output_aliases=aliases,
        compiler_params=_cparams(3),
        name="weight_regroup",
    )(*operands)


ROW_HALO = 64


def _regroup_rows_kernel(*refs, shift):
    o_ref = refs[-1]
    val = refs[0][...]
    if shift:
        val = jnp.concatenate([val, refs[1][...]], axis=0)[shift:shift + val.shape[0]]
    o_ref[...] = val.astype(o_ref.dtype)


def _regroup_rows_call(src, buf, src_row, dst_row, nrows, n_total, tr):
    depth, src_rows, d = src.shape
    qb, shift = divmod(src_row, tr)
    assert dst_row % tr == 0 and nrows % tr == 0 and shift % SUBLANES == 0 and shift <= ROW_HALO
    assert tr % ROW_HALO == 0 and src_row + nrows <= src_rows
    last = pl.cdiv(src_rows, ROW_HALO) - 1
    in_specs = [pl.BlockSpec((None, tr, d), lambda l, t: (l, qb + t, 0))]
    if shift:
        in_specs.append(pl.BlockSpec((None, ROW_HALO, d),
                                     lambda l, t: (l, jnp.minimum((qb + t + 1) * (tr // ROW_HALO), last), 0)))
    operands = [src] * len(in_specs)
    aliases = {}
    if buf is not None:
        aliases = {len(in_specs): 0}
        in_specs.append(pl.BlockSpec(memory_space=pl.ANY))
        operands.append(buf)
    return pl.pallas_call(
        functools.partial(_regroup_rows_kernel, shift=shift),
        grid=(depth, nrows // tr),
        in_specs=in_specs,
        out_specs=pl.BlockSpec((None, tr, d), lambda l, t: (l, dst_row // tr + t, 0)),
        out_shape=jax.ShapeDtypeStruct((depth, n_total, d), BF16),
        input_output_aliases=aliases,
        compiler_params=_cparams(2),
        name="weight_row_regroup",
    )(*operands)


def _gate_rows_kernel(*refs, offs, heads):
    o_ref = refs[-1]
    row = lax.broadcasted_iota(jnp.int32, refs[0].shape, 0)
    tiles = []
    for g, (off, h) in enumerate(zip(offs, heads)):
        t8 = jnp.where(jnp.logical_and(row >= off, row < off + h), refs[g][...], 0.0)
        if off:
            t8 = pltpu.roll(t8, SUBLANES - off, axis=0)
        tiles += [t8, jnp.zeros((LANES - SUBLANES, t8.shape[1]), t8.dtype)]
    o_ref[...] = jnp.concatenate(tiles, axis=0).astype(o_ref.dtype)


def _gate_rows_call(src, buf, rows, heads, dst_row):
    depth, _, d = src.shape
    n_total = buf.shape[1]
    height = len(rows) * LANES
    assert dst_row % height == 0 and all(r // SUBLANES == (r + h - 1) // SUBLANES for r, h in zip(rows, heads))
    in_specs = [pl.BlockSpec((None, SUBLANES, d), lambda l, blk=r // SUBLANES: (l, blk, 0)) for r in rows]
    in_specs.append(pl.BlockSpec(memory_space=pl.ANY))
    return pl.pallas_call(
        functools.partial(_gate_rows_kernel, offs=tuple(r % SUBLANES for r in rows), heads=tuple(heads)),
        grid=(depth,),
        in_specs=in_specs,
        out_specs=pl.BlockSpec((None, height, d), lambda l: (l, dst_row // height, 0)),
        out_shape=jax.ShapeDtypeStruct((depth, n_total, d), BF16),
        input_output_aliases={len(rows): 0},
        compiler_params=_cparams(1),
        name="gate_weight_groups",
    )(*([src] * len(rows)), buf)


def _pack_rows(rows, d):
    out = jnp.concatenate([r.reshape(1, d) for r in rows], axis=0)
    return jnp.pad(out, ((0, SUBLANES - out.shape[0]), (0, 0)))


def _lane_row(vals, lanes=LANES):
    return jnp.pad(vals.astype(F32), ((0, 0), (0, lanes - vals.shape[1])))[:, None, :]


def _pick_tile(total, candidates):
    for c in candidates:
        if total % c == 0:
            return c
    raise ValueError(f"no tile for {total} among {candidates}")


def kernel(x, c, ctx, c_ctx, w_ada, b_ada, norm_g, ffn_w_gate, ffn_w_up, ffn_w_down, w_in, gdn_conv, gdn_a_log,
           gdn_dt_bias, gdn_norm, mlstm_i_bias, mlstm_f_bias, mlstm_norm, hy_short_w, hy_short_b, hf_w1, hf_b1,
           hf_w2, hf_b2, hf_w3, hf_freq, hf_decay, hy_bias, w_a_out, w_b_out, w_c_out, w_out, final_norm):
    batch, seq, d = x.shape
    assert batch == 1 and c.shape[0] == 1 and ctx.shape[0] == 1
    ctx_len = ctx.shape[1]
    depth = w_ada.shape[0]
    m = ctx_len + seq
    assert ctx_len % CHUNK == 0 and seq % CHUNK == 0

    gdn_heads = gdn_a_log.shape[-1]
    gdn_vw = w_a_out.shape[1]
    gdn_qkw = (gdn_conv.shape[1] - gdn_vw) // 2
    gdn_dk, gdn_dv = gdn_qkw // gdn_heads, gdn_vw // gdn_heads
    ml_heads = mlstm_i_bias.shape[-1]
    ml_vw = w_b_out.shape[1]
    hy_w = w_c_out.shape[1]
    n_in = w_in.shape[-1]
    ml_qkw = (n_in - (2 * gdn_qkw + gdn_vw) - 4 * gdn_heads - 4 * ml_heads - ml_vw - gdn_vw - ml_vw
              - 3 * hy_w - 3 * d) // 2
    ml_dqk, ml_dv = ml_qkw // ml_heads, ml_vw // ml_heads
    assert gdn_dk == LANES and gdn_dv == LANES and ml_dqk == LANES and ml_dv % LANES == 0

    o_gqkv = 0
    o_gbeta = o_gqkv + 2 * gdn_qkw + gdn_vw
    o_ga = o_gbeta + 2 * gdn_heads
    o_mqkv = o_ga + 2 * gdn_heads
    o_mi = o_mqkv + 2 * ml_qkw + ml_vw
    o_mf = o_mi + 2 * ml_heads
    o_z = o_mf + 2 * ml_heads
    o_o = o_z + gdn_vw
    o_hy = o_o + ml_vw
    o_gates = o_hy + 3 * hy_w
    assert o_gates + 3 * d == n_in

    c_gqkv = 0
    c_mqkv = c_gqkv + 2 * gdn_qkw + gdn_vw
    c_z = c_mqkv + 2 * ml_qkw + ml_vw
    c_o = c_z + gdn_vw
    c_hy = c_o + ml_vw
    c_gates = c_hy + 3 * hy_w
    c_gsmall = c_gates + 3 * d
    c_msmall = c_gsmall + 4 * LANES
    n_p = c_msmall + 4 * LANES

    w_in_t = jnp.swapaxes(w_in, 1, 2)
    moves = [(o_gqkv, c_gqkv, c_mqkv - c_gqkv), (o_mqkv, c_mqkv, c_z - c_mqkv), (o_z, c_z, c_gsmall - c_z)]
    tr = 512 if all(dst % 512 == 0 and width % 512 == 0 for _, dst, width in moves) else 256
    w_in_p = None
    for src_row, dst_row, height in moves:
        w_in_p = _regroup_rows_call(w_in_t, w_in_p, src_row, dst_row, height, n_p, tr)
    gate_cols, gate_heads = [], []
    for offs, heads in (((o_gbeta, o_ga), gdn_heads), ((o_mi, o_mf), ml_heads)):
        for direction in range(2):
            for off in offs:
                gate_cols.append(off + direction * heads)
                gate_heads.append(heads)
    per_call = 1
    while per_call < len(gate_cols) and c_gsmall % (2 * per_call * LANES) == 0:
        per_call *= 2
    for g0 in range(0, len(gate_cols), per_call):
        w_in_p = _gate_rows_call(w_in_t, w_in_p, gate_cols[g0:g0 + per_call], gate_heads[g0:g0 + per_call],
                                 c_gsmall + g0 * LANES)

    f_hidden = ffn_w_gate.shape[-1]
    tf = 512
    f_pad = -(-f_hidden // tf) * tf
    wide = f_pad // 2 if f_pad % (2 * LANES) == 0 else tf
    wg = _regroup_call(ffn_w_gate.reshape(2 * depth, d, f_hidden), None, 0, 0, f_pad, d, f_pad, wide)
    wu = _regroup_call(ffn_w_up.reshape(2 * depth, d, f_hidden), None, 0, 0, f_pad, d, f_pad, wide)
    wd = _regroup_call(ffn_w_down.reshape(2 * depth, f_hidden, d), None, 0, 0, d, f_pad, d, d,
                       tb=f_pad // 4 if f_pad % (4 * 2 * SUBLANES) == 0 else None)
    wg, wu = wg.reshape(depth, 2, d, f_pad), wu.reshape(depth, 2, d, f_pad)
    wd = wd.reshape(depth, 2, f_pad, d)
    wa, wb, wc, wo = (t.astype(BF16) for t in (w_a_out, w_b_out, w_c_out, w_out))

    tm_big = _pick_tile(m, (768, 512, 384, 256, 128, 64))
    tm_mid = _pick_tile(m, (256, 128, 64))
    tm_proj = _pick_tile(m, (1056, 768, 512, 384, 256, 128, 64))
    tm_seg = _pick_tile(math.gcd(ctx_len, seq), (256, 128, 64))
    tn_in = _pick_tile(n_p, (1024, 512, 256, 128))
    nc_ctx = ctx_len // CHUNK

    cond = jnp.pad(jnp.concatenate([c_ctx[None, :], c], axis=0), ((0, SUBLANES - 2), (0, 0)))
    mods = _modulation_call(cond, w_ada, b_ada)

    dft_consts = _dft_constants(seq)
    dense_fwd, dense_inv = _dense_dft_constants(ctx_len)
    feats_lat = _filter_features(seq)
    feats_lat = feats_lat.reshape(seq // DFT_N2, DFT_N2, LANES).swapaxes(0, 1).reshape(seq, LANES)
    feats_ctx = _filter_features(ctx_len)
    tl = _pick_tile(seq, (512, 256, 128, 64))

    s = jnp.concatenate([x[0], ctx[0]], axis=0)
    for l in range(depth):
        last = l == depth - 1

        def mod(idx):
            return [mods[l, 0, idx * d:(idx + 1) * d], mods[l, 1, idx * d:(idx + 1) * d]]

        s = _ffn_call(s, _pack_rows(mod(0) + mod(1) + mod(2), d), norm_g[l, 0][None, :], wg, wu, wd, l, 0,
                      seq, tm_big, tf)
        p = _inproj_call(s, _pack_rows(mod(3) + mod(4), d), norm_g[l, 1][None, :], w_in_p, l, seq, tm_proj, tn_in)

        gqkv = _dwconv_call(p, c_gqkv, 2 * gdn_qkw + gdn_vw, jnp.transpose(gdn_conv[l]),
                            jnp.zeros((1, 2 * gdn_qkw + gdn_vw), F32), seq, tm_seg, 1024, True)
        hc = _dwconv_call(p, c_hy, 3 * hy_w, jnp.transpose(hy_short_w[l]), hy_short_b[l][None, :],
                          seq, tm_seg, 1024, False)

        og, om = _fused_scans_call(
            m // CHUNK,
            _gdn_scan_parts(gqkv, p, c_gsmall // LANES, _lane_row(gdn_a_log[l]), _lane_row(gdn_dt_bias[l]),
                            gdn_heads, gdn_dk, gdn_dv, nc_ctx),
            _mlstm_scan_parts(p, c_mqkv, c_msmall, _lane_row(mlstm_i_bias[l]), _lane_row(mlstm_f_bias[l]),
                              ml_heads, ml_dqk, ml_dv, nc_ctx))

        fargs = (jnp.pad(hf_w1[l], ((0, LANES - hf_w1.shape[1]), (0, 0))), hf_b1[l][None, :], hf_w2[l],
                 hf_b2[l][None, :], hf_w3[l], hf_freq[l], jnp.tile(hf_decay[l], 2)[None, :])
        taps = _filter_call(feats_lat, *fargs, tl, 1024)
        m1, g2, minv, pinv = dft_consts
        kh = seq // DFT_N2
        ta = _dft1_call(taps.reshape(DFT_N2, kh, taps.shape[1]), 0, taps.shape[1], m1, 1024, n_lo_major=True)
        kf = _filter_spectrum_call(ta.reshape(ta.shape[0] * DFT_N2, taps.shape[1]), g2, 1024)
        hl3 = hc.reshape(m // DFT_N2, DFT_N2, 3 * hy_w)
        tc_h = min(hy_w, 1024)
        per = hy_w // tc_h
        z1 = _long_conv(hl3, 0, False, hl3, per, hy_bias[l, 0][None, :], kf, 0, dft_consts, hy_w, tc_h, True)
        z2 = _long_conv(z1, 0, True, hl3, 2 * per, hy_bias[l, 1][None, :], kf, per, dft_consts, hy_w, tc_h, False)
        z2 = z2.reshape(seq, hy_w)
        if not last:
            taps_c = _filter_call(feats_ctx, *fargs, _pick_tile(ctx_len, (256, 128, 64)), 1024)
            kf_c = _dense_spectrum_call(taps_c, dense_fwd, 512)
            zc = _dense_hyena_call(hc[seq:], ctx_len, hy_w, kf_c, dense_fwd, dense_inv, hy_bias[l, 0][None, :],
                                   hy_bias[l, 1][None, :], 256)
        else:
            zc = jnp.zeros((ctx_len, hy_w), F32)
        hy = (z2, zc)

        merged = _merge_call(og, om, p, c_z, c_o, hy, c_gates, jnp.tile(gdn_norm[l], gdn_heads)[None, :],
                             mlstm_norm[l][None, :], wa, wb, wc, l, (gdn_heads, gdn_dv, ml_heads, ml_dv), tm_mid)
        s = _outproj_call(merged, s, _pack_rows(mod(5), d), wo, l, seq, tm_mid)
        s = _ffn_call(s, _pack_rows(mod(6) + mod(7) + mod(8), d), norm_g[l, 2][None, :], wg, wu, wd, l, 1,
                      seq, tm_big, tf)

    out = _final_norm_call(s, final_norm[None, :], 0, seq, tm_seg)
    return out[None]
```
